```python
import math
import jax, jax.numpy as jnp
from jax import lax
import numpy as np

D_MODEL = 4096
BATCH = 8
SEQ = 2048
DEPTH = 2

HEAD_DIM = 128
CONV_CH = D_MODEL // 2
CONV_TAPS = 3
NSA_HEADS = (D_MODEL // 2) // HEAD_DIM
NSA_KV_HEADS = 4
NSA_HPG = NSA_HEADS // NSA_KV_HEADS
KV_W = NSA_KV_HEADS * HEAD_DIM
N_BRANCH = 3
CMP_BLOCK = 32
CMP_STRIDE = 16
SEL_BLOCK = 64
N_SELECT = 16
WINDOW = 512
Q_BLOCK = 128
SEL_Q_CHUNK = 32
FORCE_SCORE = 1e6
NEG_INF = -1e30
REL_BUCKETS = 32
REL_MAX_DIST = 128
POOL_WINDOWS = (2, 4, 8, 16)
POOL_GROUP = D_MODEL // 4
D_FF_DENSE = 11008
N_EXPERTS = 8
TOP_K = 2
D_FF_EXPERT = 6144
EPS = 1e-6
MIX_W = CONV_CH + NSA_HEADS * HEAD_DIM
SPLIT_SIZES = (CONV_CH, CONV_CH, CONV_CH, NSA_HEADS * HEAD_DIM, KV_W, KV_W, KV_W, KV_W, KV_W, KV_W, NSA_HEADS * N_BRANCH)
IN_COLS = 3 * CONV_CH + NSA_HEADS * HEAD_DIM + 6 * KV_W + NSA_HEADS * N_BRANCH
N_EVEN = (DEPTH + 1) // 2
N_ODD = DEPTH // 2

kernel_name = 'hybrid_shortconv_nsa_pool_moe_block'


def rmsnorm(x, w):
    x32 = x.astype(jnp.float32)
    y = x32 * lax.rsqrt(jnp.mean(x32 * x32, axis=-1, keepdims=True) + EPS)
    return (y * w.astype(jnp.float32)).astype(x.dtype)


def rel_bucket(dist):
    dist = jnp.maximum(dist, 0)
    exact = REL_BUCKETS // 2
    d = jnp.maximum(dist, exact).astype(jnp.float32)
    large = exact + (jnp.log(d / exact) / math.log(REL_MAX_DIST / exact) * (REL_BUCKETS - exact)).astype(jnp.int32)
    return jnp.where(dist < exact, dist, jnp.minimum(large, REL_BUCKETS - 1))


def masked_softmax(logits, mask):
    z = jnp.where(mask, logits.astype(jnp.float32), NEG_INF)
    return jax.nn.softmax(z, axis=-1) * mask.astype(jnp.float32)


def swiglu(h, w1, w3, w2):
    return (jax.nn.silu(h @ w1) * (h @ w3)) @ w2


def split_cols(p):
    outs, off = [], 0
    for n in SPLIT_SIZES:
        outs.append(p[..., off:off + n])
        off += n
    return outs


def short_conv_mixer(gate_b, gate_c, u, conv_w):
    v = gate_c * u
    s = v.shape[1]
    vp = jnp.pad(v, ((0, 0), (CONV_TAPS - 1, 0), (0, 0)))
    y = sum(conv_w[:, k] * vp[:, k:k + s] for k in range(CONV_TAPS))
    return gate_b * y


def compress_blocks(kv, pe, w):
    b, s, g, dk = kv.shape
    ratio = CMP_BLOCK // CMP_STRIDE
    nc = s // CMP_STRIDE - ratio + 1
    ch = kv.reshape(b, s // CMP_STRIDE, CMP_STRIDE, g, dk)
    blk = jnp.concatenate([ch[:, j:j + nc] for j in range(ratio)], axis=2) + pe[None, :, None, :]
    return jnp.einsum('bnlgd,lde->bgne', blk, w)


def band_blocks(kv):
    b, g, s, dk = kv.shape
    nb = s // Q_BLOCK
    pb = WINDOW // Q_BLOCK
    kp = jnp.pad(kv, ((0, 0), (0, 0), (WINDOW, 0), (0, 0))).reshape(b, g, nb + pb, Q_BLOCK, dk)
    return jnp.concatenate([kp[:, :, j:j + nb] for j in range(pb + 1)], axis=3)


def nsa_mixer(q, k_cmp, v_cmp, k_sel, v_sel, k_win, v_win, gates,
              q_norm_w, k_norm_w, pe_k, pe_v, w_cmp_k, w_cmp_v, rel_table):
    b, s = q.shape[:2]
    g, hpg, dk = NSA_KV_HEADS, NSA_HPG, HEAD_DIM
    scale = dk ** -0.5
    t = jnp.arange(s)
    qh = rmsnorm(q, q_norm_w).reshape(b, s, g, hpg, dk).transpose(0, 2, 3, 1, 4)
    tbl = rel_table.T.reshape(g, hpg, REL_BUCKETS)

    kc = rmsnorm(compress_blocks(k_cmp, pe_k, w_cmp_k), k_norm_w)
    vc = compress_blocks(v_cmp, pe_v, w_cmp_v)
    nc = kc.shape[2]
    c_start = jnp.arange(nc) * CMP_STRIDE
    c_end = c_start + CMP_BLOCK - 1
    dist_c = t[:, None] - c_end[None, :]
    bias_c = rel_table[rel_bucket(dist_c)].transpose(2, 0, 1).reshape(g, hpg, s, nc)
    logit_c = jnp.einsum('bghqd,bgkd->bghqk', qh, kc).astype(jnp.float32) * scale + bias_c
    p_c = masked_softmax(logit_c, dist_c >= 0)
    o_cmp = jnp.einsum('bghqk,bgkd->bghqd', p_c.astype(vc.dtype), vc)

    ns = s // SEL_BLOCK
    n_top = min(N_SELECT, ns)
    j = jnp.arange(ns)
    s_start = j * SEL_BLOCK
    cover = ((c_start[:, None] <= s_start[None, :] + SEL_BLOCK - 1) & (c_end[:, None] >= s_start[None, :])).astype(jnp.float32)
    score = jnp.einsum('bghqk,kj->bgqj', p_c, cover)
    cur = (t // SEL_BLOCK)[:, None]
    forced = (j[None, :] == 0) | (j[None, :] == cur) | (j[None, :] == cur - 1)
    score = jnp.where(forced, FORCE_SCORE, jnp.where(j[None, :] > cur, -FORCE_SCORE, score))
    _, sel_idx = lax.top_k(score, n_top)
    ks_blk = rmsnorm(k_sel, k_norm_w).transpose(0, 2, 1, 3).reshape(b, g, ns, SEL_BLOCK, dk)
    vs_blk = v_sel.transpose(0, 2, 1, 3).reshape(b, g, ns, SEL_BLOCK, dk)
    qc_len = min(SEL_Q_CHUNK, s)
    nq = s // qc_len
    m = n_top * SEL_BLOCK
    bi = jnp.arange(b)[:, None, None, None]
    gi = jnp.arange(g)[None, :, None, None]
    g6 = jnp.arange(g)[None, :, None, None, None, None]
    h6 = jnp.arange(hpg)[None, None, :, None, None, None]
    offs = jnp.arange(SEL_BLOCK)

    def sel_chunk(args):
        q_blk, idx, tq = args
        k_g = ks_blk[bi, gi, idx]
        v_g = vs_blk[bi, gi, idx]
        dist = tq[None, None, :, None, None] - (idx[..., None] * SEL_BLOCK + offs)
        bias = tbl[g6, h6, rel_bucket(dist)[:, :, None]]
        logit = jnp.einsum('bghqd,bgqnld->bghqnl', q_blk, k_g).astype(jnp.float32) * scale + bias
        p = masked_softmax(logit.reshape(b, g, hpg, qc_len, m),
                           (dist >= 0)[:, :, None].reshape(b, g, 1, qc_len, m))
        return jnp.einsum('bghqm,bgqmd->bghqd', p.astype(v_g.dtype), v_g.reshape(b, g, qc_len, m, dk))

    o_sel = lax.map(sel_chunk, (qh.reshape(b, g, hpg, nq, qc_len, dk).transpose(3, 0, 1, 2, 4, 5),
                                sel_idx.reshape(b, g, nq, qc_len, n_top).transpose(2, 0, 1, 3, 4),
                                t.reshape(nq, qc_len)))
    o_sel = o_sel.transpose(1, 2, 3, 0, 4, 5).reshape(b, g, hpg, s, dk)

    nb = s // Q_BLOCK
    kw_len = WINDOW + Q_BLOCK
    kw = band_blocks(rmsnorm(k_win, k_norm_w).transpose(0, 2, 1, 3))
    vw = band_blocks(v_win.transpose(0, 2, 1, 3))
    qpos = (jnp.arange(nb) * Q_BLOCK)[:, None] + jnp.arange(Q_BLOCK)[None, :]
    kpos = (jnp.arange(nb) * Q_BLOCK - WINDOW)[:, None] + jnp.arange(kw_len)[None, :]
    dist_w = qpos[:, :, None] - kpos[:, None, :]
    mask_w = (kpos[:, None, :] >= 0) & (dist_w >= 0) & (dist_w < WINDOW)
    bias_w = rel_table[rel_bucket(dist_w)].transpose(3, 0, 1, 2).reshape(g, hpg, nb, Q_BLOCK, kw_len)
    logit_w = jnp.einsum('bghiqd,bgikd->bghiqk', qh.reshape(b, g, hpg, nb, Q_BLOCK, dk), kw).astype(jnp.float32) * scale + bias_w
    p_w = masked_softmax(logit_w, mask_w)
    o_win = jnp.einsum('bghiqk,bgikd->bghiqd', p_w.astype(vw.dtype), vw).reshape(b, g, hpg, s, dk)

    gw = jax.nn.sigmoid(gates.astype(jnp.float32)).astype(q.dtype).reshape(b, s, g, hpg, N_BRANCH).transpose(4, 0, 2, 3, 1)[..., None]
    o = gw[0] * o_cmp + gw[1] * o_sel + gw[2] * o_win
    return o.transpose(0, 3, 1, 2, 4).reshape(b, s, g * hpg * dk)


def pool_mixer(h, w_pool, pool_scale):
    b, s, d = h.shape
    h32 = h.astype(jnp.float32)
    cs = jnp.cumsum(h32, axis=1)
    t = jnp.arange(s)
    outs = []
    for gi, w in enumerate(POOL_WINDOWS):
        sl = slice(gi * POOL_GROUP, (gi + 1) * POOL_GROUP)
        cg = cs[..., sl]
        prev = jnp.pad(cg, ((0, 0), (w, 0), (0, 0)))[:, :s]
        cnt = jnp.minimum(t + 1, w).astype(jnp.float32)[None, :, None]
        outs.append(((cg - prev) / cnt - h32[..., sl]).astype(h.dtype))
    dmix = jnp.stack(outs, axis=2)
    y = jnp.einsum('bsgc,gce->bsge', dmix, w_pool).reshape(b, s, d)
    return y * pool_scale


def moe_swiglu(h, w_router, w1, w3, w2):
    b, s, d = h.shape
    hf = h.reshape(-1, d)
    logits = (hf @ w_router).astype(jnp.float32)
    vals, idx = lax.top_k(logits, TOP_K)
    wts = jax.nn.softmax(vals, axis=-1)
    gate = jnp.sum(jax.nn.one_hot(idx, N_EXPERTS, dtype=jnp.float32) * wts[..., None], axis=1)
    out = jnp.zeros_like(hf)
    for e in range(N_EXPERTS):
        out = out + gate[:, e:e + 1].astype(h.dtype) * swiglu(hf, w1[e], w3[e], w2[e])
    return out.reshape(b, s, d)


def setup_inputs(seed: int = 0) -> dict:
    key = jax.random.key(seed)
    ks = iter(jax.random.split(key, 32))

    def nrm(shape, scale):
        return jax.random.normal(next(ks), shape, jnp.float32) * scale

    def gain(shape):
        return 1.0 + 0.02 * jax.random.normal(next(ks), shape, jnp.float32)

    E, O = N_EVEN, N_ODD
    return {
        'x': nrm((BATCH, SEQ, D_MODEL), 1.0),
        'norm_mix_even': gain((E, D_MODEL)),
        'w_in_even': nrm((E, D_MODEL, IN_COLS), D_MODEL ** -0.5),
        'conv_w_even': nrm((E, CONV_CH, CONV_TAPS), CONV_TAPS ** -0.5),
        'q_norm_even': gain((E, HEAD_DIM)),
        'k_norm_even': gain((E, HEAD_DIM)),
        'cmp_pe_k_even': nrm((E, CMP_BLOCK, HEAD_DIM), 0.1),
        'cmp_pe_v_even': nrm((E, CMP_BLOCK, HEAD_DIM), 0.1),
        'w_cmp_k_even': nrm((E, CMP_BLOCK, HEAD_DIM, HEAD_DIM), (CMP_BLOCK * HEAD_DIM) ** -0.5),
        'w_cmp_v_even': nrm((E, CMP_BLOCK, HEAD_DIM, HEAD_DIM), (CMP_BLOCK * HEAD_DIM) ** -0.5),
        'w_out_even': nrm((E, MIX_W, D_MODEL), MIX_W ** -0.5),
        'norm_ffn_even': gain((E, D_MODEL)),
        'w1_dense': nrm((E, D_MODEL, D_FF_DENSE), D_MODEL ** -0.5),
        'w3_dense': nrm((E, D_MODEL, D_FF_DENSE), D_MODEL ** -0.5),
        'w2_dense': nrm((E, D_FF_DENSE, D_MODEL), D_FF_DENSE ** -0.5),
        'norm_mix_odd': gain((O, D_MODEL)),
        'w_pool_odd': nrm((O, len(POOL_WINDOWS), POOL_GROUP, POOL_GROUP), POOL_GROUP ** -0.5),
        'pool_scale_odd': gain((O, D_MODEL)),
        'norm_ffn_odd': gain((O, D_MODEL)),
        'w_router_odd': nrm((O, D_MODEL, N_EXPERTS), D_MODEL ** -0.5),
        'w1_moe': nrm((O, N_EXPERTS, D_MODEL, D_FF_EXPERT), D_MODEL ** -0.5),
        'w3_moe': nrm((O, N_EXPERTS, D_MODEL, D_FF_EXPERT), D_MODEL ** -0.5),
        'w2_moe': nrm((O, N_EXPERTS, D_FF_EXPERT, D_MODEL), D_FF_EXPERT ** -0.5),
        'rel_bias': nrm((REL_BUCKETS, NSA_HEADS), 0.5),
    }


def reference(x, norm_mix_even, w_in_even, conv_w_even, q_norm_even, k_norm_even,
              cmp_pe_k_even, cmp_pe_v_even, w_cmp_k_even, w_cmp_v_even, w_out_even,
              norm_ffn_even, w1_dense, w3_dense, w2_dense,
              norm_mix_odd, w_pool_odd, pool_scale_odd, norm_ffn_odd, w_router_odd,
              w1_moe, w3_moe, w2_moe, rel_bias):
    b, s, _ = x.shape
    for layer in range(DEPTH):
        i = layer // 2
        if layer % 2 == 0:
            h = rmsnorm(x, norm_mix_even[i])
            proj = h @ w_in_even[i]
            a_b, a_c, a_u, q, kc, vc, ks, vs, kw, vw, gts = split_cols(proj)
            y_a = short_conv_mixer(a_b, a_c, a_u, conv_w_even[i])
            kvs = lambda z: z.reshape(b, s, NSA_KV_HEADS, HEAD_DIM)
            y_b = nsa_mixer(q.reshape(b, s, NSA_HEADS, HEAD_DIM), kvs(kc), kvs(vc), kvs(ks), kvs(vs), kvs(kw), kvs(vw),
                            gts.reshape(b, s, NSA_HEADS, N_BRANCH), q_norm_even[i], k_norm_even[i],
                            cmp_pe_k_even[i], cmp_pe_v_even[i], w_cmp_k_even[i], w_cmp_v_even[i], rel_bias)
            x = x + jnp.concatenate([y_a, y_b], axis=-1) @ w_out_even[i]
            x = x + swiglu(rmsnorm(x, norm_ffn_even[i]), w1_dense[i], w3_dense[i], w2_dense[i])
        else:
            x = x + pool_mixer(rmsnorm(x, norm_mix_odd[i]), w_pool_odd[i], pool_scale_odd[i])
            x = x + moe_swiglu(rmsnorm(x, norm_ffn_odd[i]), w_router_odd[i], w1_moe[i], w3_moe[i], w2_moe[i])
    return x
```

```python
import functools
import math

import numpy as np
import jax
import jax.numpy as jnp
from jax import lax
from jax.experimental import pallas as pl
from jax.experimental.pallas import tpu as pltpu

F32 = jnp.float32
BF16 = jnp.bfloat16

HEAD_DIM = 128
KV_HEADS = 4
CONV_TAPS = 3
N_BRANCH = 3
CMP_BLOCK = 32
CMP_STRIDE = 16
SEL_BLOCK = 64
N_SELECT = 16
WINDOW = 512
FORCE_SCORE = 1e6
NEG_INF = -1e30
REL_BUCKETS = 32
REL_MAX_DIST = 128
POOL_WINDOWS = (2, 4, 8, 16)
N_EXPERTS = 8
EPS = 1e-6

LANES = 128
POOL_HALO = 16
CONV_HALO = 8
VMEM_CAP = 60 * 1024 * 1024
NT_DIMS = (((1,), (1,)), ((), ()))


def _pick(n, pref, mult=LANES):
    t = min(pref, n)
    while n % t or t % mult:
        t -= mult
    return t


def _params(sem, est_bytes):
    limit = int(min(max(est_bytes * 5 // 4 + (4 << 20), 16 << 20), VMEM_CAP))
    return pltpu.CompilerParams(dimension_semantics=sem, vmem_limit_bytes=limit)


def _rms(x, g):
    return x * lax.rsqrt(jnp.mean(x * x, axis=-1, keepdims=True) + EPS) * g


def _rmsnorm_kernel(x_ref, g_ref, o_ref):
    o_ref[...] = _rms(x_ref[...], g_ref[...]).astype(o_ref.dtype)


def rmsnorm(x, g, out_dtype, tm=256):
    m, d = x.shape
    tm = _pick(m, tm, 8)
    return pl.pallas_call(
        _rmsnorm_kernel,
        grid=(m // tm,),
        in_specs=[pl.BlockSpec((tm, d), lambda i: (i, 0)), pl.BlockSpec((1, d), lambda i: (0, 0))],
        out_specs=pl.BlockSpec((tm, d), lambda i: (i, 0)),
        out_shape=jax.ShapeDtypeStruct((m, d), out_dtype),
        compiler_params=_params(("parallel",), 2 * tm * d * 8),
        name="rmsnorm",
    )(x, g.reshape(1, d))


def _mm_kernel(*refs, n_pairs, has_res):
    o_ref = refs[-1]
    acc = None
    for p in range(n_pairs):
        d = jnp.dot(refs[2 * p][...], refs[2 * p + 1][...], preferred_element_type=F32)
        acc = d if acc is None else acc + d
    if has_res:
        acc = refs[2 * n_pairs][...] + acc
    o_ref[...] = acc.astype(o_ref.dtype)


def matmul(pairs, res, out_dtype, tm, tn, name):
    m = pairs[0][0].shape[0]
    n = pairs[0][1].shape[1]
    tm = _pick(m, tm)
    tn = _pick(n, tn)
    in_specs, args, est = [], [], 0
    for x, w in pairs:
        k = x.shape[1]
        in_specs += [pl.BlockSpec((tm, k), lambda i, j: (i, 0)), pl.BlockSpec((k, tn), lambda i, j: (0, j))]
        args += [x, w]
        est += 2 * 2 * (tm * k + k * tn)
    if res is not None:
        in_specs.append(pl.BlockSpec((tm, tn), lambda i, j: (i, j)))
        args.append(res)
        est += 2 * 4 * tm * tn
    est += 2 * 4 * tm * tn + 4 * tm * tn
    return pl.pallas_call(
        functools.partial(_mm_kernel, n_pairs=len(pairs), has_res=res is not None),
        grid=(m // tm, n // tn),
        in_specs=in_specs,
        out_specs=pl.BlockSpec((tm, tn), lambda i, j: (i, j)),
        out_shape=jax.ShapeDtypeStruct((m, n), out_dtype),
        compiler_params=_params(("parallel", "parallel"), est),
        name=name,
    )(*args)


def _glu_kernel(h_ref, w1_ref, w3_ref, o_ref):
    h = h_ref[...]
    a = jnp.dot(h, w1_ref[...], preferred_element_type=F32)
    b = jnp.dot(h, w3_ref[...], preferred_element_type=F32)
    o_ref[...] = (a * jax.nn.sigmoid(a) * b).astype(o_ref.dtype)


def glu_matmul(h, w1, w3, tm, tn):
    m, k = h.shape
    n = w1.shape[1]
    tm = _pick(m, tm)
    tn = _pick(n, tn)
    est = 2 * 2 * (tm * k + 2 * k * tn + tm * tn) + 3 * 4 * tm * tn
    return pl.pallas_call(
        _glu_kernel,
        grid=(m // tm, n // tn),
        in_specs=[pl.BlockSpec((tm, k), lambda i, j: (i, 0)),
                  pl.BlockSpec((k, tn), lambda i, j: (0, j)),
                  pl.BlockSpec((k, tn), lambda i, j: (0, j))],
        out_specs=pl.BlockSpec((tm, tn), lambda i, j: (i, j)),
        out_shape=jax.ShapeDtypeStruct((m, n), BF16),
        compiler_params=_params(("parallel", "parallel"), est),
        name="dense_glu",
    )(h, w1, w3)


def _mm_acc_kernel(x_ref, w_ref, r_ref, o_ref, acc_ref):
    k = pl.program_id(2)

    @pl.when(k == 0)
    def _():
        acc_ref[...] = jnp.zeros_like(acc_ref)

    acc_ref[...] += jnp.dot(x_ref[...], w_ref[...], preferred_element_type=F32)

    @pl.when(k == pl.num_programs(2) - 1)
    def _():
        o_ref[...] = r_ref[...] + acc_ref[...]


def matmul_acc(x, w, res, tm, tn, tk, name):
    m, k = x.shape
    n = w.shape[1]
    tm, tn, tk = _pick(m, tm), _pick(n, tn), _pick(k, tk)
    est = 2 * 2 * (tm * tk + tk * tn) + 5 * 4 * tm * tn
    return pl.pallas_call(
        _mm_acc_kernel,
        grid=(m // tm, n // tn, k // tk),
        in_specs=[pl.BlockSpec((tm, tk), lambda i, j, kk: (i, kk)),
                  pl.BlockSpec((tk, tn), lambda i, j, kk: (kk, j)),
                  pl.BlockSpec((tm, tn), lambda i, j, kk: (i, j))],
        out_specs=pl.BlockSpec((tm, tn), lambda i, j, kk: (i, j)),
        out_shape=jax.ShapeDtypeStruct((m, n), F32),
        scratch_shapes=[pltpu.VMEM((tm, tn), F32)],
        compiler_params=_params(("parallel", "parallel", "arbitrary"), est),
        name=name,
    )(x, w, res)


def _conv_kernel(ab_ref, ac_ref, au_ref, hc_ref, hu_ref, w_ref, o_ref, *, tq, seq):
    i = pl.program_id(0)
    first = (i * tq) % seq == 0
    v = ac_ref[...] * au_ref[...]
    hv = jnp.where(first, 0.0, hc_ref[...] * hu_ref[...])
    rows = lax.broadcasted_iota(jnp.int32, v.shape, 0)
    v1 = jnp.where(rows == 0, hv[CONV_HALO - 1:CONV_HALO], pltpu.roll(v, 1, 0))
    v2 = jnp.where(rows == 0, hv[CONV_HALO - 2:CONV_HALO - 1],
                   jnp.where(rows == 1, hv[CONV_HALO - 1:CONV_HALO], pltpu.roll(v, 2, 0)))
    y = w_ref[0:1, :] * v2 + w_ref[1:2, :] * v1 + w_ref[2:3, :] * v
    o_ref[...] = (ab_ref[...] * y).astype(o_ref.dtype)


def short_conv(proj, conv_w_t, ch, seq, tq=256, tc=512):
    m = proj.shape[0]
    tq = _pick(seq, tq, CONV_HALO)
    tc = _pick(ch, tc)
    nj = ch // tc
    hb = tq // CONV_HALO
    halo = lambda off: pl.BlockSpec((CONV_HALO, tc), lambda i, j: (jnp.maximum(i * hb - 1, 0), off + j))
    return pl.pallas_call(
        functools.partial(_conv_kernel, tq=tq, seq=seq),
        grid=(m // tq, nj),
        in_specs=[pl.BlockSpec((tq, tc), lambda i, j: (i, j)),
                  pl.BlockSpec((tq, tc), lambda i, j: (i, nj + j)),
                  pl.BlockSpec((tq, tc), lambda i, j: (i, 2 * nj + j)),
                  halo(nj), halo(2 * nj),
                  pl.BlockSpec((CONV_TAPS, tc), lambda i, j: (0, j))],
        out_specs=pl.BlockSpec((tq, tc), lambda i, j: (i, j)),
        out_shape=jax.ShapeDtypeStruct((m, ch), BF16),
        compiler_params=_params(("parallel", "parallel"), 2 * 4 * 4 * tq * tc + 6 * 4 * tq * tc),
        name="short_conv",
    )(proj, proj, proj, proj, proj, conv_w_t)


def _rel_bucket_np(dist):
    dist = np.maximum(dist, 0)
    exact = REL_BUCKETS // 2
    d = np.maximum(dist, exact).astype(np.float32)
    large = exact + (np.log(d / np.float32(exact)) / np.float32(math.log(REL_MAX_DIST / exact))
                     * np.float32(REL_BUCKETS - exact)).astype(np.int32)
    return np.where(dist < exact, dist, np.minimum(large, REL_BUCKETS - 1)).astype(np.int32)


def _bias_index_tables(seq):
    r = np.arange(LANES)[:, None]
    c = np.arange(LANES)[None, :]
    tiles = [_rel_bucket_np(r - c), _rel_bucket_np(LANES + r - c),
             np.full((LANES, LANES), REL_BUCKETS - 1, np.int32)]
    assert _rel_bucket_np(np.arange(LANES + 1, 4 * seq)).min() == REL_BUCKETS - 1
    t = np.arange(seq)[:, None]
    cmp_idx = _rel_bucket_np(t - (c * CMP_STRIDE + CMP_BLOCK - 1))
    return np.concatenate(tiles + [cmp_idx], axis=0).astype(np.int32)


def _bias_kernel(tbl_ref, idx_ref, o_ref):
    h = pl.program_id(0)
    idx = idx_ref[...]
    acc = jnp.zeros(idx.shape, F32)
    for b in range(REL_BUCKETS):
        acc = jnp.where(idx == b, tbl_ref[b, h], acc)
    o_ref[0] = acc


def bias_tables(rel_bias, seq):
    heads = rel_bias.shape[1]
    idx = jnp.asarray(_bias_index_tables(seq))
    rows = idx.shape[0]
    return pl.pallas_call(
        _bias_kernel,
        grid=(heads,),
        in_specs=[pl.BlockSpec(memory_space=pltpu.SMEM), pl.BlockSpec((rows, LANES), lambda h: (0, 0))],
        out_specs=pl.BlockSpec((1, rows, LANES), lambda h: (h, 0, 0)),
        out_shape=jax.ShapeDtypeStruct((heads, rows, LANES), F32),
        compiler_params=_params(("parallel",), 8 * 4 * rows * LANES),
        name="rel_bias_tables",
    )(rel_bias, idx)


def _compress_kernel(k_ref, v_ref, pek_ref, pev_ref, wk_ref, wv_ref, kn_ref, kc_ref, vc_ref, *, nch):
    half = CMP_BLOCK // CMP_STRIDE
    assert half == 2

    def comp(x_ref, pe_ref, w_ref):
        lo = jnp.zeros((nch, HEAD_DIM), F32)
        hi = jnp.zeros((nch, HEAD_DIM), F32)
        for l in range(CMP_STRIDE):
            rows = x_ref[pl.ds(l, nch, stride=CMP_STRIDE), :]
            lo += jnp.dot((rows + pe_ref[l:l + 1, :]).astype(BF16), w_ref[l], preferred_element_type=F32)
            hi += jnp.dot((rows + pe_ref[CMP_STRIDE + l:CMP_STRIDE + l + 1, :]).astype(BF16),
                          w_ref[CMP_STRIDE + l], preferred_element_type=F32)
        return lo + pltpu.roll(hi, nch - 1, 0)

    kc_ref[0, 0] = _rms(comp(k_ref, pek_ref, wk_ref), kn_ref[...])
    vc_ref[0, 0] = comp(v_ref, pev_ref, wv_ref)


def compress(proj, kc_blk, vc_blk, pe_k, pe_v, w_k, w_v, k_norm, batch, seq):
    nch = seq // CMP_STRIDE
    wspec = pl.BlockSpec((CMP_BLOCK, HEAD_DIM, HEAD_DIM), lambda b, g: (0, 0, 0))
    pespec = pl.BlockSpec((CMP_BLOCK, HEAD_DIM), lambda b, g: (0, 0))
    ospec = pl.BlockSpec((1, 1, nch, HEAD_DIM), lambda b, g: (b, g, 0, 0))
    oshape = jax.ShapeDtypeStruct((batch, KV_HEADS, nch, HEAD_DIM), F32)
    return pl.pallas_call(
        functools.partial(_compress_kernel, nch=nch),
        grid=(batch, KV_HEADS),
        in_specs=[pl.BlockSpec((seq, HEAD_DIM), lambda b, g: (b, kc_blk + g)),
                  pl.BlockSpec((seq, HEAD_DIM), lambda b, g: (b, vc_blk + g)),
                  pespec, pespec, wspec, wspec,
                  pl.BlockSpec((1, HEAD_DIM), lambda b, g: (0, 0))],
        out_specs=[ospec, ospec],
        out_shape=[oshape, oshape],
        compiler_params=_params(("parallel", "parallel"), 4 * 4 * seq * HEAD_DIM + (4 << 20)),
        name="nsa_compress",
    )(proj, proj, pe_k, pe_v, w_k, w_v, k_norm)


def _nsa_kernel(q_ref, ks_ref, vs_ref, kw_ref, vw_ref, kc_ref, vc_ref, tb_ref, bc_ref, gt_ref,
                qn_ref, kn_ref, cov_ref, o_ref, ksn, vsb, kwn, vwb, *, hpg, n_cmp, n_top):
    qi = pl.program_id(2)
    rows_all = hpg * LANES
    scale = HEAD_DIM ** -0.5
    n_sel_blocks = cov_ref.shape[0]

    @pl.when(qi == 0)
    def _():
        ksn[...] = _rms(ks_ref[...], kn_ref[...]).astype(BF16)
        kwn[...] = _rms(kw_ref[...], kn_ref[...]).astype(BF16)
        vsb[...] = vs_ref[...].astype(BF16)
        vwb[...] = vw_ref[...].astype(BF16)

    qs = [_rms(q_ref[:, h * HEAD_DIM:(h + 1) * HEAD_DIM], qn_ref[...]).astype(BF16) for h in range(hpg)]
    q = jnp.concatenate(qs, axis=0) if hpg > 1 else qs[0]

    rowq = lax.broadcasted_iota(jnp.int32, (rows_all, LANES), 0) & (LANES - 1)
    col = lax.broadcasted_iota(jnp.int32, (rows_all, LANES), 1)

    lc = lax.dot_general(q, kc_ref[0, 0].astype(BF16), NT_DIMS, preferred_element_type=F32) * scale
    lc = lc + bc_ref[...].reshape(rows_all, LANES)
    mc = ((qi * LANES + rowq - (col * CMP_STRIDE + CMP_BLOCK - 1)) >= 0) & (col < n_cmp)
    zc = jnp.where(mc, lc, NEG_INF)
    ec = jnp.exp(zc - jnp.max(zc, axis=-1, keepdims=True))
    pc = ec / jnp.sum(ec, axis=-1, keepdims=True) * mc.astype(F32)
    o_cmp = jnp.dot(pc.astype(BF16), vc_ref[0, 0].astype(BF16), preferred_element_type=F32)

    ps = pc[0:LANES]
    for h in range(1, hpg):
        ps = ps + pc[h * LANES:(h + 1) * LANES]
    p1 = ps.astype(BF16)
    r1 = ps - p1.astype(F32)
    p2 = r1.astype(BF16)
    p3 = (r1 - p2.astype(F32)).astype(BF16)
    cov = cov_ref[...]
    score = (lax.dot_general(cov, p1, NT_DIMS, preferred_element_type=F32)
             + lax.dot_general(cov, p2, NT_DIMS, preferred_element_type=F32)
             + lax.dot_general(cov, p3, NT_DIMS, preferred_element_type=F32))
    jj = lax.broadcasted_iota(jnp.int32, (n_sel_blocks, LANES), 0)
    ql = lax.broadcasted_iota(jnp.int32, (n_sel_blocks, LANES), 1)
    cur = (LANES // SEL_BLOCK) * qi + ql // SEL_BLOCK
    forced = (jj == 0) | (jj == cur) | (jj == cur - 1)
    score = jnp.where(forced, FORCE_SCORE, jnp.where(jj > cur, -FORCE_SCORE, score))
    rank = jnp.zeros((n_sel_blocks, LANES), F32)
    for j2 in range(n_sel_blocks):
        other = score[j2:j2 + 1, :]
        rank += ((other > score) | ((other == score) & (j2 < jj))).astype(F32)
    sel_t = (rank < n_top).astype(BF16)
    eye = (lax.broadcasted_iota(jnp.int32, (LANES, LANES), 0)
           == lax.broadcasted_iota(jnp.int32, (LANES, LANES), 1)).astype(BF16)
    selq = lax.dot_general(eye, sel_t, NT_DIMS, preferred_element_type=F32)
    selq = jnp.concatenate([selq] * hpg, axis=0).astype(BF16) if hpg > 1 else selq.astype(BF16)

    exp_j = lax.broadcasted_iota(jnp.int32, (n_sel_blocks, LANES), 0)
    exp_c = lax.broadcasted_iota(jnp.int32, (n_sel_blocks, LANES), 1) // SEL_BLOCK

    def step(kt, carry, k_ref, v_ref, window):
        m, l, acc = carry
        off = pl.multiple_of(kt * LANES, LANES)
        s = lax.dot_general(q, k_ref[pl.ds(off, LANES), :], NT_DIMS, preferred_element_type=F32) * scale
        boff = pl.multiple_of(jnp.minimum(qi - kt, 2) * LANES, LANES)
        s = s + tb_ref[:, pl.ds(boff, LANES), :].reshape(rows_all, LANES)
        dist = (qi - kt) * LANES + rowq - col
        if window:
            mk = (dist >= 0) & (dist < WINDOW)
        else:
            expand = (exp_j == (LANES // SEL_BLOCK) * kt + exp_c).astype(BF16)
            mk = (jnp.dot(selq, expand, preferred_element_type=F32) > 0.5) & (dist >= 0)
        z = jnp.where(mk, s, NEG_INF)
        m_new = jnp.maximum(m, jnp.max(z, axis=-1, keepdims=True))
        alpha = jnp.exp(m - m_new)
        p = jnp.exp(z - m_new) * mk.astype(F32)
        l = alpha * l + jnp.sum(p, axis=-1, keepdims=True)
        acc = alpha * acc + jnp.dot(p.astype(BF16), v_ref[pl.ds(off, LANES), :], preferred_element_type=F32)
        return m_new, l, acc

    init = (jnp.full((rows_all, 1), NEG_INF, F32), jnp.zeros((rows_all, 1), F32),
            jnp.zeros((rows_all, HEAD_DIM), F32))
    _, l_s, a_s = lax.fori_loop(0, qi + 1, functools.partial(step, k_ref=ksn, v_ref=vsb, window=False), init)
    _, l_w, a_w = lax.fori_loop(jnp.maximum(qi - WINDOW // LANES, 0), qi + 1,
                                functools.partial(step, k_ref=kwn, v_ref=vwb, window=True), init)
    o_sel = jnp.where(l_s > 0, a_s / l_s, 0.0)
    o_win = jnp.where(l_w > 0, a_w / l_w, 0.0)

    gt = jax.nn.sigmoid(gt_ref[...])
    for h in range(hpg):
        sl = slice(h * LANES, (h + 1) * LANES)
        c0 = h * N_BRANCH
        o = (gt[:, c0:c0 + 1] * o_cmp[sl] + gt[:, c0 + 1:c0 + 2] * o_sel[sl]
             + gt[:, c0 + 2:c0 + 3] * o_win[sl])
        o_ref[:, h * HEAD_DIM:(h + 1) * HEAD_DIM] = o.astype(o_ref.dtype)


def _cover_t(seq):
    nc = seq // CMP_STRIDE - CMP_BLOCK // CMP_STRIDE + 1
    ns = seq // SEL_BLOCK
    c_start = np.arange(nc) * CMP_STRIDE
    c_end = c_start + CMP_BLOCK - 1
    s_start = np.arange(ns) * SEL_BLOCK
    cover = (c_start[:, None] <= s_start[None, :] + SEL_BLOCK - 1) & (c_end[:, None] >= s_start[None, :])
    out = np.zeros((ns, seq // CMP_STRIDE), np.float32)
    out[:, :nc] = cover.T
    return out


def nsa_attention(proj, gates, kc, vc, tables, q_norm, k_norm, batch, seq, hpg, blk):
    heads = KV_HEADS * hpg
    nq = seq // LANES
    nch = seq // CMP_STRIDE
    assert nch == LANES, "compressed keys must fill exactly one lane tile"
    n_cmp = nch - CMP_BLOCK // CMP_STRIDE + 1
    ns = seq // SEL_BLOCK
    cov = jnp.asarray(_cover_t(seq), BF16)
    qw = hpg * HEAD_DIM
    q_blk = blk["q"] * HEAD_DIM // qw
    kv = lambda name: pl.BlockSpec((seq, HEAD_DIM), lambda b, g, i: (b, blk[name] + g))
    cspec = pl.BlockSpec((1, 1, nch, HEAD_DIM), lambda b, g, i: (b, g, 0, 0))
    vec = pl.BlockSpec((1, HEAD_DIM), lambda b, g, i: (0, 0))
    est = 2 * 4 * 4 * seq * HEAD_DIM + 4 * 2 * seq * HEAD_DIM + 2 * 4 * hpg * (3 + 1) * LANES * LANES + (8 << 20)
    return pl.pallas_call(
        functools.partial(_nsa_kernel, hpg=hpg, n_cmp=n_cmp, n_top=min(N_SELECT, ns)),
        grid=(batch, KV_HEADS, nq),
        in_specs=[pl.BlockSpec((LANES, qw), lambda b, g, i: (b * nq + i, q_blk + g)),
                  kv("ks"), kv("vs"), kv("kw"), kv("vw"), cspec, cspec,
                  pl.BlockSpec((hpg, 3 * LANES, LANES), lambda b, g, i: (g, 0, 0)),
                  pl.BlockSpec((hpg, LANES, LANES), lambda b, g, i: (g, 3 + i, 0)),
                  pl.BlockSpec((LANES, LANES), lambda b, g, i: (b * nq + i, g)),
                  vec, vec,
                  pl.BlockSpec((ns, nch), lambda b, g, i: (0, 0))],
        out_specs=pl.BlockSpec((LANES, qw), lambda b, g, i: (b * nq + i, g)),
        out_shape=jax.ShapeDtypeStruct((batch * seq, heads * HEAD_DIM), BF16),
        scratch_shapes=[pltpu.VMEM((seq, HEAD_DIM), BF16)] * 4,
        compiler_params=_params(("parallel", "parallel", "arbitrary"), est),
        name="nsa_attention",
    )(proj, proj, proj, proj, proj, kc, vc, tables, tables, gates, q_norm, k_norm, cov)


def _pool_kernel(x_ref, halo_ref, gm_ref, wp_ref, ps_ref, gf_ref, wr_ref, x_out, route_out, *, tq, seq, cg):
    i = pl.program_id(0)
    start = (i * tq) % seq
    x = x_ref[...]
    h = _rms(x, gm_ref[...])
    hh = jnp.where(start == 0, 0.0, _rms(halo_ref[...], gm_ref[...]))
    t1 = (start + lax.broadcasted_iota(jnp.int32, (tq, 1), 0) + 1).astype(F32)
    ys = []
    for gi, w in enumerate(POOL_WINDOWS):
        sl = slice(gi * cg, (gi + 1) * cg)
        s = jnp.concatenate([hh[:, sl], h[:, sl]], axis=0)
        span = 1
        while span < w:
            s = s + pltpu.roll(s, span, 0)
            span *= 2
        dm = s[POOL_HALO:] / jnp.minimum(t1, float(w)) - h[:, sl]
        ys.append(jnp.dot(dm.astype(BF16), wp_ref[gi], preferred_element_type=F32))
    x3 = x + jnp.concatenate(ys, axis=1) * ps_ref[...]
    x_out[...] = x3

    h4 = _rms(x3, gf_ref[...])
    lane = lax.broadcasted_iota(jnp.int32, (tq, LANES), 1)
    logits = jnp.full((tq, LANES), -jnp.inf, F32)
    for e in range(N_EXPERTS):
        logits = jnp.where(lane == e, jnp.sum(h4 * wr_ref[e:e + 1, :], axis=-1, keepdims=True), logits)
    m1 = jnp.max(logits, axis=-1, keepdims=True)
    i1 = jnp.min(jnp.where(logits == m1, lane, LANES), axis=-1, keepdims=True)
    rest = jnp.where(lane == i1, -jnp.inf, logits)
    m2 = jnp.max(rest, axis=-1, keepdims=True)
    i2 = jnp.min(jnp.where(rest == m2, lane, LANES), axis=-1, keepdims=True)
    e2 = jnp.exp(m2 - m1)
    den = 1.0 + e2
    route = jnp.where(lane == 0, i1.astype(F32),
                      jnp.where(lane == 1, i2.astype(F32),
                                jnp.where(lane == 2, 1.0 / den, jnp.where(lane == 3, e2 / den, 0.0))))
    route_out[...] = route


def pool_and_route(x, g_mix, w_pool, pool_scale, g_ffn, w_router_t, seq, tq=256):
    m, d = x.shape
    cg = d // len(POOL_WINDOWS)
    tq = _pick(seq, tq, POOL_HALO)
    hb = tq // POOL_HALO
    vec = pl.BlockSpec((1, d), lambda i: (0, 0))
    est = 2 * 2 * 4 * tq * d + 2 * 2 * len(POOL_WINDOWS) * cg * cg + 8 * 4 * tq * d
    return pl.pallas_call(
        functools.partial(_pool_kernel, tq=tq, seq=seq, cg=cg),
        grid=(m // tq,),
        in_specs=[pl.BlockSpec((tq, d), lambda i: (i, 0)),
                  pl.BlockSpec((POOL_HALO, d), lambda i: (jnp.maximum(i * hb - 1, 0), 0)),
                  vec,
                  pl.BlockSpec((len(POOL_WINDOWS), cg, cg), lambda i: (0, 0, 0)),
                  vec, vec,
                  pl.BlockSpec((N_EXPERTS, d), lambda i: (0, 0))],
        out_specs=[pl.BlockSpec((tq, d), lambda i: (i, 0)), pl.BlockSpec((tq, LANES), lambda i: (i, 0))],
        out_shape=[jax.ShapeDtypeStruct((m, d), F32), jax.ShapeDtypeStruct((m, LANES), F32)],
        compiler_params=_params(("parallel",), est),
        name="pool_mixer_router",
    )(x, x, g_mix, w_pool, pool_scale, g_ffn, w_router_t)


def _route_tables(route, tm, n_tiles):
    n = route.shape[0]
    i1 = route[:, 0].astype(jnp.int32)
    i2 = route[:, 1].astype(jnp.int32)
    onehot = jax.nn.one_hot(i1, N_EXPERTS, dtype=jnp.int32) + jax.nn.one_hot(i2, N_EXPERTS, dtype=jnp.int32)
    padded = (jnp.sum(onehot, axis=0) + tm - 1) // tm * tm
    end = jnp.cumsum(padded)
    pos = (end - padded)[None, :] + jnp.cumsum(onehot, axis=0) - onehot
    p1 = jnp.take_along_axis(pos, i1[:, None], axis=1)[:, 0]
    p2 = jnp.take_along_axis(pos, i2[:, None], axis=1)[:, 0]
    tok = jnp.arange(n, dtype=jnp.int32)
    rows = n_tiles * tm
    row_token = jnp.zeros((rows,), jnp.int32).at[p1].set(tok).at[p2].set(tok)
    row_gate = jnp.zeros((rows,), F32).at[p1].set(route[:, 2]).at[p2].set(route[:, 3])
    n_used = (end[-1] // tm).astype(jnp.int32)
    tile = jnp.arange(n_tiles, dtype=jnp.int32)
    tile_expert = jnp.searchsorted(end, tile * tm, side="right").astype(jnp.int32)
    tile_expert = jnp.where(tile < n_used, tile_expert, tile_expert[n_used - 1])
    return tile_expert, row_token, row_gate.reshape(rows, 1), n_used.reshape(1), p1.astype(jnp.int32), p2.astype(jnp.int32)


def _gather_rows(idx_ref, base, src_hbm, dst_ref, sem, count):
    def issue(r, carry):
        pltpu.make_async_copy(src_hbm.at[pl.ds(idx_ref[base + r], 1), :], dst_ref.at[pl.ds(r, 1), :], sem).start()
        return carry

    lax.fori_loop(0, count, issue, 0)
    pltpu.make_async_copy(src_hbm.at[pl.ds(0, count), :], dst_ref, sem).wait()


def _moe_glu_kernel(te_ref, rt_ref, nu_ref, x_hbm, g_ref, w1_ref, w3_ref, o_ref, xg_ref, h_ref, sem, *, tm):
    i = pl.program_id(0)
    f = pl.program_id(1)
    valid = i < nu_ref[0]

    @pl.when(valid & (f == 0))
    def _():
        _gather_rows(rt_ref, i * tm, x_hbm, xg_ref, sem, tm)
        h_ref[...] = _rms(xg_ref[...], g_ref[...]).astype(BF16)

    @pl.when(valid)
    def _():
        h = h_ref[...]
        a = jnp.dot(h, w1_ref[0], preferred_element_type=F32)
        b = jnp.dot(h, w3_ref[0], preferred_element_type=F32)
        o_ref[...] = (a * jax.nn.sigmoid(a) * b).astype(o_ref.dtype)

    @pl.when(jnp.logical_not(valid))
    def _():
        o_ref[...] = jnp.zeros_like(o_ref)


def moe_glu(x, g_ffn, w1, w3, tile_expert, row_token, n_used, tm, tf):
    d = x.shape[1]
    ff = w1.shape[2]
    tf = _pick(ff, tf)
    nf = ff // tf
    n_tiles = tile_expert.shape[0]
    wspec = pl.BlockSpec((1, d, tf), lambda i, f, te, rt, nu: (te[i], 0, jnp.where(i < nu[0], f, nf - 1)))
    est = 4 * tm * d + 2 * tm * d + 2 * 2 * 2 * d * tf + 2 * 2 * tm * tf + 3 * 4 * tm * tf + 4 * tm * d
    return pl.pallas_call(
        functools.partial(_moe_glu_kernel, tm=tm),
        grid_spec=pltpu.PrefetchScalarGridSpec(
            num_scalar_prefetch=3,
            grid=(n_tiles, nf),
            in_specs=[pl.BlockSpec(memory_space=pl.ANY),
                      pl.BlockSpec((1, d), lambda i, f, te, rt, nu: (0, 0)),
                      wspec, wspec],
            out_specs=pl.BlockSpec((tm, tf), lambda i, f, te, rt, nu: (i, f)),
            scratch_shapes=[pltpu.VMEM((tm, d), F32), pltpu.VMEM((tm, d), BF16), pltpu.SemaphoreType.DMA(())]),
        out_shape=jax.ShapeDtypeStruct((n_tiles * tm, ff), BF16),
        compiler_params=_params(("arbitrary", "arbitrary"), est),
        name="moe_glu",
    )(tile_expert, row_token, n_used, x, g_ffn, w1, w3)


def _moe_down_kernel(te_ref, nu_ref, a_ref, w_ref, gate_ref, o_ref):
    valid = pl.program_id(0) < nu_ref[0]

    @pl.when(valid)
    def _():
        o_ref[...] = gate_ref[...] * jnp.dot(a_ref[...], w_ref[0], preferred_element_type=F32)

    @pl.when(jnp.logical_not(valid))
    def _():
        o_ref[...] = jnp.zeros_like(o_ref)


def moe_down(act, w2, row_gate, tile_expert, n_used, tm, tn):
    ff = act.shape[1]
    d = w2.shape[2]
    tn = _pick(d, tn)
    nn = d // tn
    n_tiles = tile_expert.shape[0]
    est = 2 * 2 * (tm * ff + ff * tn) + 3 * 4 * tm * tn + 2 * 4 * tm * LANES
    return pl.pallas_call(
        _moe_down_kernel,
        grid_spec=pltpu.PrefetchScalarGridSpec(
            num_scalar_prefetch=2,
            grid=(n_tiles, nn),
            in_specs=[pl.BlockSpec((tm, ff), lambda i, j, te, nu: (i, 0)),
                      pl.BlockSpec((1, ff, tn), lambda i, j, te, nu: (te[i], 0, jnp.where(i < nu[0], j, nn - 1))),
                      pl.BlockSpec((tm, 1), lambda i, j, te, nu: (i, 0))],
            out_specs=pl.BlockSpec((tm, tn), lambda i, j, te, nu: (i, j))),
        out_shape=jax.ShapeDtypeStruct((n_tiles * tm, d), F32),
        compiler_params=_params(("arbitrary", "arbitrary"), est),
        name="moe_down",
    )(tile_expert, n_used, act, w2, row_gate)


def _moe_combine_kernel(p1_ref, p2_ref, x_ref, y_hbm, o_ref, ya_ref, yb_ref, sem_a, sem_b, *, tq):
    base = pl.program_id(0) * tq
    _gather_rows(p1_ref, base, y_hbm, ya_ref, sem_a, tq)
    _gather_rows(p2_ref, base, y_hbm, yb_ref, sem_b, tq)
    o_ref[...] = x_ref[...] + (ya_ref[...] + yb_ref[...])


def moe_combine(x, y, p1, p2, tq=256):
    m, d = x.shape
    tq = _pick(m, tq, 8)
    return pl.pallas_call(
        functools.partial(_moe_combine_kernel, tq=tq),
        grid_spec=pltpu.PrefetchScalarGridSpec(
            num_scalar_prefetch=2,
            grid=(m // tq,),
            in_specs=[pl.BlockSpec((tq, d), lambda i, a, b: (i, 0)), pl.BlockSpec(memory_space=pl.ANY)],
            out_specs=pl.BlockSpec((tq, d), lambda i, a, b: (i, 0)),
            scratch_shapes=[pltpu.VMEM((tq, d), F32), pltpu.VMEM((tq, d), F32),
                            pltpu.SemaphoreType.DMA(()), pltpu.SemaphoreType.DMA(())]),
        out_shape=jax.ShapeDtypeStruct((m, d), F32),
        compiler_params=_params(("arbitrary",), 7 * 4 * tq * d),
        name="moe_combine",
    )(p1, p2, x, y)


def _pad_to(a, axis, size):
    if a.shape[axis] == size:
        return a
    pad = [(0, 0)] * a.ndim
    pad[axis] = (0, size - a.shape[axis])
    return jnp.pad(a, pad)


def _even_layer(x2d, batch, seq, tables, norm_mix, w_in, conv_w, q_norm, k_norm, pe_k, pe_v, w_cmp_k, w_cmp_v,
                w_out, norm_ffn, w1, w3, w2):
    d = x2d.shape[1]
    ch = d // 2
    heads = ch // HEAD_DIM
    hpg = heads // KV_HEADS
    kv_w = KV_HEADS * HEAD_DIM
    main = 3 * ch + heads * HEAD_DIM + 6 * kv_w
    assert main + heads * N_BRANCH == w_in.shape[1]

    h = rmsnorm(x2d, norm_mix, BF16)
    proj = matmul([(h, w_in[:, :main].astype(BF16))], None, F32, 1024, 1024, "in_proj")
    wg = w_in[:, main:].reshape(d, KV_HEADS, hpg * N_BRANCH)
    wg = _pad_to(wg, 2, LANES).reshape(d, KV_HEADS * LANES).astype(BF16)
    gates = matmul([(h, wg)], None, F32, 1024, KV_HEADS * LANES, "gate_proj")

    y_a = short_conv(proj, conv_w.T, ch, seq)
    base = (3 * ch + heads * HEAD_DIM) // HEAD_DIM
    step = kv_w // HEAD_DIM
    blk = {"q": 3 * ch // HEAD_DIM, "kc": base, "vc": base + step, "ks": base + 2 * step,
           "vs": base + 3 * step, "kw": base + 4 * step, "vw": base + 5 * step}
    kn = k_norm.reshape(1, HEAD_DIM)
    kc, vc = compress(proj, blk["kc"], blk["vc"], pe_k, pe_v, w_cmp_k.astype(BF16), w_cmp_v.astype(BF16),
                      kn, batch, seq)
    y_b = nsa_attention(proj, gates, kc, vc, tables, q_norm.reshape(1, HEAD_DIM), kn, batch, seq, hpg, blk)

    wo = w_out.astype(BF16)
    x2d = matmul([(y_a, wo[:ch]), (y_b, wo[ch:])], x2d, F32, 1024, 512, "out_proj")

    ff = w1.shape[1]
    ffp = -(-ff // 1024) * 1024
    h2 = rmsnorm(x2d, norm_ffn, BF16)
    act = glu_matmul(h2, _pad_to(w1.astype(BF16), 1, ffp), _pad_to(w3.astype(BF16), 1, ffp), 1024, 512)
    return matmul_acc(act, _pad_to(w2.astype(BF16), 0, ffp), x2d, 1024, 1024, 1024, "dense_down")


def _odd_layer(x2d, seq, norm_mix, w_pool, pool_scale, norm_ffn, w_router, w1, w3, w2):
    m, d = x2d.shape
    x3, route = pool_and_route(x2d, norm_mix.reshape(1, d), w_pool.astype(BF16), pool_scale.reshape(1, d),
                               norm_ffn.reshape(1, d), w_router.T, seq)
    tm = min(512, m // 8)
    n_tiles = 2 * m // tm + N_EXPERTS
    tile_expert, row_token, row_gate, n_used, p1, p2 = _route_tables(route, tm, n_tiles)
    act = moe_glu(x3, norm_ffn.reshape(1, d), w1.astype(BF16), w3.astype(BF16), tile_expert, row_token, n_used,
                  tm, 512)
    y = moe_down(act, w2.astype(BF16), row_gate, tile_expert, n_used, tm, 1024)
    return moe_combine(x3, y, p1, p2)


def kernel(x, norm_mix_even, w_in_even, conv_w_even, q_norm_even, k_norm_even, cmp_pe_k_even, cmp_pe_v_even,
           w_cmp_k_even, w_cmp_v_even, w_out_even, norm_ffn_even, w1_dense, w3_dense, w2_dense,
           norm_mix_odd, w_pool_odd, pool_scale_odd, norm_ffn_odd, w_router_odd,
           w1_moe, w3_moe, w2_moe, rel_bias):
    batch, seq, d = x.shape
    depth = norm_mix_even.shape[0] + norm_mix_odd.shape[0]
    tables = bias_tables(rel_bias, seq)
    x2d = x.reshape(batch * seq, d)
    for layer in range(depth):
        i = layer // 2
        if layer % 2 == 0:
            x2d = _even_layer(x2d, batch, seq, tables, norm_mix_even[i], w_in_even[i], conv_w_even[i],
                              q_norm_even[i], k_norm_even[i], cmp_pe_k_even[i], cmp_pe_v_even[i],
                              w_cmp_k_even[i], w_cmp_v_even[i], w_out_even[i], norm_ffn_even[i],
                              w1_dense[i], w3_dense[i], w2_dense[i])
        else:
            x2d = _odd_layer(x2d, seq, norm_mix_odd[i], w_pool_odd[i], pool_scale_odd[i], norm_ffn_odd[i],
                             w_router_odd[i], w1_moe[i], w3_moe[i], w2_moe[i])
    return x2d.reshape(batch, seq, d)
```

```python
import functools
import math

import numpy as np
import jax
import jax.numpy as jnp
from jax import lax
from jax.experimental import pallas as pl
from jax.experimental.pallas import tpu as pltpu

F32 = jnp.float32
BF16 = jnp.bfloat16

HEAD_DIM = 128
KV_HEADS = 4
CONV_TAPS = 3
N_BRANCH = 3
CMP_BLOCK = 32
CMP_STRIDE = 16
SEL_BLOCK = 64
N_SELECT = 16
WINDOW = 512
FORCE_SCORE = 1e6
NEG_INF = -1e30
REL_BUCKETS = 32
REL_MAX_DIST = 128
POOL_WINDOWS = (2, 4, 8, 16)
N_EXPERTS = 8
EPS = 1e-6

LANES = 128
POOL_HALO = 16
CONV_HALO = 8
VMEM_CAP = 60 * 1024 * 1024
NT_DIMS = (((1,), (1,)), ((), ()))


def _pick(n, pref, mult=LANES):
    t = min(pref, n)
    while n % t or t % mult:
        t -= mult
    return t


def _params(sem, est_bytes):
    limit = int(min(max(est_bytes * 5 // 4 + (4 << 20), 16 << 20), VMEM_CAP))
    return pltpu.CompilerParams(dimension_semantics=sem, vmem_limit_bytes=limit)


def _rms(x, g):
    return x * lax.rsqrt(jnp.mean(x * x, axis=-1, keepdims=True) + EPS) * g


def _rmsnorm_kernel(x_ref, g_ref, o_ref):
    o_ref[...] = _rms(x_ref[...], g_ref[...]).astype(o_ref.dtype)


def rmsnorm(x, g, out_dtype, tm=256):
    m, d = x.shape
    tm = _pick(m, tm, 8)
    return pl.pallas_call(
        _rmsnorm_kernel,
        grid=(m // tm,),
        in_specs=[pl.BlockSpec((tm, d), lambda i: (i, 0)), pl.BlockSpec((1, d), lambda i: (0, 0))],
        out_specs=pl.BlockSpec((tm, d), lambda i: (i, 0)),
        out_shape=jax.ShapeDtypeStruct((m, d), out_dtype),
        compiler_params=_params(("parallel",), 2 * tm * d * 8),
        name="rmsnorm",
    )(x, g.reshape(1, d))


def _mm_kernel(*refs, n_pairs, has_res):
    o_ref = refs[-1]
    acc = None
    for p in range(n_pairs):
        d = jnp.dot(refs[2 * p][...], refs[2 * p + 1][...], preferred_element_type=F32)
        acc = d if acc is None else acc + d
    if has_res:
        acc = refs[2 * n_pairs][...] + acc
    o_ref[...] = acc.astype(o_ref.dtype)


def matmul(pairs, res, out_dtype, tm, tn, name, n=None):
    m = pairs[0][0].shape[0]
    n = pairs[0][1].shape[1] if n is None else n
    tm = _pick(m, tm)
    tn = _pick(n, tn)
    in_specs, args, est = [], [], 0
    for x, w, r in pairs:
        k = x.shape[1]
        in_specs += [pl.BlockSpec((tm, k), lambda i, j: (i, 0)), pl.BlockSpec((k, tn), lambda i, j, r=r: (r, j))]
        args += [x, w]
        est += 2 * 2 * (tm * k + k * tn)
    if res is not None:
        in_specs.append(pl.BlockSpec((tm, tn), lambda i, j: (i, j)))
        args.append(res)
        est += 2 * 4 * tm * tn
    est += 2 * 4 * tm * tn + 4 * tm * tn
    return pl.pallas_call(
        functools.partial(_mm_kernel, n_pairs=len(pairs), has_res=res is not None),
        grid=(m // tm, n // tn),
        in_specs=in_specs,
        out_specs=pl.BlockSpec((tm, tn), lambda i, j: (i, j)),
        out_shape=jax.ShapeDtypeStruct((m, n), out_dtype),
        compiler_params=_params(("parallel", "parallel"), est),
        name=name,
    )(*args)


def _glu_kernel(h_ref, w1_ref, w3_ref, o_ref):
    h = h_ref[...]
    a = jnp.dot(h, w1_ref[...], preferred_element_type=F32)
    b = jnp.dot(h, w3_ref[...], preferred_element_type=F32)
    o_ref[...] = (a * jax.nn.sigmoid(a) * b).astype(o_ref.dtype)


def glu_matmul(h, w1, w3, tm, tn):
    m, k = h.shape
    n = w1.shape[1]
    tm = _pick(m, tm)
    tn = _pick(n, tn)
    est = 2 * 2 * (tm * k + 2 * k * tn + tm * tn) + 3 * 4 * tm * tn
    return pl.pallas_call(
        _glu_kernel,
        grid=(m // tm, n // tn),
        in_specs=[pl.BlockSpec((tm, k), lambda i, j: (i, 0)),
                  pl.BlockSpec((k, tn), lambda i, j: (0, j)),
                  pl.BlockSpec((k, tn), lambda i, j: (0, j))],
        out_specs=pl.BlockSpec((tm, tn), lambda i, j: (i, j)),
        out_shape=jax.ShapeDtypeStruct((m, n), BF16),
        compiler_params=_params(("parallel", "parallel"), est),
        name="dense_glu",
    )(h, w1, w3)


def _mm_acc_kernel(x_ref, w_ref, r_ref, o_ref, acc_ref):
    k = pl.program_id(2)

    @pl.when(k == 0)
    def _():
        acc_ref[...] = jnp.zeros_like(acc_ref)

    acc_ref[...] += jnp.dot(x_ref[...], w_ref[...], preferred_element_type=F32)

    @pl.when(k == pl.num_programs(2) - 1)
    def _():
        o_ref[...] = r_ref[...] + acc_ref[...]


def matmul_acc(x, w, res, tm, tn, tk, name):
    m, k = x.shape
    n = w.shape[1]
    tm, tn, tk = _pick(m, tm), _pick(n, tn), _pick(k, tk)
    est = 2 * 2 * (tm * tk + tk * tn) + 5 * 4 * tm * tn
    return pl.pallas_call(
        _mm_acc_kernel,
        grid=(m // tm, n // tn, k // tk),
        in_specs=[pl.BlockSpec((tm, tk), lambda i, j, kk: (i, kk)),
                  pl.BlockSpec((tk, tn), lambda i, j, kk: (kk, j)),
                  pl.BlockSpec((tm, tn), lambda i, j, kk: (i, j))],
        out_specs=pl.BlockSpec((tm, tn), lambda i, j, kk: (i, j)),
        out_shape=jax.ShapeDtypeStruct((m, n), F32),
        scratch_shapes=[pltpu.VMEM((tm, tn), F32)],
        compiler_params=_params(("parallel", "parallel", "arbitrary"), est),
        name=name,
    )(x, w, res)


def _conv_kernel(ab_ref, ac_ref, au_ref, hc_ref, hu_ref, w_ref, o_ref, *, tq, seq):
    i = pl.program_id(0)
    first = (i * tq) % seq == 0
    v = ac_ref[...] * au_ref[...]
    hv = jnp.where(first, 0.0, hc_ref[...] * hu_ref[...])
    rows = lax.broadcasted_iota(jnp.int32, v.shape, 0)
    v1 = jnp.where(rows == 0, hv[CONV_HALO - 1:CONV_HALO], pltpu.roll(v, 1, 0))
    v2 = jnp.where(rows == 0, hv[CONV_HALO - 2:CONV_HALO - 1],
                   jnp.where(rows == 1, hv[CONV_HALO - 1:CONV_HALO], pltpu.roll(v, 2, 0)))
    y = w_ref[0:1, :] * v2 + w_ref[1:2, :] * v1 + w_ref[2:3, :] * v
    o_ref[...] = (ab_ref[...] * y).astype(o_ref.dtype)


def short_conv(proj, conv_w_t, ch, seq, tq=256, tc=512):
    m = proj.shape[0]
    tq = _pick(seq, tq, CONV_HALO)
    tc = _pick(ch, tc)
    nj = ch // tc
    hb = tq // CONV_HALO
    halo = lambda off: pl.BlockSpec((CONV_HALO, tc), lambda i, j: (jnp.maximum(i * hb - 1, 0), off + j))
    return pl.pallas_call(
        functools.partial(_conv_kernel, tq=tq, seq=seq),
        grid=(m // tq, nj),
        in_specs=[pl.BlockSpec((tq, tc), lambda i, j: (i, j)),
                  pl.BlockSpec((tq, tc), lambda i, j: (i, nj + j)),
                  pl.BlockSpec((tq, tc), lambda i, j: (i, 2 * nj + j)),
                  halo(nj), halo(2 * nj),
                  pl.BlockSpec((CONV_TAPS, tc), lambda i, j: (0, j))],
        out_specs=pl.BlockSpec((tq, tc), lambda i, j: (i, j)),
        out_shape=jax.ShapeDtypeStruct((m, ch), BF16),
        compiler_params=_params(("parallel", "parallel"), 2 * 4 * 4 * tq * tc + 6 * 4 * tq * tc),
        name="short_conv",
    )(proj, proj, proj, proj, proj, conv_w_t)


def _rel_bucket_np(dist):
    dist = np.maximum(dist, 0)
    exact = REL_BUCKETS // 2
    d = np.maximum(dist, exact).astype(np.float32)
    large = exact + (np.log(d / np.float32(exact)) / np.float32(math.log(REL_MAX_DIST / exact))
                     * np.float32(REL_BUCKETS - exact)).astype(np.int32)
    return np.where(dist < exact, dist, np.minimum(large, REL_BUCKETS - 1)).astype(np.int32)


WIN_TILES = WINDOW // LANES
N_BIAS_TILES = 2 * WIN_TILES + 1


def _bias_index_tables(seq):
    r = np.arange(LANES)[:, None]
    c = np.arange(LANES)[None, :]
    far = np.full((LANES, LANES), REL_BUCKETS - 1, np.int32)
    zero = np.zeros((LANES, LANES), np.float32)
    neg = np.full((LANES, LANES), NEG_INF, np.float32)
    assert _rel_bucket_np(np.arange(LANES + 1, 4 * seq)).min() == REL_BUCKETS - 1
    idx, add = [], []
    for d in range(-WIN_TILES, WIN_TILES + 1):
        if d < 0:
            idx.append(far), add.append(neg)
        elif d == 0:
            idx.append(_rel_bucket_np(r - c)), add.append(np.where(r >= c, zero, neg))
        elif d == 1:
            idx.append(_rel_bucket_np(LANES + r - c)), add.append(zero)
        elif d < WIN_TILES:
            idx.append(far), add.append(zero)
        else:
            idx.append(far), add.append(np.where(c > r, zero, neg))
    n_tile_rows = len(idx) * LANES
    t = np.arange(seq)[:, None]
    idx.append(_rel_bucket_np(t - (c * CMP_STRIDE + CMP_BLOCK - 1)))
    add.append(np.zeros((seq, LANES), np.float32))
    idx = np.concatenate(idx, axis=0).astype(np.int32)
    add = np.concatenate(add, axis=0).astype(np.float32)
    shift = (np.arange(idx.shape[0])[:, None] < n_tile_rows).astype(np.float32) * np.ones((1, LANES), np.float32)
    return idx, shift, add


def _bias_kernel(tbl_ref, idx_ref, shift_ref, add_ref, o_ref):
    h = pl.program_id(0)
    idx = idx_ref[...]
    acc = jnp.zeros(idx.shape, F32)
    for b in range(REL_BUCKETS):
        acc = jnp.where(idx == b, tbl_ref[b, h], acc)
    o_ref[0] = acc - shift_ref[...] * tbl_ref[REL_BUCKETS - 1, h] + add_ref[...]


def bias_tables(rel_bias, seq):
    heads = rel_bias.shape[1]
    idx, shift, add = (jnp.asarray(a) for a in _bias_index_tables(seq))
    rows = idx.shape[0]
    full = pl.BlockSpec((rows, LANES), lambda h: (0, 0))
    return pl.pallas_call(
        _bias_kernel,
        grid=(heads,),
        in_specs=[pl.BlockSpec(memory_space=pltpu.SMEM), full, full, full],
        out_specs=pl.BlockSpec((1, rows, LANES), lambda h: (h, 0, 0)),
        out_shape=jax.ShapeDtypeStruct((heads, rows, LANES), F32),
        compiler_params=_params(("parallel",), 12 * 4 * rows * LANES),
        name="rel_bias_tables",
    )(rel_bias, idx, shift, add)


def _compress_kernel(k_ref, v_ref, pek_ref, pev_ref, wk_ref, wv_ref, kn_ref, kc_ref, vc_ref, *, nch):
    half = CMP_BLOCK // CMP_STRIDE
    assert half == 2

    def comp(x_ref, pe_ref, w_ref):
        lo = jnp.zeros((nch, HEAD_DIM), F32)
        hi = jnp.zeros((nch, HEAD_DIM), F32)
        for l in range(CMP_STRIDE):
            rows = x_ref[pl.ds(l, nch, stride=CMP_STRIDE), :]
            lo += jnp.dot((rows + pe_ref[l:l + 1, :]).astype(BF16), w_ref[l], preferred_element_type=F32)
            hi += jnp.dot((rows + pe_ref[CMP_STRIDE + l:CMP_STRIDE + l + 1, :]).astype(BF16),
                          w_ref[CMP_STRIDE + l], preferred_element_type=F32)
        return lo + pltpu.roll(hi, nch - 1, 0)

    kc_ref[0, 0] = _rms(comp(k_ref, pek_ref, wk_ref), kn_ref[...])
    vc_ref[0, 0] = comp(v_ref, pev_ref, wv_ref)


def compress(proj, kc_blk, vc_blk, pe_k, pe_v, w_k, w_v, k_norm, batch, seq):
    nch = seq // CMP_STRIDE
    wspec = pl.BlockSpec((CMP_BLOCK, HEAD_DIM, HEAD_DIM), lambda b, g: (0, 0, 0))
    pespec = pl.BlockSpec((CMP_BLOCK, HEAD_DIM), lambda b, g: (0, 0))
    ospec = pl.BlockSpec((1, 1, nch, HEAD_DIM), lambda b, g: (b, g, 0, 0))
    oshape = jax.ShapeDtypeStruct((batch, KV_HEADS, nch, HEAD_DIM), F32)
    return pl.pallas_call(
        functools.partial(_compress_kernel, nch=nch),
        grid=(batch, KV_HEADS),
        in_specs=[pl.BlockSpec((seq, HEAD_DIM), lambda b, g: (b, kc_blk + g)),
                  pl.BlockSpec((seq, HEAD_DIM), lambda b, g: (b, vc_blk + g)),
                  pespec, pespec, wspec, wspec,
                  pl.BlockSpec((1, HEAD_DIM), lambda b, g: (0, 0))],
        out_specs=[ospec, ospec],
        out_shape=[oshape, oshape],
        compiler_params=_params(("parallel", "parallel"), 4 * 4 * seq * HEAD_DIM + (4 << 20)),
        name="nsa_compress",
    )(proj, proj, pe_k, pe_v, w_k, w_v, k_norm)


SEL_CHUNK = 512


def _nsa_kernel(q_ref, ks_ref, vs_ref, kw_ref, vw_ref, kc_ref, vc_ref, wt_ref, bc_ref, gt_ref,
                qn_ref, kn_ref, cov_ref, oh_ref, o_ref, ksa, vsb, kwn, vwb, *, hpg, n_cmp, n_top):
    qi = pl.program_id(2)
    rows_all = hpg * LANES
    scale = HEAD_DIM ** -0.5
    n_sel_blocks = cov_ref.shape[0]
    chunk_tiles = SEL_CHUNK // LANES

    @pl.when(qi == 0)
    def _():
        ksa[:, :HEAD_DIM] = _rms(ks_ref[...], kn_ref[...]).astype(BF16)
        ksa[:, HEAD_DIM:] = oh_ref[...]
        kwn[...] = _rms(kw_ref[...], kn_ref[...]).astype(BF16)
        vsb[...] = vs_ref[...].astype(BF16)
        vwb[...] = vw_ref[...].astype(BF16)

    qs = [_rms(q_ref[:, h * HEAD_DIM:(h + 1) * HEAD_DIM], qn_ref[...]).astype(BF16) for h in range(hpg)]
    q = jnp.concatenate(qs, axis=0) if hpg > 1 else qs[0]

    def bias_tile(d):
        off = pl.multiple_of((d + WIN_TILES) * LANES, LANES)
        return wt_ref[:, pl.ds(off, LANES), :].reshape(rows_all, LANES)

    rowq = lax.broadcasted_iota(jnp.int32, (rows_all, LANES), 0) & (LANES - 1)
    col = lax.broadcasted_iota(jnp.int32, (rows_all, LANES), 1)

    lc = lax.dot_general(q, kc_ref[0, 0].astype(BF16), NT_DIMS, preferred_element_type=F32) * scale
    lc = lc + bc_ref[...].reshape(rows_all, LANES)
    mc = ((qi * LANES + rowq - (col * CMP_STRIDE + CMP_BLOCK - 1)) >= 0) & (col < n_cmp)
    zc = jnp.where(mc, lc, NEG_INF)
    ec = jnp.exp(zc - jnp.max(zc, axis=-1, keepdims=True))
    pc = ec / jnp.sum(ec, axis=-1, keepdims=True) * mc.astype(F32)
    o_cmp = jnp.dot(pc.astype(BF16), vc_ref[0, 0].astype(BF16), preferred_element_type=F32)

    ps = pc[0:LANES]
    for h in range(1, hpg):
        ps = ps + pc[h * LANES:(h + 1) * LANES]
    p1 = ps.astype(BF16)
    r1 = ps - p1.astype(F32)
    p2 = r1.astype(BF16)
    p3 = (r1 - p2.astype(F32)).astype(BF16)
    cov = cov_ref[...]
    score = (lax.dot_general(cov, p1, NT_DIMS, preferred_element_type=F32)
             + lax.dot_general(cov, p2, NT_DIMS, preferred_element_type=F32)
             + lax.dot_general(cov, p3, NT_DIMS, preferred_element_type=F32))
    jj = lax.broadcasted_iota(jnp.int32, (n_sel_blocks, LANES), 0)
    ql = lax.broadcasted_iota(jnp.int32, (n_sel_blocks, LANES), 1)
    cur = (LANES // SEL_BLOCK) * qi + ql // SEL_BLOCK
    forced = (jj == 0) | (jj == cur) | (jj == cur - 1)
    score = jnp.where(forced, FORCE_SCORE, jnp.where(jj > cur, -FORCE_SCORE, score))
    rank = jnp.zeros((n_sel_blocks, LANES), F32)
    for j2 in range(n_sel_blocks):
        other = score[j2:j2 + 1, :]
        rank += ((other > score) | ((other == score) & (j2 < jj))).astype(F32)
    sel_t = ((rank < n_top) & (jj <= cur)).astype(BF16)
    sel_t = jnp.concatenate([sel_t, jnp.ones((LANES - n_sel_blocks, LANES), BF16)], axis=0)
    eye = (lax.broadcasted_iota(jnp.int32, (LANES, LANES), 0)
           == lax.broadcasted_iota(jnp.int32, (LANES, LANES), 1)).astype(BF16)
    selq = lax.dot_general(eye, sel_t, NT_DIMS, preferred_element_type=F32)
    sel_neg = ((selq - 1.0) * (-NEG_INF)).astype(BF16)
    sel_neg = jnp.concatenate([sel_neg] * hpg, axis=0) if hpg > 1 else sel_neg
    qa = jnp.concatenate([q, sel_neg], axis=1)

    def sel_chunk(c, carry, near):
        m, l, acc = carry
        off = pl.multiple_of(c * SEL_CHUNK, SEL_CHUNK)
        s = lax.dot_general(qa, ksa[pl.ds(off, SEL_CHUNK), :], NT_DIMS, preferred_element_type=F32) * scale
        if near:
            s = jnp.concatenate([s[:, u * LANES:(u + 1) * LANES]
                                 + bias_tile(jnp.clip(qi - (c * chunk_tiles + u), 0, 2))
                                 for u in range(chunk_tiles)], axis=1)
        m_new = jnp.maximum(m, jnp.max(s, axis=-1, keepdims=True))
        alpha = jnp.exp(m - m_new)
        p = jnp.exp(s - m_new)
        l = alpha * l + jnp.sum(p, axis=-1, keepdims=True)
        acc = alpha * acc + jnp.dot(p.astype(BF16), vsb[pl.ds(off, SEL_CHUNK), :], preferred_element_type=F32)
        return m_new, l, acc

    init = (jnp.full((rows_all, 1), NEG_INF, F32), jnp.zeros((rows_all, 1), F32),
            jnp.zeros((rows_all, HEAD_DIM), F32))
    n_far = (jnp.maximum(qi, 1) - 1) // chunk_tiles
    carry = lax.fori_loop(0, n_far, functools.partial(sel_chunk, near=False), init)
    _, l_s, a_s = lax.fori_loop(n_far, qi // chunk_tiles + 1, functools.partial(sel_chunk, near=True), carry)
    o_sel = a_s / l_s

    win_keys = WINDOW + LANES
    kt0 = jnp.maximum(qi - WIN_TILES, 0)
    woff = pl.multiple_of(kt0 * LANES, LANES)
    sw = lax.dot_general(q, kwn[pl.ds(woff, win_keys), :], NT_DIMS, preferred_element_type=F32) * scale
    zw = jnp.concatenate([sw[:, u * LANES:(u + 1) * LANES] + bias_tile(qi - (kt0 + u))
                          for u in range(win_keys // LANES)], axis=1)
    pw = jnp.exp(zw - jnp.max(zw, axis=-1, keepdims=True))
    o_win = (jnp.dot(pw.astype(BF16), vwb[pl.ds(woff, win_keys), :], preferred_element_type=F32)
             / jnp.sum(pw, axis=-1, keepdims=True))

    gt = jax.nn.sigmoid(gt_ref[...])
    for h in range(hpg):
        sl = slice(h * LANES, (h + 1) * LANES)
        c0 = h * N_BRANCH
        o = (gt[:, c0:c0 + 1] * o_cmp[sl] + gt[:, c0 + 1:c0 + 2] * o_sel[sl]
             + gt[:, c0 + 2:c0 + 3] * o_win[sl])
        o_ref[:, h * HEAD_DIM:(h + 1) * HEAD_DIM] = o.astype(o_ref.dtype)


def _cover_t(seq):
    nc = seq // CMP_STRIDE - CMP_BLOCK // CMP_STRIDE + 1
    ns = seq // SEL_BLOCK
    c_start = np.arange(nc) * CMP_STRIDE
    c_end = c_start + CMP_BLOCK - 1
    s_start = np.arange(ns) * SEL_BLOCK
    cover = (c_start[:, None] <= s_start[None, :] + SEL_BLOCK - 1) & (c_end[:, None] >= s_start[None, :])
    out = np.zeros((ns, seq // CMP_STRIDE), np.float32)
    out[:, :nc] = cover.T
    return out


def nsa_attention(proj, gates, kc, vc, tables, q_norm, k_norm, batch, seq, hpg, blk):
    heads = KV_HEADS * hpg
    nq = seq // LANES
    nch = seq // CMP_STRIDE
    assert nch == LANES, "compressed keys must fill exactly one lane tile"
    n_cmp = nch - CMP_BLOCK // CMP_STRIDE + 1
    ns = seq // SEL_BLOCK
    assert seq % SEL_CHUNK == 0 and ns <= LANES
    cov = jnp.asarray(_cover_t(seq), BF16)
    onehot = jnp.asarray(np.arange(seq)[:, None] // SEL_BLOCK == np.arange(LANES)[None, :], BF16)
    qw = hpg * HEAD_DIM
    q_blk = blk["q"] * HEAD_DIM // qw
    kv = lambda name: pl.BlockSpec((seq, HEAD_DIM), lambda b, g, i: (b, blk[name] + g))
    cspec = pl.BlockSpec((1, 1, nch, HEAD_DIM), lambda b, g, i: (b, g, 0, 0))
    vec = pl.BlockSpec((1, HEAD_DIM), lambda b, g, i: (0, 0))
    rows_all = hpg * LANES
    est = (2 * 4 * 4 * seq * HEAD_DIM + 5 * 2 * seq * HEAD_DIM + 2 * 2 * seq * LANES
           + 2 * 4 * hpg * (N_BIAS_TILES + 1) * LANES * LANES + 6 * 4 * rows_all * (WINDOW + LANES) + (8 << 20))
    return pl.pallas_call(
        functools.partial(_nsa_kernel, hpg=hpg, n_cmp=n_cmp, n_top=min(N_SELECT, ns)),
        grid=(batch, KV_HEADS, nq),
        in_specs=[pl.BlockSpec((LANES, qw), lambda b, g, i: (b * nq + i, q_blk + g)),
                  kv("ks"), kv("vs"), kv("kw"), kv("vw"), cspec, cspec,
                  pl.BlockSpec((hpg, N_BIAS_TILES * LANES, LANES), lambda b, g, i: (g, 0, 0)),
                  pl.BlockSpec((hpg, LANES, LANES), lambda b, g, i: (g, N_BIAS_TILES + i, 0)),
                  pl.BlockSpec((LANES, LANES), lambda b, g, i: (b * nq + i, g)),
                  vec, vec,
                  pl.BlockSpec((ns, nch), lambda b, g, i: (0, 0)),
                  pl.BlockSpec((seq, LANES), lambda b, g, i: (0, 0))],
        out_specs=pl.BlockSpec((LANES, qw), lambda b, g, i: (b * nq + i, g)),
        out_shape=jax.ShapeDtypeStruct((batch * seq, heads * HEAD_DIM), BF16),
        scratch_shapes=[pltpu.VMEM((seq, 2 * HEAD_DIM), BF16)] + [pltpu.VMEM((seq, HEAD_DIM), BF16)] * 3,
        compiler_params=_params(("parallel", "parallel", "arbitrary"), est),
        name="nsa_attention",
    )(proj, proj, proj, proj, proj, kc, vc, tables, tables, gates, q_norm, k_norm, cov, onehot)


def _pool_kernel(x_ref, halo_ref, gm_ref, wp_ref, ps_ref, gf_ref, wr_ref, x_out, route_out, *, tq, seq, cg):
    i = pl.program_id(0)
    start = (i * tq) % seq
    x = x_ref[...]
    h = _rms(x, gm_ref[...])
    hh = jnp.where(start == 0, 0.0, _rms(halo_ref[...], gm_ref[...]))
    t1 = (start + lax.broadcasted_iota(jnp.int32, (tq, 1), 0) + 1).astype(F32)
    ys = []
    for gi, w in enumerate(POOL_WINDOWS):
        sl = slice(gi * cg, (gi + 1) * cg)
        s = jnp.concatenate([hh[:, sl], h[:, sl]], axis=0)
        span = 1
        while span < w:
            s = s + pltpu.roll(s, span, 0)
            span *= 2
        dm = s[POOL_HALO:] / jnp.minimum(t1, float(w)) - h[:, sl]
        ys.append(jnp.dot(dm.astype(BF16), wp_ref[gi], preferred_element_type=F32))
    x3 = x + jnp.concatenate(ys, axis=1) * ps_ref[...]
    x_out[...] = x3

    h4 = _rms(x3, gf_ref[...])
    lane = lax.broadcasted_iota(jnp.int32, (tq, LANES), 1)
    logits = jnp.full((tq, LANES), -jnp.inf, F32)
    for e in range(N_EXPERTS):
        logits = jnp.where(lane == e, jnp.sum(h4 * wr_ref[e:e + 1, :], axis=-1, keepdims=True), logits)
    m1 = jnp.max(logits, axis=-1, keepdims=True)
    i1 = jnp.min(jnp.where(logits == m1, lane, LANES), axis=-1, keepdims=True)
    rest = jnp.where(lane == i1, -jnp.inf, logits)
    m2 = jnp.max(rest, axis=-1, keepdims=True)
    i2 = jnp.min(jnp.where(rest == m2, lane, LANES), axis=-1, keepdims=True)
    e2 = jnp.exp(m2 - m1)
    den = 1.0 + e2
    route = jnp.where(lane == 0, i1.astype(F32),
                      jnp.where(lane == 1, i2.astype(F32),
                                jnp.where(lane == 2, 1.0 / den, jnp.where(lane == 3, e2 / den, 0.0))))
    route_out[...] = route


def pool_and_route(x, g_mix, w_pool, pool_scale, g_ffn, w_router_t, seq, tq=256):
    m, d = x.shape
    cg = d // len(POOL_WINDOWS)
    tq = _pick(seq, tq, POOL_HALO)
    hb = tq // POOL_HALO
    vec = pl.BlockSpec((1, d), lambda i: (0, 0))
    est = 2 * 2 * 4 * tq * d + 2 * 2 * len(POOL_WINDOWS) * cg * cg + 8 * 4 * tq * d
    return pl.pallas_call(
        functools.partial(_pool_kernel, tq=tq, seq=seq, cg=cg),
        grid=(m // tq,),
        in_specs=[pl.BlockSpec((tq, d), lambda i: (i, 0)),
                  pl.BlockSpec((POOL_HALO, d), lambda i: (jnp.maximum(i * hb - 1, 0), 0)),
                  vec,
                  pl.BlockSpec((len(POOL_WINDOWS), cg, cg), lambda i: (0, 0, 0)),
                  vec, vec,
                  pl.BlockSpec((N_EXPERTS, d), lambda i: (0, 0))],
        out_specs=[pl.BlockSpec((tq, d), lambda i: (i, 0)), pl.BlockSpec((tq, LANES), lambda i: (i, 0))],
        out_shape=[jax.ShapeDtypeStruct((m, d), F32), jax.ShapeDtypeStruct((m, LANES), F32)],
        compiler_params=_params(("parallel",), est),
        name="pool_mixer_router",
    )(x, x, g_mix, w_pool, pool_scale, g_ffn, w_router_t)


def _route_tables(route, tm, n_tiles):
    n = route.shape[0]
    i1 = route[:, 0].astype(jnp.int32)
    i2 = route[:, 1].astype(jnp.int32)
    onehot = jax.nn.one_hot(i1, N_EXPERTS, dtype=jnp.int32) + jax.nn.one_hot(i2, N_EXPERTS, dtype=jnp.int32)
    padded = (jnp.sum(onehot, axis=0) + tm - 1) // tm * tm
    end = jnp.cumsum(padded)
    pos = (end - padded)[None, :] + jnp.cumsum(onehot, axis=0) - onehot
    p1 = jnp.take_along_axis(pos, i1[:, None], axis=1)[:, 0]
    p2 = jnp.take_along_axis(pos, i2[:, None], axis=1)[:, 0]
    tok = jnp.arange(n, dtype=jnp.int32)
    rows = n_tiles * tm
    row_token = jnp.zeros((rows,), jnp.int32).at[jnp.concatenate([p1, p2])].set(jnp.concatenate([tok, tok]))
    n_used = (end[-1] // tm).astype(jnp.int32)
    tile = jnp.arange(n_tiles, dtype=jnp.int32)
    tile_expert = jnp.searchsorted(end, tile * tm, side="right").astype(jnp.int32)
    tile_expert = jnp.where(tile < n_used, tile_expert, tile_expert[n_used - 1])
    return tile_expert, row_token, n_used.reshape(1), p1.astype(jnp.int32), p2.astype(jnp.int32)


def _gather_start(idx_ref, base, src_hbm, dst_ref, sem, count):
    def issue(r, carry):
        pltpu.make_async_copy(src_hbm.at[pl.ds(idx_ref[base + r], 1), :], dst_ref.at[pl.ds(r, 1), :], sem).start()
        return carry

    lax.fori_loop(0, count, issue, 0)


def _gather_wait(src_hbm, dst_ref, sem, count):
    pltpu.make_async_copy(src_hbm.at[pl.ds(0, count), :], dst_ref, sem).wait()


def _moe_glu_kernel(te_ref, rt_ref, nu_ref, x_hbm, g_ref, w1_ref, w3_ref, o_ref, xg_ref, h_ref, sem, *, tm):
    i = pl.program_id(0)
    f = pl.program_id(1)
    valid = i < nu_ref[0]
    slot = i % 2

    @pl.when((i == 0) & (f == 0) & valid)
    def _():
        _gather_start(rt_ref, 0, x_hbm, xg_ref.at[0], sem.at[0], tm)

    @pl.when(valid & (f == 0))
    def _():
        _gather_wait(x_hbm, xg_ref.at[slot], sem.at[slot], tm)
        h_ref[...] = _rms(xg_ref[slot], g_ref[...]).astype(BF16)

    @pl.when((i + 1 < nu_ref[0]) & (f == 0))
    def _():
        _gather_start(rt_ref, (i + 1) * tm, x_hbm, xg_ref.at[1 - slot], sem.at[1 - slot], tm)

    @pl.when(valid)
    def _():
        h = h_ref[...]
        a = jnp.dot(h, w1_ref[0], preferred_element_type=F32)
        b = jnp.dot(h, w3_ref[0], preferred_element_type=F32)
        o_ref[...] = (a * jax.nn.sigmoid(a) * b).astype(o_ref.dtype)

    @pl.when(jnp.logical_not(valid))
    def _():
        o_ref[...] = jnp.zeros_like(o_ref)


def moe_glu(x, g_ffn, w1, w3, tile_expert, row_token, n_used, tm, tf):
    d = x.shape[1]
    ff = w1.shape[2]
    tf = _pick(ff, tf)
    nf = ff // tf
    n_tiles = tile_expert.shape[0]
    wspec = pl.BlockSpec((1, d, tf), lambda i, f, te, rt, nu: (te[i], 0, jnp.where(i < nu[0], f, nf - 1)))
    est = 2 * 4 * tm * d + 2 * tm * d + 2 * 2 * 2 * d * tf + 2 * 2 * tm * tf + 3 * 4 * tm * tf + 4 * tm * d
    return pl.pallas_call(
        functools.partial(_moe_glu_kernel, tm=tm),
        grid_spec=pltpu.PrefetchScalarGridSpec(
            num_scalar_prefetch=3,
            grid=(n_tiles, nf),
            in_specs=[pl.BlockSpec(memory_space=pl.ANY),
                      pl.BlockSpec((1, d), lambda i, f, te, rt, nu: (0, 0)),
                      wspec, wspec],
            out_specs=pl.BlockSpec((tm, tf), lambda i, f, te, rt, nu: (i, f)),
            scratch_shapes=[pltpu.VMEM((2, tm, d), F32), pltpu.VMEM((tm, d), BF16),
                            pltpu.SemaphoreType.DMA((2,))]),
        out_shape=jax.ShapeDtypeStruct((n_tiles * tm, ff), BF16),
        compiler_params=_params(("arbitrary", "arbitrary"), est),
        name="moe_glu",
    )(tile_expert, row_token, n_used, x, g_ffn, w1, w3)


def _moe_down_kernel(te_ref, nu_ref, a_ref, w_ref, o_ref):
    valid = pl.program_id(0) < nu_ref[0]

    @pl.when(valid)
    def _():
        o_ref[...] = jnp.dot(a_ref[...], w_ref[0], preferred_element_type=F32)

    @pl.when(jnp.logical_not(valid))
    def _():
        o_ref[...] = jnp.zeros_like(o_ref)


def moe_down(act, w2, tile_expert, n_used, tm, tn):
    ff = act.shape[1]
    d = w2.shape[2]
    tn = _pick(d, tn)
    nn = d // tn
    n_tiles = tile_expert.shape[0]
    est = 2 * 2 * (tm * ff + ff * tn) + 3 * 4 * tm * tn
    return pl.pallas_call(
        _moe_down_kernel,
        grid_spec=pltpu.PrefetchScalarGridSpec(
            num_scalar_prefetch=2,
            grid=(n_tiles, nn),
            in_specs=[pl.BlockSpec((tm, ff), lambda i, j, te, nu: (i, 0)),
                      pl.BlockSpec((1, ff, tn), lambda i, j, te, nu: (te[i], 0, jnp.where(i < nu[0], j, nn - 1)))],
            out_specs=pl.BlockSpec((tm, tn), lambda i, j, te, nu: (i, j))),
        out_shape=jax.ShapeDtypeStruct((n_tiles * tm, d), F32),
        compiler_params=_params(("arbitrary", "arbitrary"), est),
        name="moe_down",
    )(tile_expert, n_used, act, w2)


def _moe_combine_kernel(p1_ref, p2_ref, x_ref, r_ref, y_hbm, o_ref, ya_ref, yb_ref, sem_a, sem_b, *, tq):
    i = pl.program_id(0)
    slot = i % 2

    def start(step, s):
        _gather_start(p1_ref, step * tq, y_hbm, ya_ref.at[s], sem_a.at[s], tq)
        _gather_start(p2_ref, step * tq, y_hbm, yb_ref.at[s], sem_b.at[s], tq)

    @pl.when(i == 0)
    def _():
        start(0, 0)

    @pl.when(i + 1 < pl.num_programs(0))
    def _():
        start(i + 1, 1 - slot)

    _gather_wait(y_hbm, ya_ref.at[slot], sem_a.at[slot], tq)
    _gather_wait(y_hbm, yb_ref.at[slot], sem_b.at[slot], tq)
    r = r_ref[...]
    o_ref[...] = x_ref[...] + (r[:, 2:3] * ya_ref[slot] + r[:, 3:4] * yb_ref[slot])


def moe_combine(x, route, y, p1, p2, tq=256):
    m, d = x.shape
    tq = _pick(m, tq, 8)
    return pl.pallas_call(
        functools.partial(_moe_combine_kernel, tq=tq),
        grid_spec=pltpu.PrefetchScalarGridSpec(
            num_scalar_prefetch=2,
            grid=(m // tq,),
            in_specs=[pl.BlockSpec((tq, d), lambda i, a, b: (i, 0)),
                      pl.BlockSpec((tq, LANES), lambda i, a, b: (i, 0)),
                      pl.BlockSpec(memory_space=pl.ANY)],
            out_specs=pl.BlockSpec((tq, d), lambda i, a, b: (i, 0)),
            scratch_shapes=[pltpu.VMEM((2, tq, d), F32), pltpu.VMEM((2, tq, d), F32),
                            pltpu.SemaphoreType.DMA((2,)), pltpu.SemaphoreType.DMA((2,))]),
        out_shape=jax.ShapeDtypeStruct((m, d), F32),
        compiler_params=_params(("arbitrary",), 9 * 4 * tq * d),
        name="moe_combine",
    )(p1, p2, x, route, y)


def _pad_to(a, axis, size):
    if a.shape[axis] == size:
        return a
    pad = [(0, 0)] * a.ndim
    pad[axis] = (0, size - a.shape[axis])
    return jnp.pad(a, pad)


def _even_layer(x2d, batch, seq, tables, norm_mix, w_in, conv_w, q_norm, k_norm, pe_k, pe_v, w_cmp_k, w_cmp_v,
                w_out, norm_ffn, w1, w3, w2):
    d = x2d.shape[1]
    ch = d // 2
    heads = ch // HEAD_DIM
    hpg = heads // KV_HEADS
    kv_w = KV_HEADS * HEAD_DIM
    main = 3 * ch + heads * HEAD_DIM + 6 * kv_w
    assert main + heads * N_BRANCH == w_in.shape[1]

    h = rmsnorm(x2d, norm_mix, BF16)
    proj = matmul([(h, w_in.astype(BF16), 0)], None, F32, 1024, 1024, "in_proj", n=main)
    wg = w_in[:, main:].reshape(d, KV_HEADS, hpg * N_BRANCH)
    wg = _pad_to(wg, 2, LANES).reshape(d, KV_HEADS * LANES).astype(BF16)
    gates = matmul([(h, wg, 0)], None, F32, 1024, KV_HEADS * LANES, "gate_proj")

    y_a = short_conv(proj, conv_w.T, ch, seq)
    base = (3 * ch + heads * HEAD_DIM) // HEAD_DIM
    step = kv_w // HEAD_DIM
    blk = {"q": 3 * ch // HEAD_DIM, "kc": base, "vc": base + step, "ks": base + 2 * step,
           "vs": base + 3 * step, "kw": base + 4 * step, "vw": base + 5 * step}
    kn = k_norm.reshape(1, HEAD_DIM)
    kc, vc = compress(proj, blk["kc"], blk["vc"], pe_k, pe_v, w_cmp_k.astype(BF16), w_cmp_v.astype(BF16),
                      kn, batch, seq)
    y_b = nsa_attention(proj, gates, kc, vc, tables, q_norm.reshape(1, HEAD_DIM), kn, batch, seq, hpg, blk)

    wo = w_out.astype(BF16)
    x2d = matmul([(y_a, wo, 0), (y_b, wo, 1)], x2d, F32, 1024, 512, "out_proj")

    h2 = rmsnorm(x2d, norm_ffn, BF16)
    act = glu_matmul(h2, w1.astype(BF16), w3.astype(BF16), 1024, 256)
    return matmul_acc(act, w2.astype(BF16), x2d, 512, 1024, 5632, "dense_down")


def _odd_layer(x2d, seq, norm_mix, w_pool, pool_scale, norm_ffn, w_router, w1, w3, w2):
    m, d = x2d.shape
    x3, route = pool_and_route(x2d, norm_mix.reshape(1, d), w_pool.astype(BF16), pool_scale.reshape(1, d),
                               norm_ffn.reshape(1, d), w_router.T, seq)
    tm = min(512, m // 8)
    n_tiles = 2 * m // tm + N_EXPERTS
    tile_expert, row_token, n_used, p1, p2 = _route_tables(route, tm, n_tiles)
    act = moe_glu(x3, norm_ffn.reshape(1, d), w1.astype(BF16), w3.astype(BF16), tile_expert, row_token, n_used,
                  tm, 512)
    y = moe_down(act, w2.astype(BF16), tile_expert, n_used, tm, 1024)
    return moe_combine(x3, route, y, p1, p2)


def kernel(x, norm_mix_even, w_in_even, conv_w_even, q_norm_even, k_norm_even, cmp_pe_k_even, cmp_pe_v_even,
           w_cmp_k_even, w_cmp_v_even, w_out_even, norm_ffn_even, w1_dense, w3_dense, w2_dense,
           norm_mix_odd, w_pool_odd, pool_scale_odd, norm_ffn_odd, w_router_odd,
           w1_moe, w3_moe, w2_moe, rel_bias):
    batch, seq, d = x.shape
    depth = norm_mix_even.shape[0] + norm_mix_odd.shape[0]
    tables = bias_tables(rel_bias, seq)
    x2d = x.reshape(batch * seq, d)
    for layer in range(depth):
        i = layer // 2
        if layer % 2 == 0:
            x2d = _even_layer(x2d, batch, seq, tables, norm_mix_even[i], w_in_even[i], conv_w_even[i],
                              q_norm_even[i], k_norm_even[i], cmp_pe_k_even[i], cmp_pe_v_even[i],
                              w_cmp_k_even[i], w_cmp_v_even[i], w_out_even[i], norm_ffn_even[i],
                              w1_dense[i], w3_dense[i], w2_dense[i])
        else:
            x2d = _odd_layer(x2d, seq, norm_mix_odd[i], w_pool_odd[i], pool_scale_odd[i], norm_ffn_odd[i],
                             w_router_odd[i], w1_moe[i], w3_moe[i], w2_moe[i])
    return x2d.reshape(batch, seq, d)
```

```python
import functools
import math

import numpy as np
import jax
import jax.numpy as jnp
from jax import lax
from jax.experimental import pallas as pl
from jax.experimental.pallas import tpu as pltpu

F32 = jnp.float32
BF16 = jnp.bfloat16

HEAD_DIM = 128
KV_HEADS = 4
CONV_TAPS = 3
N_BRANCH = 3
CMP_BLOCK = 32
CMP_STRIDE = 16
SEL_BLOCK = 64
N_SELECT = 16
WINDOW = 512
FORCE_SCORE = 1e6
NEG_INF = -1e30
REL_BUCKETS = 32
REL_MAX_DIST = 128
POOL_WINDOWS = (2, 4, 8, 16)
N_EXPERTS = 8
EPS = 1e-6

LANES = 128
POOL_HALO = 16
CONV_HALO = 8
VMEM_CAP = 60 * 1024 * 1024
NT_DIMS = (((1,), (1,)), ((), ()))


def _pick(n, pref, mult=LANES):
    t = min(pref, n)
    while n % t or t % mult:
        t -= mult
    return t


def _params(sem, est_bytes):
    limit = int(min(max(est_bytes * 5 // 4 + (4 << 20), 16 << 20), VMEM_CAP))
    return pltpu.CompilerParams(dimension_semantics=sem, vmem_limit_bytes=limit)


def _rms(x, g):
    return x * lax.rsqrt(jnp.mean(x * x, axis=-1, keepdims=True) + EPS) * g


def _rmsnorm_kernel(x_ref, g_ref, o_ref):
    o_ref[...] = _rms(x_ref[...], g_ref[...]).astype(o_ref.dtype)


def rmsnorm(x, g, out_dtype, tm=256):
    m, d = x.shape
    tm = _pick(m, tm, 8)
    return pl.pallas_call(
        _rmsnorm_kernel,
        grid=(m // tm,),
        in_specs=[pl.BlockSpec((tm, d), lambda i: (i, 0)), pl.BlockSpec((1, d), lambda i: (0, 0))],
        out_specs=pl.BlockSpec((tm, d), lambda i: (i, 0)),
        out_shape=jax.ShapeDtypeStruct((m, d), out_dtype),
        compiler_params=_params(("parallel",), 2 * tm * d * 8),
        name="rmsnorm",
    )(x, g.reshape(1, d))


def _mm_kernel(*refs, n_pairs, has_res):
    o_ref = refs[-1]
    acc = None
    for p in range(n_pairs):
        d = jnp.dot(refs[2 * p][...], refs[2 * p + 1][...], preferred_element_type=F32)
        acc = d if acc is None else acc + d
    if has_res:
        acc = refs[2 * n_pairs][...] + acc
    o_ref[...] = acc.astype(o_ref.dtype)


def matmul(pairs, res, out_dtype, tm, tn, name, n=None):
    m = pairs[0][0].shape[0]
    n = pairs[0][1].shape[1] if n is None else n
    tm = _pick(m, tm)
    tn = _pick(n, tn)
    in_specs, args, est = [], [], 0
    for x, w, r in pairs:
        k = x.shape[1]
        in_specs += [pl.BlockSpec((tm, k), lambda i, j: (i, 0)), pl.BlockSpec((k, tn), lambda i, j, r=r: (r, j))]
        args += [x, w]
        est += 2 * 2 * (tm * k + k * tn)
    if res is not None:
        in_specs.append(pl.BlockSpec((tm, tn), lambda i, j: (i, j)))
        args.append(res)
        est += 2 * 4 * tm * tn
    est += 2 * 4 * tm * tn + 4 * tm * tn
    return pl.pallas_call(
        functools.partial(_mm_kernel, n_pairs=len(pairs), has_res=res is not None),
        grid=(m // tm, n // tn),
        in_specs=in_specs,
        out_specs=pl.BlockSpec((tm, tn), lambda i, j: (i, j)),
        out_shape=jax.ShapeDtypeStruct((m, n), out_dtype),
        compiler_params=_params(("parallel", "parallel"), est),
        name=name,
    )(*args)


def _glu_kernel(h_ref, w1_ref, w3_ref, o_ref):
    h = h_ref[...]
    a = jnp.dot(h, w1_ref[...], preferred_element_type=F32)
    b = jnp.dot(h, w3_ref[...], preferred_element_type=F32)
    o_ref[...] = (a * jax.nn.sigmoid(a) * b).astype(o_ref.dtype)


def glu_matmul(h, w1, w3, tm, tn):
    m, k = h.shape
    n = w1.shape[1]
    tm = _pick(m, tm)
    tn = _pick(n, tn)
    est = 2 * 2 * (tm * k + 2 * k * tn + tm * tn) + 3 * 4 * tm * tn
    return pl.pallas_call(
        _glu_kernel,
        grid=(m // tm, n // tn),
        in_specs=[pl.BlockSpec((tm, k), lambda i, j: (i, 0)),
                  pl.BlockSpec((k, tn), lambda i, j: (0, j)),
                  pl.BlockSpec((k, tn), lambda i, j: (0, j))],
        out_specs=pl.BlockSpec((tm, tn), lambda i, j: (i, j)),
        out_shape=jax.ShapeDtypeStruct((m, n), BF16),
        compiler_params=_params(("parallel", "parallel"), est),
        name="dense_glu",
    )(h, w1, w3)


def _mm_acc_kernel(x_ref, w_ref, r_ref, o_ref, acc_ref):
    k = pl.program_id(2)

    @pl.when(k == 0)
    def _():
        acc_ref[...] = jnp.zeros_like(acc_ref)

    acc_ref[...] += jnp.dot(x_ref[...], w_ref[...], preferred_element_type=F32)

    @pl.when(k == pl.num_programs(2) - 1)
    def _():
        o_ref[...] = r_ref[...] + acc_ref[...]


def matmul_acc(x, w, res, tm, tn, tk, name):
    m, k = x.shape
    n = w.shape[1]
    tm, tn, tk = _pick(m, tm), _pick(n, tn), _pick(k, tk)
    est = 2 * 2 * (tm * tk + tk * tn) + 5 * 4 * tm * tn
    return pl.pallas_call(
        _mm_acc_kernel,
        grid=(m // tm, n // tn, k // tk),
        in_specs=[pl.BlockSpec((tm, tk), lambda i, j, kk: (i, kk)),
                  pl.BlockSpec((tk, tn), lambda i, j, kk: (kk, j)),
                  pl.BlockSpec((tm, tn), lambda i, j, kk: (i, j))],
        out_specs=pl.BlockSpec((tm, tn), lambda i, j, kk: (i, j)),
        out_shape=jax.ShapeDtypeStruct((m, n), F32),
        scratch_shapes=[pltpu.VMEM((tm, tn), F32)],
        compiler_params=_params(("parallel", "parallel", "arbitrary"), est),
        name=name,
    )(x, w, res)


def _conv_kernel(ab_ref, ac_ref, au_ref, hc_ref, hu_ref, w_ref, o_ref, *, tq, seq):
    i = pl.program_id(0)
    first = (i * tq) % seq == 0
    v = ac_ref[...] * au_ref[...]
    hv = jnp.where(first, 0.0, hc_ref[...] * hu_ref[...])
    rows = lax.broadcasted_iota(jnp.int32, v.shape, 0)
    v1 = jnp.where(rows == 0, hv[CONV_HALO - 1:CONV_HALO], pltpu.roll(v, 1, 0))
    v2 = jnp.where(rows == 0, hv[CONV_HALO - 2:CONV_HALO - 1],
                   jnp.where(rows == 1, hv[CONV_HALO - 1:CONV_HALO], pltpu.roll(v, 2, 0)))
    y = w_ref[0:1, :] * v2 + w_ref[1:2, :] * v1 + w_ref[2:3, :] * v
    o_ref[...] = (ab_ref[...] * y).astype(o_ref.dtype)


def short_conv(proj, conv_w_t, ch, seq, tq=512, tc=1024):
    m = proj.shape[0]
    tq = _pick(seq, tq, CONV_HALO)
    tc = _pick(ch, tc)
    nj = ch // tc
    hb = tq // CONV_HALO
    halo = lambda off: pl.BlockSpec((CONV_HALO, tc), lambda i, j: (jnp.maximum(i * hb - 1, 0), off + j))
    return pl.pallas_call(
        functools.partial(_conv_kernel, tq=tq, seq=seq),
        grid=(m // tq, nj),
        in_specs=[pl.BlockSpec((tq, tc), lambda i, j: (i, j)),
                  pl.BlockSpec((tq, tc), lambda i, j: (i, nj + j)),
                  pl.BlockSpec((tq, tc), lambda i, j: (i, 2 * nj + j)),
                  halo(nj), halo(2 * nj),
                  pl.BlockSpec((CONV_TAPS, tc), lambda i, j: (0, j))],
        out_specs=pl.BlockSpec((tq, tc), lambda i, j: (i, j)),
        out_shape=jax.ShapeDtypeStruct((m, ch), BF16),
        compiler_params=_params(("parallel", "parallel"), 2 * 4 * 4 * tq * tc + 6 * 4 * tq * tc),
        name="short_conv",
    )(proj, proj, proj, proj, proj, conv_w_t)


def _rel_bucket_np(dist):
    dist = np.maximum(dist, 0)
    exact = REL_BUCKETS // 2
    d = np.maximum(dist, exact).astype(np.float32)
    large = exact + (np.log(d / np.float32(exact)) / np.float32(math.log(REL_MAX_DIST / exact))
                     * np.float32(REL_BUCKETS - exact)).astype(np.int32)
    return np.where(dist < exact, dist, np.minimum(large, REL_BUCKETS - 1)).astype(np.int32)


WIN_TILES = WINDOW // LANES
N_BIAS_TILES = 2 * WIN_TILES + 1


def _bias_index_tables(seq):
    r = np.arange(LANES)[:, None]
    c = np.arange(LANES)[None, :]
    far = np.full((LANES, LANES), REL_BUCKETS - 1, np.int32)
    zero = np.zeros((LANES, LANES), np.float32)
    neg = np.full((LANES, LANES), NEG_INF, np.float32)
    assert _rel_bucket_np(np.arange(LANES + 1, 4 * seq)).min() == REL_BUCKETS - 1
    idx, add = [], []
    for d in range(-WIN_TILES, WIN_TILES + 1):
        if d < 0:
            idx.append(far), add.append(neg)
        elif d == 0:
            idx.append(_rel_bucket_np(r - c)), add.append(np.where(r >= c, zero, neg))
        elif d == 1:
            idx.append(_rel_bucket_np(LANES + r - c)), add.append(zero)
        elif d < WIN_TILES:
            idx.append(far), add.append(zero)
        else:
            idx.append(far), add.append(np.where(c > r, zero, neg))
    n_tile_rows = len(idx) * LANES
    t = np.arange(seq)[:, None]
    idx.append(_rel_bucket_np(t - (c * CMP_STRIDE + CMP_BLOCK - 1)))
    add.append(np.zeros((seq, LANES), np.float32))
    idx = np.concatenate(idx, axis=0).astype(np.int32)
    add = np.concatenate(add, axis=0).astype(np.float32)
    shift = (np.arange(idx.shape[0])[:, None] < n_tile_rows).astype(np.float32) * np.ones((1, LANES), np.float32)
    return idx, shift, add


def _bias_kernel(tbl_ref, idx_ref, shift_ref, add_ref, o_ref):
    h = pl.program_id(0)
    idx = idx_ref[...]
    acc = jnp.zeros(idx.shape, F32)
    for b in range(REL_BUCKETS):
        acc = jnp.where(idx == b, tbl_ref[b, h], acc)
    o_ref[0] = acc - shift_ref[...] * tbl_ref[REL_BUCKETS - 1, h] + add_ref[...]


def bias_tables(rel_bias, seq):
    heads = rel_bias.shape[1]
    idx, shift, add = (jnp.asarray(a) for a in _bias_index_tables(seq))
    rows = idx.shape[0]
    full = pl.BlockSpec((rows, LANES), lambda h: (0, 0))
    return pl.pallas_call(
        _bias_kernel,
        grid=(heads,),
        in_specs=[pl.BlockSpec(memory_space=pltpu.SMEM), full, full, full],
        out_specs=pl.BlockSpec((1, rows, LANES), lambda h: (h, 0, 0)),
        out_shape=jax.ShapeDtypeStruct((heads, rows, LANES), F32),
        compiler_params=_params(("parallel",), 12 * 4 * rows * LANES),
        name="rel_bias_tables",
    )(rel_bias, idx, shift, add)


def _compress_kernel(k_ref, v_ref, pek_ref, pev_ref, wk_ref, wv_ref, kn_ref, kc_ref, vc_ref, *, nch):
    half = CMP_BLOCK // CMP_STRIDE
    assert half == 2

    def comp(x_ref, pe_ref, w_ref):
        lo = jnp.zeros((nch, HEAD_DIM), F32)
        hi = jnp.zeros((nch, HEAD_DIM), F32)
        for l in range(CMP_STRIDE):
            rows = x_ref[pl.ds(l, nch, stride=CMP_STRIDE), :]
            lo += jnp.dot((rows + pe_ref[l:l + 1, :]).astype(BF16), w_ref[l], preferred_element_type=F32)
            hi += jnp.dot((rows + pe_ref[CMP_STRIDE + l:CMP_STRIDE + l + 1, :]).astype(BF16),
                          w_ref[CMP_STRIDE + l], preferred_element_type=F32)
        return lo + pltpu.roll(hi, nch - 1, 0)

    kc_ref[0, 0] = _rms(comp(k_ref, pek_ref, wk_ref), kn_ref[...])
    vc_ref[0, 0] = comp(v_ref, pev_ref, wv_ref)


def compress(proj, kc_blk, vc_blk, pe_k, pe_v, w_k, w_v, k_norm, batch, seq):
    nch = seq // CMP_STRIDE
    wspec = pl.BlockSpec((CMP_BLOCK, HEAD_DIM, HEAD_DIM), lambda b, g: (0, 0, 0))
    pespec = pl.BlockSpec((CMP_BLOCK, HEAD_DIM), lambda b, g: (0, 0))
    ospec = pl.BlockSpec((1, 1, nch, HEAD_DIM), lambda b, g: (b, g, 0, 0))
    oshape = jax.ShapeDtypeStruct((batch, KV_HEADS, nch, HEAD_DIM), F32)
    return pl.pallas_call(
        functools.partial(_compress_kernel, nch=nch),
        grid=(batch, KV_HEADS),
        in_specs=[pl.BlockSpec((seq, HEAD_DIM), lambda b, g: (b, kc_blk + g)),
                  pl.BlockSpec((seq, HEAD_DIM), lambda b, g: (b, vc_blk + g)),
                  pespec, pespec, wspec, wspec,
                  pl.BlockSpec((1, HEAD_DIM), lambda b, g: (0, 0))],
        out_specs=[ospec, ospec],
        out_shape=[oshape, oshape],
        compiler_params=_params(("parallel", "parallel"), 4 * 4 * seq * HEAD_DIM + (4 << 20)),
        name="nsa_compress",
    )(proj, proj, pe_k, pe_v, w_k, w_v, k_norm)


SEL_CHUNK = 512


def _nsa_kernel(q_ref, ks_ref, vs_ref, kw_ref, vw_ref, kc_ref, vc_ref, wt_ref, bc_ref, gt_ref,
                qn_ref, kn_ref, cov_ref, oh_ref, o_ref, ksa, vsb, kwn, vwb, *, hpg, n_cmp, n_top):
    qi = pl.program_id(2)
    rows_all = hpg * LANES
    scale = HEAD_DIM ** -0.5
    n_sel_blocks = cov_ref.shape[0]
    chunk_tiles = SEL_CHUNK // LANES

    @pl.when(qi == 0)
    def _():
        ksa[:, :HEAD_DIM] = _rms(ks_ref[...], kn_ref[...]).astype(BF16)
        ksa[:, HEAD_DIM:] = oh_ref[...]
        kwn[...] = _rms(kw_ref[...], kn_ref[...]).astype(BF16)
        vsb[...] = vs_ref[...].astype(BF16)
        vwb[...] = vw_ref[...].astype(BF16)

    qs = [_rms(q_ref[:, h * HEAD_DIM:(h + 1) * HEAD_DIM], qn_ref[...]).astype(BF16) for h in range(hpg)]
    q = jnp.concatenate(qs, axis=0) if hpg > 1 else qs[0]

    def bias_tile(d):
        off = pl.multiple_of((d + WIN_TILES) * LANES, LANES)
        return wt_ref[:, pl.ds(off, LANES), :].reshape(rows_all, LANES)

    rowq = lax.broadcasted_iota(jnp.int32, (rows_all, LANES), 0) & (LANES - 1)
    col = lax.broadcasted_iota(jnp.int32, (rows_all, LANES), 1)

    lc = lax.dot_general(q, kc_ref[0, 0].astype(BF16), NT_DIMS, preferred_element_type=F32) * scale
    lc = lc + bc_ref[...].reshape(rows_all, LANES)
    mc = ((qi * LANES + rowq - (col * CMP_STRIDE + CMP_BLOCK - 1)) >= 0) & (col < n_cmp)
    zc = jnp.where(mc, lc, NEG_INF)
    ec = jnp.exp(zc - jnp.max(zc, axis=-1, keepdims=True))
    pc = ec / jnp.sum(ec, axis=-1, keepdims=True) * mc.astype(F32)
    o_cmp = jnp.dot(pc.astype(BF16), vc_ref[0, 0].astype(BF16), preferred_element_type=F32)

    ps = pc[0:LANES]
    for h in range(1, hpg):
        ps = ps + pc[h * LANES:(h + 1) * LANES]
    p1 = ps.astype(BF16)
    r1 = ps - p1.astype(F32)
    p2 = r1.astype(BF16)
    p3 = (r1 - p2.astype(F32)).astype(BF16)
    cov = cov_ref[...]
    score = (lax.dot_general(cov, p1, NT_DIMS, preferred_element_type=F32)
             + lax.dot_general(cov, p2, NT_DIMS, preferred_element_type=F32)
             + lax.dot_general(cov, p3, NT_DIMS, preferred_element_type=F32))
    jj = lax.broadcasted_iota(jnp.int32, (n_sel_blocks, LANES), 0)
    ql = lax.broadcasted_iota(jnp.int32, (n_sel_blocks, LANES), 1)
    cur = (LANES // SEL_BLOCK) * qi + ql // SEL_BLOCK
    forced = (jj == 0) | (jj == cur) | (jj == cur - 1)
    score = jnp.where(forced, FORCE_SCORE, jnp.where(jj > cur, -FORCE_SCORE, score))
    rank = jnp.zeros((n_sel_blocks, LANES), F32)
    for j2 in range(n_sel_blocks):
        other = score[j2:j2 + 1, :]
        rank += ((other > score) | ((other == score) & (j2 < jj))).astype(F32)
    sel_t = ((rank < n_top) & (jj <= cur)).astype(BF16)
    sel_t = jnp.concatenate([sel_t, jnp.ones((LANES - n_sel_blocks, LANES), BF16)], axis=0)
    eye = (lax.broadcasted_iota(jnp.int32, (LANES, LANES), 0)
           == lax.broadcasted_iota(jnp.int32, (LANES, LANES), 1)).astype(BF16)
    selq = lax.dot_general(eye, sel_t, NT_DIMS, preferred_element_type=F32)
    sel_neg = ((selq - 1.0) * (-NEG_INF)).astype(BF16)
    sel_neg = jnp.concatenate([sel_neg] * hpg, axis=0) if hpg > 1 else sel_neg
    qa = jnp.concatenate([q, sel_neg], axis=1)

    def sel_chunk(c, carry, near):
        m, l, acc = carry
        off = pl.multiple_of(c * SEL_CHUNK, SEL_CHUNK)
        s = lax.dot_general(qa, ksa[pl.ds(off, SEL_CHUNK), :], NT_DIMS, preferred_element_type=F32) * scale
        if near:
            s = jnp.concatenate([s[:, u * LANES:(u + 1) * LANES]
                                 + bias_tile(jnp.clip(qi - (c * chunk_tiles + u), 0, 2))
                                 for u in range(chunk_tiles)], axis=1)
        m_new = jnp.maximum(m, jnp.max(s, axis=-1, keepdims=True))
        alpha = jnp.exp(m - m_new)
        p = jnp.exp(s - m_new)
        l = alpha * l + jnp.sum(p, axis=-1, keepdims=True)
        acc = alpha * acc + jnp.dot(p.astype(BF16), vsb[pl.ds(off, SEL_CHUNK), :], preferred_element_type=F32)
        return m_new, l, acc

    init = (jnp.full((rows_all, 1), NEG_INF, F32), jnp.zeros((rows_all, 1), F32),
            jnp.zeros((rows_all, HEAD_DIM), F32))
    n_far = (jnp.maximum(qi, 1) - 1) // chunk_tiles
    carry = lax.fori_loop(0, n_far, functools.partial(sel_chunk, near=False), init)
    _, l_s, a_s = lax.fori_loop(n_far, qi // chunk_tiles + 1, functools.partial(sel_chunk, near=True), carry)
    o_sel = a_s / l_s

    win_keys = WINDOW + LANES
    kt0 = jnp.maximum(qi - WIN_TILES, 0)
    woff = pl.multiple_of(kt0 * LANES, LANES)
    sw = lax.dot_general(q, kwn[pl.ds(woff, win_keys), :], NT_DIMS, preferred_element_type=F32) * scale
    zw = jnp.concatenate([sw[:, u * LANES:(u + 1) * LANES] + bias_tile(qi - (kt0 + u))
                          for u in range(win_keys // LANES)], axis=1)
    pw = jnp.exp(zw - jnp.max(zw, axis=-1, keepdims=True))
    o_win = (jnp.dot(pw.astype(BF16), vwb[pl.ds(woff, win_keys), :], preferred_element_type=F32)
             / jnp.sum(pw, axis=-1, keepdims=True))

    gt = jax.nn.sigmoid(gt_ref[...])
    for h in range(hpg):
        sl = slice(h * LANES, (h + 1) * LANES)
        c0 = h * N_BRANCH
        o = (gt[:, c0:c0 + 1] * o_cmp[sl] + gt[:, c0 + 1:c0 + 2] * o_sel[sl]
             + gt[:, c0 + 2:c0 + 3] * o_win[sl])
        o_ref[:, h * HEAD_DIM:(h + 1) * HEAD_DIM] = o.astype(o_ref.dtype)


def _cover_t(seq):
    nc = seq // CMP_STRIDE - CMP_BLOCK // CMP_STRIDE + 1
    ns = seq // SEL_BLOCK
    c_start = np.arange(nc) * CMP_STRIDE
    c_end = c_start + CMP_BLOCK - 1
    s_start = np.arange(ns) * SEL_BLOCK
    cover = (c_start[:, None] <= s_start[None, :] + SEL_BLOCK - 1) & (c_end[:, None] >= s_start[None, :])
    out = np.zeros((ns, seq // CMP_STRIDE), np.float32)
    out[:, :nc] = cover.T
    return out


def nsa_attention(proj, gates, kc, vc, tables, q_norm, k_norm, batch, seq, hpg, blk):
    heads = KV_HEADS * hpg
    nq = seq // LANES
    nch = seq // CMP_STRIDE
    assert nch == LANES, "compressed keys must fill exactly one lane tile"
    n_cmp = nch - CMP_BLOCK // CMP_STRIDE + 1
    ns = seq // SEL_BLOCK
    assert seq % SEL_CHUNK == 0 and ns <= LANES
    cov = jnp.asarray(_cover_t(seq), BF16)
    onehot = jnp.asarray(np.arange(seq)[:, None] // SEL_BLOCK == np.arange(LANES)[None, :], BF16)
    qw = hpg * HEAD_DIM
    q_blk = blk["q"] * HEAD_DIM // qw
    kv = lambda name: pl.BlockSpec((seq, HEAD_DIM), lambda b, g, i: (b, blk[name] + g))
    cspec = pl.BlockSpec((1, 1, nch, HEAD_DIM), lambda b, g, i: (b, g, 0, 0))
    vec = pl.BlockSpec((1, HEAD_DIM), lambda b, g, i: (0, 0))
    rows_all = hpg * LANES
    est = (2 * 4 * 4 * seq * HEAD_DIM + 5 * 2 * seq * HEAD_DIM + 2 * 2 * seq * LANES
           + 2 * 4 * hpg * (N_BIAS_TILES + 1) * LANES * LANES + 6 * 4 * rows_all * (WINDOW + LANES) + (8 << 20))
    return pl.pallas_call(
        functools.partial(_nsa_kernel, hpg=hpg, n_cmp=n_cmp, n_top=min(N_SELECT, ns)),
        grid=(batch, KV_HEADS, nq),
        in_specs=[pl.BlockSpec((LANES, qw), lambda b, g, i: (b * nq + i, q_blk + g)),
                  kv("ks"), kv("vs"), kv("kw"), kv("vw"), cspec, cspec,
                  pl.BlockSpec((hpg, N_BIAS_TILES * LANES, LANES), lambda b, g, i: (g, 0, 0)),
                  pl.BlockSpec((hpg, LANES, LANES), lambda b, g, i: (g, N_BIAS_TILES + i, 0)),
                  pl.BlockSpec((LANES, LANES), lambda b, g, i: (b * nq + i, g)),
                  vec, vec,
                  pl.BlockSpec((ns, nch), lambda b, g, i: (0, 0)),
                  pl.BlockSpec((seq, LANES), lambda b, g, i: (0, 0))],
        out_specs=pl.BlockSpec((LANES, qw), lambda b, g, i: (b * nq + i, g)),
        out_shape=jax.ShapeDtypeStruct((batch * seq, heads * HEAD_DIM), BF16),
        scratch_shapes=[pltpu.VMEM((seq, 2 * HEAD_DIM), BF16)] + [pltpu.VMEM((seq, HEAD_DIM), BF16)] * 3,
        compiler_params=_params(("parallel", "parallel", "arbitrary"), est),
        name="nsa_attention",
    )(proj, proj, proj, proj, proj, kc, vc, tables, tables, gates, q_norm, k_norm, cov, onehot)


def _pool_kernel(x_ref, halo_ref, gm_ref, wp_ref, ps_ref, gf_ref, wr_ref, x_out, route_out, *, tq, seq, cg):
    i = pl.program_id(0)
    start = (i * tq) % seq
    x = x_ref[...]
    h = _rms(x, gm_ref[...])
    hh = jnp.where(start == 0, 0.0, _rms(halo_ref[...], gm_ref[...]))
    t1 = (start + lax.broadcasted_iota(jnp.int32, (tq, 1), 0) + 1).astype(F32)
    ys = []
    for gi, w in enumerate(POOL_WINDOWS):
        sl = slice(gi * cg, (gi + 1) * cg)
        s = jnp.concatenate([hh[:, sl], h[:, sl]], axis=0)
        span = 1
        while span < w:
            s = s + pltpu.roll(s, span, 0)
            span *= 2
        dm = s[POOL_HALO:] / jnp.minimum(t1, float(w)) - h[:, sl]
        ys.append(jnp.dot(dm.astype(BF16), wp_ref[gi], preferred_element_type=F32))
    x3 = x + jnp.concatenate(ys, axis=1) * ps_ref[...]
    x_out[...] = x3

    h4 = _rms(x3, gf_ref[...])
    lane = lax.broadcasted_iota(jnp.int32, (tq, LANES), 1)
    logits = jnp.full((tq, LANES), -jnp.inf, F32)
    for e in range(N_EXPERTS):
        logits = jnp.where(lane == e, jnp.sum(h4 * wr_ref[e:e + 1, :], axis=-1, keepdims=True), logits)
    m1 = jnp.max(logits, axis=-1, keepdims=True)
    i1 = jnp.min(jnp.where(logits == m1, lane, LANES), axis=-1, keepdims=True)
    rest = jnp.where(lane == i1, -jnp.inf, logits)
    m2 = jnp.max(rest, axis=-1, keepdims=True)
    i2 = jnp.min(jnp.where(rest == m2, lane, LANES), axis=-1, keepdims=True)
    e2 = jnp.exp(m2 - m1)
    den = 1.0 + e2
    route = jnp.where(lane == 0, i1.astype(F32),
                      jnp.where(lane == 1, i2.astype(F32),
                                jnp.where(lane == 2, 1.0 / den, jnp.where(lane == 3, e2 / den, 0.0))))
    route_out[...] = route


def pool_and_route(x, g_mix, w_pool, pool_scale, g_ffn, w_router_t, seq, tq=256):
    m, d = x.shape
    cg = d // len(POOL_WINDOWS)
    tq = _pick(seq, tq, POOL_HALO)
    hb = tq // POOL_HALO
    vec = pl.BlockSpec((1, d), lambda i: (0, 0))
    est = 2 * 2 * 4 * tq * d + 2 * 2 * len(POOL_WINDOWS) * cg * cg + 8 * 4 * tq * d
    return pl.pallas_call(
        functools.partial(_pool_kernel, tq=tq, seq=seq, cg=cg),
        grid=(m // tq,),
        in_specs=[pl.BlockSpec((tq, d), lambda i: (i, 0)),
                  pl.BlockSpec((POOL_HALO, d), lambda i: (jnp.maximum(i * hb - 1, 0), 0)),
                  vec,
                  pl.BlockSpec((len(POOL_WINDOWS), cg, cg), lambda i: (0, 0, 0)),
                  vec, vec,
                  pl.BlockSpec((N_EXPERTS, d), lambda i: (0, 0))],
        out_specs=[pl.BlockSpec((tq, d), lambda i: (i, 0)), pl.BlockSpec((tq, LANES), lambda i: (i, 0))],
        out_shape=[jax.ShapeDtypeStruct((m, d), F32), jax.ShapeDtypeStruct((m, LANES), F32)],
        compiler_params=_params(("parallel",), est),
        name="pool_mixer_router",
    )(x, x, g_mix, w_pool, pool_scale, g_ffn, w_router_t)


def _route_tables(route, tm, n_tiles):
    n = route.shape[0]
    i1 = route[:, 0].astype(jnp.int32)
    i2 = route[:, 1].astype(jnp.int32)
    onehot = jax.nn.one_hot(i1, N_EXPERTS, dtype=jnp.int32) + jax.nn.one_hot(i2, N_EXPERTS, dtype=jnp.int32)
    count = jnp.sum(onehot, axis=0)
    padded = (count + tm - 1) // tm * tm
    end = jnp.cumsum(padded)
    pos = (end - padded)[None, :] + jnp.cumsum(onehot, axis=0) - onehot
    p1 = jnp.take_along_axis(pos, i1[:, None], axis=1)[:, 0]
    p2 = jnp.take_along_axis(pos, i2[:, None], axis=1)[:, 0]
    tok = jnp.arange(n, dtype=jnp.int32)
    rows = n_tiles * tm
    row_token = jnp.zeros((rows,), jnp.int32).at[jnp.concatenate([p1, p2])].set(jnp.concatenate([tok, tok]))
    n_used = (end[-1] // tm).astype(jnp.int32)
    tile = jnp.arange(n_tiles, dtype=jnp.int32)
    tile_expert = jnp.searchsorted(end, tile * tm, side="right").astype(jnp.int32)
    tile_expert = jnp.where(tile < n_used, tile_expert, tile_expert[n_used - 1])
    tile_rows = jnp.clip((count - padded + end)[tile_expert] - tile * tm, 0, tm)
    tile_rows = jnp.where(tile < n_used, tile_rows, 0).astype(jnp.int32)
    return tile_expert, tile_rows, row_token, n_used.reshape(1), p1.astype(jnp.int32), p2.astype(jnp.int32)


def _gather_start(idx_ref, base, src_hbm, dst_ref, sem, count):
    def issue(r, carry):
        pltpu.make_async_copy(src_hbm.at[pl.ds(idx_ref[base + r], 1), :], dst_ref.at[pl.ds(r, 1), :], sem).start()
        return carry

    lax.fori_loop(0, count, issue, 0)


def _gather_wait(src_hbm, dst_ref, sem, count):
    pltpu.make_async_copy(src_hbm.at[pl.ds(0, count), :], dst_ref, sem).wait()


MOE_SUB = 256


def _moe_gather_kernel(rt_ref, nu_ref, x_hbm, g_ref, o_ref, xg_ref, sem, *, tm):
    i = pl.program_id(0)
    valid = i < nu_ref[0]
    slot = i % 2

    @pl.when(i == 0)
    def _():
        _gather_start(rt_ref, 0, x_hbm, xg_ref.at[0], sem.at[0], tm)

    @pl.when(i + 1 < nu_ref[0])
    def _():
        _gather_start(rt_ref, (i + 1) * tm, x_hbm, xg_ref.at[1 - slot], sem.at[1 - slot], tm)

    @pl.when(valid)
    def _():
        _gather_wait(x_hbm, xg_ref.at[slot], sem.at[slot], tm)
        o_ref[...] = _rms(xg_ref[slot], g_ref[...]).astype(o_ref.dtype)

    @pl.when(jnp.logical_not(valid))
    def _():
        o_ref[...] = jnp.zeros_like(o_ref)


def moe_gather(x, g_ffn, row_token, n_used, tm):
    d = x.shape[1]
    n_tiles = row_token.shape[0] // tm
    return pl.pallas_call(
        functools.partial(_moe_gather_kernel, tm=tm),
        grid_spec=pltpu.PrefetchScalarGridSpec(
            num_scalar_prefetch=2,
            grid=(n_tiles,),
            in_specs=[pl.BlockSpec(memory_space=pl.ANY), pl.BlockSpec((1, d), lambda i, rt, nu: (0, 0))],
            out_specs=pl.BlockSpec((tm, d), lambda i, rt, nu: (i, 0)),
            scratch_shapes=[pltpu.VMEM((2, tm, d), F32), pltpu.SemaphoreType.DMA((2,))]),
        out_shape=jax.ShapeDtypeStruct((n_tiles * tm, d), BF16),
        compiler_params=_params(("arbitrary",), 2 * 4 * tm * d + 2 * 2 * tm * d + 2 * 4 * tm * d),
        name="moe_gather",
    )(row_token, n_used, x, g_ffn)


def _moe_glu_kernel(te_ref, tr_ref, nu_ref, h_ref, w1_ref, w3_ref, o_ref, w1b, w3b, *, tm):
    i = pl.program_id(1)
    valid = i < nu_ref[0]

    @pl.when(valid & ((i == 0) | (te_ref[i] != te_ref[jnp.maximum(i - 1, 0)])))
    def _():
        w1b[...] = w1_ref[0].astype(BF16)
        w3b[...] = w3_ref[0].astype(BF16)

    for sb in range(tm // MOE_SUB):
        rows = pl.ds(sb * MOE_SUB, MOE_SUB)
        live = valid & (sb * MOE_SUB < tr_ref[i])

        @pl.when(live)
        def _():
            h = h_ref[rows, :]
            a = jnp.dot(h, w1b[...], preferred_element_type=F32)
            b = jnp.dot(h, w3b[...], preferred_element_type=F32)
            o_ref[rows, :] = (a * jax.nn.sigmoid(a) * b).astype(o_ref.dtype)

        @pl.when(jnp.logical_not(live))
        def _():
            o_ref[rows, :] = jnp.zeros((MOE_SUB, o_ref.shape[1]), o_ref.dtype)


def moe_glu(h, w1, w3, tile_expert, tile_rows, n_used, tm, tf):
    d = h.shape[1]
    ff = w1.shape[2]
    tf = _pick(ff, tf)
    n_tiles = tile_expert.shape[0]
    wspec = pl.BlockSpec((1, d, tf), lambda f, i, te, tr, nu: (te[i], 0, f))
    est = 2 * 2 * 4 * d * tf + 2 * 2 * d * tf + 2 * 2 * tm * d + 2 * 2 * tm * tf + 4 * 4 * MOE_SUB * tf
    return pl.pallas_call(
        functools.partial(_moe_glu_kernel, tm=tm),
        grid_spec=pltpu.PrefetchScalarGridSpec(
            num_scalar_prefetch=3,
            grid=(ff // tf, n_tiles),
            in_specs=[pl.BlockSpec((tm, d), lambda f, i, te, tr, nu: (jnp.minimum(i, nu[0] - 1), 0)),
                      wspec, wspec],
            out_specs=pl.BlockSpec((tm, tf), lambda f, i, te, tr, nu: (i, f)),
            scratch_shapes=[pltpu.VMEM((d, tf), BF16), pltpu.VMEM((d, tf), BF16)]),
        out_shape=jax.ShapeDtypeStruct((n_tiles * tm, ff), BF16),
        compiler_params=_params(("arbitrary", "arbitrary"), est),
        name="moe_glu",
    )(tile_expert, tile_rows, n_used, h, w1, w3)


def _moe_down_kernel(te_ref, tr_ref, nu_ref, a_ref, w_ref, o_ref, *, tm):
    i = pl.program_id(0)
    valid = i < nu_ref[0]
    for sb in range(tm // MOE_SUB):
        rows = pl.ds(sb * MOE_SUB, MOE_SUB)
        live = valid & (sb * MOE_SUB < tr_ref[i])

        @pl.when(live)
        def _():
            o_ref[rows, :] = jnp.dot(a_ref[rows, :], w_ref[0], preferred_element_type=F32)

        @pl.when(jnp.logical_not(live))
        def _():
            o_ref[rows, :] = jnp.zeros((MOE_SUB, o_ref.shape[1]), o_ref.dtype)


def moe_down(act, w2, tile_expert, tile_rows, n_used, tm, tn):
    ff = act.shape[1]
    d = w2.shape[2]
    tn = _pick(d, tn)
    nn = d // tn
    n_tiles = tile_expert.shape[0]
    est = 2 * 2 * (tm * ff + ff * tn) + 2 * 4 * tm * tn + 4 * MOE_SUB * tn
    return pl.pallas_call(
        functools.partial(_moe_down_kernel, tm=tm),
        grid_spec=pltpu.PrefetchScalarGridSpec(
            num_scalar_prefetch=3,
            grid=(n_tiles, nn),
            in_specs=[pl.BlockSpec((tm, ff), lambda i, j, te, tr, nu: (jnp.minimum(i, nu[0] - 1), 0)),
                      pl.BlockSpec((1, ff, tn),
                                   lambda i, j, te, tr, nu: (te[i], 0, jnp.where(i < nu[0], j, nn - 1)))],
            out_specs=pl.BlockSpec((tm, tn), lambda i, j, te, tr, nu: (i, j))),
        out_shape=jax.ShapeDtypeStruct((n_tiles * tm, d), F32),
        compiler_params=_params(("arbitrary", "arbitrary"), est),
        name="moe_down",
    )(tile_expert, tile_rows, n_used, act, w2)


def _moe_combine_kernel(p1_ref, p2_ref, x_ref, r_ref, y_hbm, o_ref, ya_ref, yb_ref, sem_a, sem_b, *, tq):
    i = pl.program_id(0)
    slot = i % 2

    def start(step, s):
        _gather_start(p1_ref, step * tq, y_hbm, ya_ref.at[s], sem_a.at[s], tq)
        _gather_start(p2_ref, step * tq, y_hbm, yb_ref.at[s], sem_b.at[s], tq)

    @pl.when(i == 0)
    def _():
        start(0, 0)

    @pl.when(i + 1 < pl.num_programs(0))
    def _():
        start(i + 1, 1 - slot)

    _gather_wait(y_hbm, ya_ref.at[slot], sem_a.at[slot], tq)
    _gather_wait(y_hbm, yb_ref.at[slot], sem_b.at[slot], tq)
    r = r_ref[...]
    o_ref[...] = x_ref[...] + (r[:, 2:3] * ya_ref[slot] + r[:, 3:4] * yb_ref[slot])


def moe_combine(x, route, y, p1, p2, tq=256):
    m, d = x.shape
    tq = _pick(m, tq, 8)
    return pl.pallas_call(
        functools.partial(_moe_combine_kernel, tq=tq),
        grid_spec=pltpu.PrefetchScalarGridSpec(
            num_scalar_prefetch=2,
            grid=(m // tq,),
            in_specs=[pl.BlockSpec((tq, d), lambda i, a, b: (i, 0)),
                      pl.BlockSpec((tq, LANES), lambda i, a, b: (i, 0)),
                      pl.BlockSpec(memory_space=pl.ANY)],
            out_specs=pl.BlockSpec((tq, d), lambda i, a, b: (i, 0)),
            scratch_shapes=[pltpu.VMEM((2, tq, d), F32), pltpu.VMEM((2, tq, d), F32),
                            pltpu.SemaphoreType.DMA((2,)), pltpu.SemaphoreType.DMA((2,))]),
        out_shape=jax.ShapeDtypeStruct((m, d), F32),
        compiler_params=_params(("arbitrary",), 9 * 4 * tq * d),
        name="moe_combine",
    )(p1, p2, x, route, y)


def _pad_to(a, axis, size):
    if a.shape[axis] == size:
        return a
    pad = [(0, 0)] * a.ndim
    pad[axis] = (0, size - a.shape[axis])
    return jnp.pad(a, pad)


def _even_layer(x2d, batch, seq, tables, norm_mix, w_in, conv_w, q_norm, k_norm, pe_k, pe_v, w_cmp_k, w_cmp_v,
                w_out, norm_ffn, w1, w3, w2):
    d = x2d.shape[1]
    ch = d // 2
    heads = ch // HEAD_DIM
    hpg = heads // KV_HEADS
    kv_w = KV_HEADS * HEAD_DIM
    main = 3 * ch + heads * HEAD_DIM + 6 * kv_w
    assert main + heads * N_BRANCH == w_in.shape[1]

    h = rmsnorm(x2d, norm_mix, BF16)
    proj = matmul([(h, w_in.astype(BF16), 0)], None, F32, 1024, 1024, "in_proj", n=main)
    wg = w_in[:, main:].reshape(d, KV_HEADS, hpg * N_BRANCH)
    wg = _pad_to(wg, 2, LANES).reshape(d, KV_HEADS * LANES).astype(BF16)
    gates = matmul([(h, wg, 0)], None, F32, 1024, KV_HEADS * LANES, "gate_proj")

    y_a = short_conv(proj, conv_w.T, ch, seq)
    base = (3 * ch + heads * HEAD_DIM) // HEAD_DIM
    step = kv_w // HEAD_DIM
    blk = {"q": 3 * ch // HEAD_DIM, "kc": base, "vc": base + step, "ks": base + 2 * step,
           "vs": base + 3 * step, "kw": base + 4 * step, "vw": base + 5 * step}
    kn = k_norm.reshape(1, HEAD_DIM)
    kc, vc = compress(proj, blk["kc"], blk["vc"], pe_k, pe_v, w_cmp_k.astype(BF16), w_cmp_v.astype(BF16),
                      kn, batch, seq)
    y_b = nsa_attention(proj, gates, kc, vc, tables, q_norm.reshape(1, HEAD_DIM), kn, batch, seq, hpg, blk)

    wo = w_out.astype(BF16)
    x2d = matmul([(y_a, wo, 0), (y_b, wo, 1)], x2d, F32, 1024, 512, "out_proj")

    h2 = rmsnorm(x2d, norm_ffn, BF16)
    act = glu_matmul(h2, w1.astype(BF16), w3.astype(BF16), 1024, 256)
    return matmul_acc(act, w2.astype(BF16), x2d, 512, 1024, 5632, "dense_down")


def _odd_layer(x2d, seq, norm_mix, w_pool, pool_scale, norm_ffn, w_router, w1, w3, w2):
    m, d = x2d.shape
    x3, route = pool_and_route(x2d, norm_mix.reshape(1, d), w_pool.astype(BF16), pool_scale.reshape(1, d),
                               norm_ffn.reshape(1, d), w_router.T, seq)
    tm = min(512, m // 8)
    n_tiles = 2 * m // tm + N_EXPERTS
    tile_expert, tile_rows, row_token, n_used, p1, p2 = _route_tables(route, tm, n_tiles)
    hs = moe_gather(x3, norm_ffn.reshape(1, d), row_token, n_used, tm)
    act = moe_glu(hs, w1, w3, tile_expert, tile_rows, n_used, tm, 512)
    y = moe_down(act, w2.astype(BF16), tile_expert, tile_rows, n_used, tm, 1024)
    return moe_combine(x3, route, y, p1, p2)


def kernel(x, norm_mix_even, w_in_even, conv_w_even, q_norm_even, k_norm_even, cmp_pe_k_even, cmp_pe_v_even,
           w_cmp_k_even, w_cmp_v_even, w_out_even, norm_ffn_even, w1_dense, w3_dense, w2_dense,
           norm_mix_odd, w_pool_odd, pool_scale_odd, norm_ffn_odd, w_router_odd,
           w1_moe, w3_moe, w2_moe, rel_bias):
    batch, seq, d = x.shape
    depth = norm_mix_even.shape[0] + norm_mix_odd.shape[0]
    tables = bias_tables(rel_bias, seq)
    x2d = x.reshape(batch * seq, d)
    for layer in range(depth):
        i = layer // 2
        if layer % 2 == 0:
            x2d = _even_layer(x2d, batch, seq, tables, norm_mix_even[i], w_in_even[i], conv_w_even[i],
                              q_norm_even[i], k_norm_even[i], cmp_pe_k_even[i], cmp_pe_v_even[i],
                              w_cmp_k_even[i], w_cmp_v_even[i], w_out_even[i], norm_ffn_even[i],
                              w1_dense[i], w3_dense[i], w2_dense[i])
        else:
            x2d = _odd_layer(x2d, seq, norm_mix_odd[i], w_pool_odd[i], pool_scale_odd[i], norm_ffn_odd[i],
                             w_router_odd[i], w1_moe[i], w3_moe[i], w2_moe[i])
    return x2d.reshape(batch, seq, d)
```

```python
import functools
import math

import numpy as np
import jax
import jax.numpy as jnp
from jax import lax
from jax.experimental import pallas as pl
from jax.experimental.pallas import tpu as pltpu

F32 = jnp.float32
BF16 = jnp.bfloat16

HEAD_DIM = 128
KV_HEADS = 4
CONV_TAPS = 3
N_BRANCH = 3
CMP_BLOCK = 32
CMP_STRIDE = 16
SEL_BLOCK = 64
N_SELECT = 16
WINDOW = 512
FORCE_SCORE = 1e6
NEG_INF = -1e30
REL_BUCKETS = 32
REL_MAX_DIST = 128
POOL_WINDOWS = (2, 4, 8, 16)
N_EXPERTS = 8
EPS = 1e-6

LANES = 128
POOL_HALO = 16
CONV_HALO = 8
VMEM_CAP = 60 * 1024 * 1024
NT_DIMS = (((1,), (1,)), ((), ()))


def _pick(n, pref, mult=LANES):
    t = min(pref, n)
    while n % t or t % mult:
        t -= mult
    return t


def _params(sem, est_bytes):
    limit = int(min(max(est_bytes * 5 // 4 + (4 << 20), 16 << 20), VMEM_CAP))
    return pltpu.CompilerParams(dimension_semantics=sem, vmem_limit_bytes=limit)


def _rms(x, g):
    return x * lax.rsqrt(jnp.mean(x * x, axis=-1, keepdims=True) + EPS) * g


def _rmsnorm_kernel(x_ref, g_ref, o_ref):
    o_ref[...] = _rms(x_ref[...], g_ref[...]).astype(o_ref.dtype)


def rmsnorm(x, g, out_dtype, tm=256):
    m, d = x.shape
    tm = _pick(m, tm, 8)
    return pl.pallas_call(
        _rmsnorm_kernel,
        grid=(m // tm,),
        in_specs=[pl.BlockSpec((tm, d), lambda i: (i, 0)), pl.BlockSpec((1, d), lambda i: (0, 0))],
        out_specs=pl.BlockSpec((tm, d), lambda i: (i, 0)),
        out_shape=jax.ShapeDtypeStruct((m, d), out_dtype),
        compiler_params=_params(("parallel",), 2 * tm * d * 8),
        name="rmsnorm",
    )(x, g.reshape(1, d))


def _mm_kernel(*refs, n_pairs, has_res):
    o_ref = refs[-1]
    acc = None
    for p in range(n_pairs):
        d = jnp.dot(refs[2 * p][...], refs[2 * p + 1][...], preferred_element_type=F32)
        acc = d if acc is None else acc + d
    if has_res:
        acc = refs[2 * n_pairs][...] + acc
    o_ref[...] = acc.astype(o_ref.dtype)


def matmul(pairs, res, out_dtype, tm, tn, name, n=None):
    m = pairs[0][0].shape[0]
    n = pairs[0][1].shape[1] if n is None else n
    tm = _pick(m, tm)
    tn = _pick(n, tn)
    in_specs, args, est = [], [], 0
    for x, w, r in pairs:
        k = x.shape[1]
        in_specs += [pl.BlockSpec((tm, k), lambda i, j: (i, 0)), pl.BlockSpec((k, tn), lambda i, j, r=r: (r, j))]
        args += [x, w]
        est += 2 * 2 * (tm * k + k * tn)
    if res is not None:
        in_specs.append(pl.BlockSpec((tm, tn), lambda i, j: (i, j)))
        args.append(res)
        est += 2 * 4 * tm * tn
    est += 2 * 4 * tm * tn + 4 * tm * tn
    return pl.pallas_call(
        functools.partial(_mm_kernel, n_pairs=len(pairs), has_res=res is not None),
        grid=(m // tm, n // tn),
        in_specs=in_specs,
        out_specs=pl.BlockSpec((tm, tn), lambda i, j: (i, j)),
        out_shape=jax.ShapeDtypeStruct((m, n), out_dtype),
        compiler_params=_params(("parallel", "parallel"), est),
        name=name,
    )(*args)


def _glu_kernel(h_ref, w1_ref, w3_ref, o_ref):
    h = h_ref[...]
    a = jnp.dot(h, w1_ref[...], preferred_element_type=F32)
    b = jnp.dot(h, w3_ref[...], preferred_element_type=F32)
    o_ref[...] = (a * jax.nn.sigmoid(a) * b).astype(o_ref.dtype)


def glu_matmul(h, w1, w3, tm, tn):
    m, k = h.shape
    n = w1.shape[1]
    tm = _pick(m, tm)
    tn = _pick(n, tn)
    est = 2 * 2 * (tm * k + 2 * k * tn + tm * tn) + 3 * 4 * tm * tn
    return pl.pallas_call(
        _glu_kernel,
        grid=(m // tm, n // tn),
        in_specs=[pl.BlockSpec((tm, k), lambda i, j: (i, 0)),
                  pl.BlockSpec((k, tn), lambda i, j: (0, j)),
                  pl.BlockSpec((k, tn), lambda i, j: (0, j))],
        out_specs=pl.BlockSpec((tm, tn), lambda i, j: (i, j)),
        out_shape=jax.ShapeDtypeStruct((m, n), BF16),
        compiler_params=_params(("parallel", "parallel"), est),
        name="dense_glu",
    )(h, w1, w3)


def _mm_acc_kernel(x_ref, w_ref, r_ref, o_ref, acc_ref):
    k = pl.program_id(2)

    @pl.when(k == 0)
    def _():
        acc_ref[...] = jnp.zeros_like(acc_ref)

    acc_ref[...] += jnp.dot(x_ref[...], w_ref[...], preferred_element_type=F32)

    @pl.when(k == pl.num_programs(2) - 1)
    def _():
        o_ref[...] = r_ref[...] + acc_ref[...]


def matmul_acc(x, w, res, tm, tn, tk, name):
    m, k = x.shape
    n = w.shape[1]
    tm, tn, tk = _pick(m, tm), _pick(n, tn), _pick(k, tk)
    est = 2 * 2 * (tm * tk + tk * tn) + 5 * 4 * tm * tn
    return pl.pallas_call(
        _mm_acc_kernel,
        grid=(m // tm, n // tn, k // tk),
        in_specs=[pl.BlockSpec((tm, tk), lambda i, j, kk: (i, kk)),
                  pl.BlockSpec((tk, tn), lambda i, j, kk: (kk, j)),
                  pl.BlockSpec((tm, tn), lambda i, j, kk: (i, j))],
        out_specs=pl.BlockSpec((tm, tn), lambda i, j, kk: (i, j)),
        out_shape=jax.ShapeDtypeStruct((m, n), F32),
        scratch_shapes=[pltpu.VMEM((tm, tn), F32)],
        compiler_params=_params(("parallel", "parallel", "arbitrary"), est),
        name=name,
    )(x, w, res)


def _conv_kernel(ab_ref, ac_ref, au_ref, hc_ref, hu_ref, w_ref, o_ref, *, tq, seq):
    i = pl.program_id(0)
    first = (i * tq) % seq == 0
    v = ac_ref[...] * au_ref[...]
    hv = jnp.where(first, 0.0, hc_ref[...] * hu_ref[...])
    rows = lax.broadcasted_iota(jnp.int32, v.shape, 0)
    v1 = jnp.where(rows == 0, hv[CONV_HALO - 1:CONV_HALO], pltpu.roll(v, 1, 0))
    v2 = jnp.where(rows == 0, hv[CONV_HALO - 2:CONV_HALO - 1],
                   jnp.where(rows == 1, hv[CONV_HALO - 1:CONV_HALO], pltpu.roll(v, 2, 0)))
    y = w_ref[0:1, :] * v2 + w_ref[1:2, :] * v1 + w_ref[2:3, :] * v
    o_ref[...] = (ab_ref[...] * y).astype(o_ref.dtype)


def short_conv(proj, conv_w_t, ch, seq, tq=512, tc=1024):
    m = proj.shape[0]
    tq = _pick(seq, tq, CONV_HALO)
    tc = _pick(ch, tc)
    nj = ch // tc
    hb = tq // CONV_HALO
    halo = lambda off: pl.BlockSpec((CONV_HALO, tc), lambda i, j: (jnp.maximum(i * hb - 1, 0), off + j))
    return pl.pallas_call(
        functools.partial(_conv_kernel, tq=tq, seq=seq),
        grid=(m // tq, nj),
        in_specs=[pl.BlockSpec((tq, tc), lambda i, j: (i, j)),
                  pl.BlockSpec((tq, tc), lambda i, j: (i, nj + j)),
                  pl.BlockSpec((tq, tc), lambda i, j: (i, 2 * nj + j)),
                  halo(nj), halo(2 * nj),
                  pl.BlockSpec((CONV_TAPS, tc), lambda i, j: (0, j))],
        out_specs=pl.BlockSpec((tq, tc), lambda i, j: (i, j)),
        out_shape=jax.ShapeDtypeStruct((m, ch), BF16),
        compiler_params=_params(("parallel", "parallel"), 2 * 4 * 4 * tq * tc + 6 * 4 * tq * tc),
        name="short_conv",
    )(proj, proj, proj, proj, proj, conv_w_t)


def _rel_bucket_np(dist):
    dist = np.maximum(dist, 0)
    exact = REL_BUCKETS // 2
    d = np.maximum(dist, exact).astype(np.float32)
    large = exact + (np.log(d / np.float32(exact)) / np.float32(math.log(REL_MAX_DIST / exact))
                     * np.float32(REL_BUCKETS - exact)).astype(np.int32)
    return np.where(dist < exact, dist, np.minimum(large, REL_BUCKETS - 1)).astype(np.int32)


WIN_TILES = WINDOW // LANES
N_BIAS_TILES = 2 * WIN_TILES + 1


def _bias_index_tables(seq):
    r = np.arange(LANES)[None, :]
    c = np.arange(LANES)[:, None]
    far = np.full((LANES, LANES), REL_BUCKETS - 1, np.int32)
    zero = np.zeros((LANES, LANES), np.float32)
    neg = np.full((LANES, LANES), NEG_INF, np.float32)
    assert _rel_bucket_np(np.arange(LANES + 1, 4 * seq)).min() == REL_BUCKETS - 1
    idx, add = [], []
    for d in range(-WIN_TILES, WIN_TILES + 1):
        if d < 0:
            idx.append(far), add.append(neg)
        elif d == 0:
            idx.append(_rel_bucket_np(r - c)), add.append(np.where(r >= c, zero, neg))
        elif d == 1:
            idx.append(_rel_bucket_np(LANES + r - c)), add.append(zero)
        elif d < WIN_TILES:
            idx.append(far), add.append(zero)
        else:
            idx.append(far), add.append(np.where(c > r, zero, neg))
    n_tile_rows = len(idx) * LANES
    for qt in range(seq // LANES):
        idx.append(_rel_bucket_np(qt * LANES + r - (c * CMP_STRIDE + CMP_BLOCK - 1)))
        add.append(zero)
    idx = np.concatenate(idx, axis=0).astype(np.int32)
    add = np.concatenate(add, axis=0).astype(np.float32)
    shift = (np.arange(idx.shape[0])[:, None] < n_tile_rows).astype(np.float32) * np.ones((1, LANES), np.float32)
    return idx, shift, add


def _bias_kernel(tbl_ref, idx_ref, shift_ref, add_ref, o_ref, *, hpg):
    g = pl.program_id(0)
    idx = idx_ref[...]
    for h in range(hpg):
        head = g * hpg + h
        acc = jnp.zeros(idx.shape, F32)
        for b in range(REL_BUCKETS):
            acc = jnp.where(idx == b, tbl_ref[b, head], acc)
        o_ref[0, :, h * LANES:(h + 1) * LANES] = (acc - shift_ref[...] * tbl_ref[REL_BUCKETS - 1, head]
                                                   + add_ref[...])


def bias_tables(rel_bias, seq):
    heads = rel_bias.shape[1]
    hpg = heads // KV_HEADS
    idx, shift, add = (jnp.asarray(a) for a in _bias_index_tables(seq))
    rows = idx.shape[0]
    full = pl.BlockSpec((rows, LANES), lambda g: (0, 0))
    return pl.pallas_call(
        functools.partial(_bias_kernel, hpg=hpg),
        grid=(KV_HEADS,),
        in_specs=[pl.BlockSpec(memory_space=pltpu.SMEM), full, full, full],
        out_specs=pl.BlockSpec((1, rows, hpg * LANES), lambda g: (g, 0, 0)),
        out_shape=jax.ShapeDtypeStruct((KV_HEADS, rows, hpg * LANES), F32),
        compiler_params=_params(("parallel",), (6 + 2 * hpg + 4) * 4 * rows * LANES),
        name="rel_bias_tables",
    )(rel_bias, idx, shift, add)


def _compress_kernel(k_ref, v_ref, pek_ref, pev_ref, wk_ref, wv_ref, kn_ref, kc_ref, vc_ref, *, nch):
    half = CMP_BLOCK // CMP_STRIDE
    assert half == 2

    def comp(x_ref, pe_ref, w_ref):
        lo = jnp.zeros((nch, HEAD_DIM), F32)
        hi = jnp.zeros((nch, HEAD_DIM), F32)
        for l in range(CMP_STRIDE):
            rows = x_ref[pl.ds(l, nch, stride=CMP_STRIDE), :]
            lo += jnp.dot((rows + pe_ref[l:l + 1, :]).astype(BF16), w_ref[l], preferred_element_type=F32)
            hi += jnp.dot((rows + pe_ref[CMP_STRIDE + l:CMP_STRIDE + l + 1, :]).astype(BF16),
                          w_ref[CMP_STRIDE + l], preferred_element_type=F32)
        return lo + pltpu.roll(hi, nch - 1, 0)

    kc_ref[0, 0] = _rms(comp(k_ref, pek_ref, wk_ref), kn_ref[...])
    vc_ref[0, 0] = comp(v_ref, pev_ref, wv_ref).T


def compress(proj, kc_blk, vc_blk, pe_k, pe_v, w_k, w_v, k_norm, batch, seq):
    nch = seq // CMP_STRIDE
    wspec = pl.BlockSpec((CMP_BLOCK, HEAD_DIM, HEAD_DIM), lambda b, g: (0, 0, 0))
    pespec = pl.BlockSpec((CMP_BLOCK, HEAD_DIM), lambda b, g: (0, 0))
    ospec = pl.BlockSpec((1, 1, nch, HEAD_DIM), lambda b, g: (b, g, 0, 0))
    oshape = jax.ShapeDtypeStruct((batch, KV_HEADS, nch, HEAD_DIM), F32)
    return pl.pallas_call(
        functools.partial(_compress_kernel, nch=nch),
        grid=(batch, KV_HEADS),
        in_specs=[pl.BlockSpec((seq, HEAD_DIM), lambda b, g: (b, kc_blk + g)),
                  pl.BlockSpec((seq, HEAD_DIM), lambda b, g: (b, vc_blk + g)),
                  pespec, pespec, wspec, wspec,
                  pl.BlockSpec((1, HEAD_DIM), lambda b, g: (0, 0))],
        out_specs=[ospec, ospec],
        out_shape=[oshape, oshape],
        compiler_params=_params(("parallel", "parallel"), 4 * 4 * seq * HEAD_DIM + (4 << 20)),
        name="nsa_compress",
    )(proj, proj, pe_k, pe_v, w_k, w_v, k_norm)


SEL_CHUNK = 512


def _nsa_kernel(q_ref, ks_ref, vs_ref, kw_ref, vw_ref, kc_ref, vct_ref, wt_ref, bc_ref, gt_ref,
                qn_ref, kn_ref, cov_ref, oh_ref, o_ref, ksa, vst, kwn, vwt, *, hpg, n_cmp, n_top):
    qi = pl.program_id(2)
    cols_all = hpg * LANES
    scale = HEAD_DIM ** -0.5
    n_sel_blocks = cov_ref.shape[0]
    chunk_tiles = SEL_CHUNK // LANES
    seq = ks_ref.shape[0]

    @pl.when(qi == 0)
    def _():
        ksa[:, :HEAD_DIM] = _rms(ks_ref[...], kn_ref[...]).astype(BF16)
        ksa[:, HEAD_DIM:] = oh_ref[...]
        kwn[...] = _rms(kw_ref[...], kn_ref[...]).astype(BF16)
        for c in range(seq // LANES):
            blk = slice(c * LANES, (c + 1) * LANES)
            vst[:, blk] = vs_ref[blk, :].T.astype(BF16)
            vwt[:, blk] = vw_ref[blk, :].T.astype(BF16)

    qs = [_rms(q_ref[:, h * HEAD_DIM:(h + 1) * HEAD_DIM], qn_ref[...]).astype(BF16) for h in range(hpg)]
    q = jnp.concatenate(qs, axis=0) if hpg > 1 else qs[0]

    def bias_tile(d):
        off = pl.multiple_of((d + WIN_TILES) * LANES, LANES)
        return wt_ref[0, pl.ds(off, LANES), :]

    blk_row = lax.broadcasted_iota(jnp.int32, (LANES, cols_all), 0)
    qry = lax.broadcasted_iota(jnp.int32, (LANES, cols_all), 1) & (LANES - 1)

    lc = lax.dot_general(kc_ref[0, 0].astype(BF16), q, NT_DIMS, preferred_element_type=F32) * scale + bc_ref[0]
    mc = ((qi * LANES + qry - (blk_row * CMP_STRIDE + CMP_BLOCK - 1)) >= 0) & (blk_row < n_cmp)
    zc = jnp.where(mc, lc, NEG_INF)
    ec = jnp.exp(zc - jnp.max(zc, axis=0, keepdims=True))
    pc = ec * (1.0 / jnp.sum(ec, axis=0, keepdims=True)) * mc.astype(F32)
    o_cmp = jnp.dot(vct_ref[0, 0].astype(BF16), pc.astype(BF16), preferred_element_type=F32)

    ps = pc[:, 0:LANES]
    for h in range(1, hpg):
        ps = ps + pc[:, h * LANES:(h + 1) * LANES]
    p1 = ps.astype(BF16)
    r1 = ps - p1.astype(F32)
    p2 = r1.astype(BF16)
    p3 = (r1 - p2.astype(F32)).astype(BF16)
    cov = cov_ref[...]
    score = (jnp.dot(cov, p1, preferred_element_type=F32) + jnp.dot(cov, p2, preferred_element_type=F32)
             + jnp.dot(cov, p3, preferred_element_type=F32))
    jj = lax.broadcasted_iota(jnp.int32, (n_sel_blocks, LANES), 0)
    ql = lax.broadcasted_iota(jnp.int32, (n_sel_blocks, LANES), 1)
    cur = (LANES // SEL_BLOCK) * qi + ql // SEL_BLOCK
    forced = (jj == 0) | (jj == cur) | (jj == cur - 1)
    score = jnp.where(forced, FORCE_SCORE, jnp.where(jj > cur, -FORCE_SCORE, score))
    rank = jnp.zeros((n_sel_blocks, LANES), F32)
    for j2 in range(n_sel_blocks):
        other = score[j2:j2 + 1, :]
        rank += ((other > score) | ((other == score) & (j2 < jj))).astype(F32)
    sel_t = ((rank < n_top) & (jj <= cur)).astype(BF16)
    sel_t = jnp.concatenate([sel_t, jnp.ones((LANES - n_sel_blocks, LANES), BF16)], axis=0)
    eye = (lax.broadcasted_iota(jnp.int32, (LANES, LANES), 0)
           == lax.broadcasted_iota(jnp.int32, (LANES, LANES), 1)).astype(BF16)
    selq = lax.dot_general(eye, sel_t, NT_DIMS, preferred_element_type=F32)
    sel_neg = ((selq - 1.0) * (-NEG_INF)).astype(BF16)
    sel_neg = jnp.concatenate([sel_neg] * hpg, axis=0) if hpg > 1 else sel_neg
    qa = jnp.concatenate([q, sel_neg], axis=1)

    def sel_chunk(c, carry, near):
        m, l, acc = carry
        off = pl.multiple_of(c * SEL_CHUNK, SEL_CHUNK)
        s = lax.dot_general(ksa[pl.ds(off, SEL_CHUNK), :], qa, NT_DIMS, preferred_element_type=F32) * scale
        if near:
            s = jnp.concatenate([s[u * LANES:(u + 1) * LANES]
                                 + bias_tile(jnp.clip(qi - (c * chunk_tiles + u), 0, 2))
                                 for u in range(chunk_tiles)], axis=0)
        m_new = jnp.maximum(m, jnp.max(s, axis=0, keepdims=True))
        alpha = jnp.exp(m - m_new)
        p = jnp.exp(s - m_new)
        l = alpha * l + jnp.sum(p, axis=0, keepdims=True)
        acc = alpha * acc + jnp.dot(vst[:, pl.ds(off, SEL_CHUNK)], p.astype(BF16), preferred_element_type=F32)
        return m_new, l, acc

    init = (jnp.full((1, cols_all), NEG_INF, F32), jnp.zeros((1, cols_all), F32),
            jnp.zeros((HEAD_DIM, cols_all), F32))
    n_far = (jnp.maximum(qi, 1) - 1) // chunk_tiles
    carry = lax.fori_loop(0, n_far, functools.partial(sel_chunk, near=False), init)
    _, l_s, a_s = lax.fori_loop(n_far, qi // chunk_tiles + 1, functools.partial(sel_chunk, near=True), carry)

    win_keys = WINDOW + LANES
    kt0 = jnp.maximum(qi - WIN_TILES, 0)
    woff = pl.multiple_of(kt0 * LANES, LANES)
    sw = lax.dot_general(kwn[pl.ds(woff, win_keys), :], q, NT_DIMS, preferred_element_type=F32) * scale
    zw = jnp.concatenate([sw[u * LANES:(u + 1) * LANES] + bias_tile(qi - (kt0 + u))
                          for u in range(win_keys // LANES)], axis=0)
    pw = jnp.exp(zw - jnp.max(zw, axis=0, keepdims=True))
    l_w = jnp.sum(pw, axis=0, keepdims=True)
    a_w = jnp.dot(vwt[:, pl.ds(woff, win_keys)], pw.astype(BF16), preferred_element_type=F32)

    gt = jax.nn.sigmoid(gt_ref[...]).T
    inv_s = 1.0 / l_s
    inv_w = 1.0 / l_w
    for h in range(hpg):
        sl = slice(h * LANES, (h + 1) * LANES)
        c0 = h * N_BRANCH
        o = (gt[c0:c0 + 1] * o_cmp[:, sl] + (gt[c0 + 1:c0 + 2] * inv_s[:, sl]) * a_s[:, sl]
             + (gt[c0 + 2:c0 + 3] * inv_w[:, sl]) * a_w[:, sl])
        o_ref[:, h * HEAD_DIM:(h + 1) * HEAD_DIM] = o.T.astype(o_ref.dtype)


def _cover_t(seq):
    nc = seq // CMP_STRIDE - CMP_BLOCK // CMP_STRIDE + 1
    ns = seq // SEL_BLOCK
    c_start = np.arange(nc) * CMP_STRIDE
    c_end = c_start + CMP_BLOCK - 1
    s_start = np.arange(ns) * SEL_BLOCK
    cover = (c_start[:, None] <= s_start[None, :] + SEL_BLOCK - 1) & (c_end[:, None] >= s_start[None, :])
    out = np.zeros((ns, seq // CMP_STRIDE), np.float32)
    out[:, :nc] = cover.T
    return out


def nsa_attention(proj, gates, kc, vc, tables, q_norm, k_norm, batch, seq, hpg, blk):
    heads = KV_HEADS * hpg
    nq = seq // LANES
    nch = seq // CMP_STRIDE
    assert nch == LANES, "compressed keys must fill exactly one lane tile"
    n_cmp = nch - CMP_BLOCK // CMP_STRIDE + 1
    ns = seq // SEL_BLOCK
    assert seq % SEL_CHUNK == 0 and ns <= LANES
    cov = jnp.asarray(_cover_t(seq), BF16)
    onehot = jnp.asarray(np.arange(seq)[:, None] // SEL_BLOCK == np.arange(LANES)[None, :], BF16)
    qw = hpg * HEAD_DIM
    q_blk = blk["q"] * HEAD_DIM // qw
    kv = lambda name: pl.BlockSpec((seq, HEAD_DIM), lambda b, g, i: (b, blk[name] + g))
    cspec = pl.BlockSpec((1, 1, nch, HEAD_DIM), lambda b, g, i: (b, g, 0, 0))
    vec = pl.BlockSpec((1, HEAD_DIM), lambda b, g, i: (0, 0))
    rows_all = hpg * LANES
    est = (2 * 4 * 4 * seq * HEAD_DIM + 5 * 2 * seq * HEAD_DIM + 2 * 2 * seq * LANES
           + 2 * 4 * hpg * (N_BIAS_TILES + 1) * LANES * LANES + 6 * 4 * rows_all * (WINDOW + LANES) + (8 << 20))
    return pl.pallas_call(
        functools.partial(_nsa_kernel, hpg=hpg, n_cmp=n_cmp, n_top=min(N_SELECT, ns)),
        grid=(batch, KV_HEADS, nq),
        in_specs=[pl.BlockSpec((LANES, qw), lambda b, g, i: (b * nq + i, q_blk + g)),
                  kv("ks"), kv("vs"), kv("kw"), kv("vw"), cspec, cspec,
                  pl.BlockSpec((1, N_BIAS_TILES * LANES, qw), lambda b, g, i: (g, 0, 0)),
                  pl.BlockSpec((1, LANES, qw), lambda b, g, i: (g, N_BIAS_TILES + i, 0)),
                  pl.BlockSpec((LANES, LANES), lambda b, g, i: (b * nq + i, g)),
                  vec, vec,
                  pl.BlockSpec((ns, nch), lambda b, g, i: (0, 0)),
                  pl.BlockSpec((seq, LANES), lambda b, g, i: (0, 0))],
        out_specs=pl.BlockSpec((LANES, qw), lambda b, g, i: (b * nq + i, g)),
        out_shape=jax.ShapeDtypeStruct((batch * seq, heads * HEAD_DIM), BF16),
        scratch_shapes=[pltpu.VMEM((seq, 2 * HEAD_DIM), BF16), pltpu.VMEM((HEAD_DIM, seq), BF16),
                        pltpu.VMEM((seq, HEAD_DIM), BF16), pltpu.VMEM((HEAD_DIM, seq), BF16)],
        compiler_params=_params(("parallel", "parallel", "arbitrary"), est),
        name="nsa_attention",
    )(proj, proj, proj, proj, proj, kc, vc, tables, tables, gates, q_norm, k_norm, cov, onehot)


def _pool_kernel(x_ref, halo_ref, gm_ref, wp_ref, ps_ref, gf_ref, wr_ref, x_out, route_out, *, tq, seq, cg):
    i = pl.program_id(0)
    start = (i * tq) % seq
    x = x_ref[...]
    h = _rms(x, gm_ref[...])
    hh = jnp.where(start == 0, 0.0, _rms(halo_ref[...], gm_ref[...]))
    t1 = (start + lax.broadcasted_iota(jnp.int32, (tq, 1), 0) + 1).astype(F32)
    ys = []
    for gi, w in enumerate(POOL_WINDOWS):
        sl = slice(gi * cg, (gi + 1) * cg)
        s = jnp.concatenate([hh[:, sl], h[:, sl]], axis=0)
        span = 1
        while span < w:
            s = s + pltpu.roll(s, span, 0)
            span *= 2
        dm = s[POOL_HALO:] / jnp.minimum(t1, float(w)) - h[:, sl]
        ys.append(jnp.dot(dm.astype(BF16), wp_ref[gi], preferred_element_type=F32))
    x3 = x + jnp.concatenate(ys, axis=1) * ps_ref[...]
    x_out[...] = x3

    h4 = _rms(x3, gf_ref[...])
    lane = lax.broadcasted_iota(jnp.int32, (tq, LANES), 1)
    logits = jnp.full((tq, LANES), -jnp.inf, F32)
    for e in range(N_EXPERTS):
        logits = jnp.where(lane == e, jnp.sum(h4 * wr_ref[e:e + 1, :], axis=-1, keepdims=True), logits)
    m1 = jnp.max(logits, axis=-1, keepdims=True)
    i1 = jnp.min(jnp.where(logits == m1, lane, LANES), axis=-1, keepdims=True)
    rest = jnp.where(lane == i1, -jnp.inf, logits)
    m2 = jnp.max(rest, axis=-1, keepdims=True)
    i2 = jnp.min(jnp.where(rest == m2, lane, LANES), axis=-1, keepdims=True)
    e2 = jnp.exp(m2 - m1)
    den = 1.0 + e2
    route = jnp.where(lane == 0, i1.astype(F32),
                      jnp.where(lane == 1, i2.astype(F32),
                                jnp.where(lane == 2, 1.0 / den, jnp.where(lane == 3, e2 / den, 0.0))))
    route_out[...] = route


def pool_and_route(x, g_mix, w_pool, pool_scale, g_ffn, w_router_t, seq, tq=256):
    m, d = x.shape
    cg = d // len(POOL_WINDOWS)
    tq = _pick(seq, tq, POOL_HALO)
    hb = tq // POOL_HALO
    vec = pl.BlockSpec((1, d), lambda i: (0, 0))
    est = 2 * 2 * 4 * tq * d + 2 * 2 * len(POOL_WINDOWS) * cg * cg + 8 * 4 * tq * d
    return pl.pallas_call(
        functools.partial(_pool_kernel, tq=tq, seq=seq, cg=cg),
        grid=(m // tq,),
        in_specs=[pl.BlockSpec((tq, d), lambda i: (i, 0)),
                  pl.BlockSpec((POOL_HALO, d), lambda i: (jnp.maximum(i * hb - 1, 0), 0)),
                  vec,
                  pl.BlockSpec((len(POOL_WINDOWS), cg, cg), lambda i: (0, 0, 0)),
                  vec, vec,
                  pl.BlockSpec((N_EXPERTS, d), lambda i: (0, 0))],
        out_specs=[pl.BlockSpec((tq, d), lambda i: (i, 0)), pl.BlockSpec((tq, LANES), lambda i: (i, 0))],
        out_shape=[jax.ShapeDtypeStruct((m, d), F32), jax.ShapeDtypeStruct((m, LANES), F32)],
        compiler_params=_params(("parallel",), est),
        name="pool_mixer_router",
    )(x, x, g_mix, w_pool, pool_scale, g_ffn, w_router_t)


def _route_tables(route, tm, n_tiles):
    n = route.shape[0]
    i1 = route[:, 0].astype(jnp.int32)
    i2 = route[:, 1].astype(jnp.int32)
    onehot = jax.nn.one_hot(i1, N_EXPERTS, dtype=jnp.int32) + jax.nn.one_hot(i2, N_EXPERTS, dtype=jnp.int32)
    count = jnp.sum(onehot, axis=0)
    padded = (count + tm - 1) // tm * tm
    end = jnp.cumsum(padded)
    pos = (end - padded)[None, :] + jnp.cumsum(onehot, axis=0) - onehot
    p1 = jnp.take_along_axis(pos, i1[:, None], axis=1)[:, 0]
    p2 = jnp.take_along_axis(pos, i2[:, None], axis=1)[:, 0]
    tok = jnp.arange(n, dtype=jnp.int32)
    rows = n_tiles * tm
    row_token = jnp.zeros((rows,), jnp.int32).at[jnp.concatenate([p1, p2])].set(jnp.concatenate([tok, tok]))
    n_used = (end[-1] // tm).astype(jnp.int32)
    tile = jnp.arange(n_tiles, dtype=jnp.int32)
    tile_expert = jnp.searchsorted(end, tile * tm, side="right").astype(jnp.int32)
    tile_expert = jnp.where(tile < n_used, tile_expert, tile_expert[n_used - 1])
    tile_rows = jnp.clip((count - padded + end)[tile_expert] - tile * tm, 0, tm)
    tile_rows = jnp.where(tile < n_used, tile_rows, 0).astype(jnp.int32)
    ids = jnp.arange(N_EXPERTS, dtype=jnp.int32)
    later = jnp.where((ids[None, :] > ids[:, None]) & (count[None, :] > 0), ids[None, :], N_EXPERTS)
    following = jnp.min(later, axis=1)
    next_expert = jnp.where(following < N_EXPERTS, following, -1)[tile_expert].astype(jnp.int32)
    return (tile_expert, tile_rows, next_expert, row_token, n_used.reshape(1),
            p1.astype(jnp.int32), p2.astype(jnp.int32))


def _gather_start(idx_ref, base, src_hbm, dst_ref, sem, count):
    def issue(r, carry):
        pltpu.make_async_copy(src_hbm.at[pl.ds(idx_ref[base + r], 1), :], dst_ref.at[pl.ds(r, 1), :], sem).start()
        return carry

    lax.fori_loop(0, count, issue, 0, unroll=8)


def _gather_wait(src_hbm, dst_ref, sem, count):
    pltpu.make_async_copy(src_hbm.at[pl.ds(0, count), :], dst_ref, sem).wait()


MOE_SUB = 256


def _moe_gather_kernel(rt_ref, nu_ref, x_hbm, g_ref, o_ref, xg_ref, sem, *, tm):
    i = pl.program_id(0)
    valid = i < nu_ref[0]
    slot = i % 2

    @pl.when(i == 0)
    def _():
        _gather_start(rt_ref, 0, x_hbm, xg_ref.at[0], sem.at[0], tm)

    @pl.when(i + 1 < nu_ref[0])
    def _():
        _gather_start(rt_ref, (i + 1) * tm, x_hbm, xg_ref.at[1 - slot], sem.at[1 - slot], tm)

    @pl.when(valid)
    def _():
        _gather_wait(x_hbm, xg_ref.at[slot], sem.at[slot], tm)
        o_ref[...] = _rms(xg_ref[slot], g_ref[...]).astype(o_ref.dtype)

    @pl.when(jnp.logical_not(valid))
    def _():
        o_ref[...] = jnp.zeros_like(o_ref)


def moe_gather(x, g_ffn, row_token, n_used, tm):
    d = x.shape[1]
    n_tiles = row_token.shape[0] // tm
    return pl.pallas_call(
        functools.partial(_moe_gather_kernel, tm=tm),
        grid_spec=pltpu.PrefetchScalarGridSpec(
            num_scalar_prefetch=2,
            grid=(n_tiles,),
            in_specs=[pl.BlockSpec(memory_space=pl.ANY), pl.BlockSpec((1, d), lambda i, rt, nu: (0, 0))],
            out_specs=pl.BlockSpec((tm, d), lambda i, rt, nu: (i, 0)),
            scratch_shapes=[pltpu.VMEM((2, tm, d), F32), pltpu.SemaphoreType.DMA((2,))]),
        out_shape=jax.ShapeDtypeStruct((n_tiles * tm, d), BF16),
        compiler_params=_params(("arbitrary",), 2 * 4 * tm * d + 2 * 2 * tm * d + 2 * 4 * tm * d),
        name="moe_gather",
    )(row_token, n_used, x, g_ffn)


def _moe_glu_kernel(te_ref, tr_ref, nu_ref, nx_ref, h_ref, w1_hbm, w3_hbm, o_ref, s1, s3, w1b, w3b, sem, *, tm, tf):
    f = pl.program_id(0)
    i = pl.program_id(1)
    valid = i < nu_ref[0]

    def copies(e, col_tile):
        cols = pl.ds(pl.multiple_of(col_tile * tf, tf), tf)
        return (pltpu.make_async_copy(w1_hbm.at[e, :, cols], s1, sem.at[0]),
                pltpu.make_async_copy(w3_hbm.at[e, :, cols], s3, sem.at[1]))

    def start(e, col_tile):
        for c in copies(e, col_tile):
            c.start()

    @pl.when((f == 0) & (i == 0))
    def _():
        start(te_ref[0], 0)

    @pl.when(valid & ((i == 0) | (te_ref[i] != te_ref[jnp.maximum(i - 1, 0)])))
    def _():
        for c in copies(0, 0):
            c.wait()
        def cast_rows(r, carry):
            rows = pl.ds(pl.multiple_of(r * MOE_SUB, MOE_SUB), MOE_SUB)
            w1b[rows, :] = s1[rows, :].astype(BF16)
            w3b[rows, :] = s3[rows, :].astype(BF16)
            return carry

        lax.fori_loop(0, s1.shape[0] // MOE_SUB, cast_rows, 0)
        nxt = nx_ref[i]

        @pl.when(nxt >= 0)
        def _():
            start(nxt, f)

        @pl.when((nxt < 0) & (f + 1 < pl.num_programs(0)))
        def _():
            start(te_ref[0], f + 1)

    for sb in range(tm // MOE_SUB):
        rows = pl.ds(sb * MOE_SUB, MOE_SUB)
        live = valid & (sb * MOE_SUB < tr_ref[i])

        @pl.when(live)
        def _():
            h = h_ref[rows, :]
            a = jnp.dot(h, w1b[...], preferred_element_type=F32)
            b = jnp.dot(h, w3b[...], preferred_element_type=F32)
            o_ref[rows, :] = (a * jax.nn.sigmoid(a) * b).astype(o_ref.dtype)

        @pl.when(jnp.logical_not(live))
        def _():
            o_ref[rows, :] = jnp.zeros((MOE_SUB, o_ref.shape[1]), o_ref.dtype)


def moe_glu(h, w1, w3, tile_expert, tile_rows, n_used, next_expert, tm, tf):
    d = h.shape[1]
    ff = w1.shape[2]
    tf = _pick(ff, tf)
    n_tiles = tile_expert.shape[0]
    est = 2 * 4 * d * tf + 2 * 2 * d * tf + 2 * 2 * tm * d + 2 * 2 * tm * tf + 4 * 4 * MOE_SUB * tf + 4 * d * tf
    return pl.pallas_call(
        functools.partial(_moe_glu_kernel, tm=tm, tf=tf),
        grid_spec=pltpu.PrefetchScalarGridSpec(
            num_scalar_prefetch=4,
            grid=(ff // tf, n_tiles),
            in_specs=[pl.BlockSpec((tm, d), lambda f, i, te, tr, nu, nx: (jnp.minimum(i, nu[0] - 1), 0)),
                      pl.BlockSpec(memory_space=pl.ANY), pl.BlockSpec(memory_space=pl.ANY)],
            out_specs=pl.BlockSpec((tm, tf), lambda f, i, te, tr, nu, nx: (i, f)),
            scratch_shapes=[pltpu.VMEM((d, tf), F32), pltpu.VMEM((d, tf), F32),
                            pltpu.VMEM((d, tf), BF16), pltpu.VMEM((d, tf), BF16),
                            pltpu.SemaphoreType.DMA((2,))]),
        out_shape=jax.ShapeDtypeStruct((n_tiles * tm, ff), BF16),
        compiler_params=_params(("arbitrary", "arbitrary"), est),
        name="moe_glu",
    )(tile_expert, tile_rows, n_used, next_expert, h, w1, w3)


def _moe_down_kernel(te_ref, tr_ref, nu_ref, a_ref, w_ref, o_ref, *, tm):
    i = pl.program_id(0)
    valid = i < nu_ref[0]
    for sb in range(tm // MOE_SUB):
        rows = pl.ds(sb * MOE_SUB, MOE_SUB)
        live = valid & (sb * MOE_SUB < tr_ref[i])

        @pl.when(live)
        def _():
            o_ref[rows, :] = jnp.dot(a_ref[rows, :], w_ref[0], preferred_element_type=F32)

        @pl.when(jnp.logical_not(live))
        def _():
            o_ref[rows, :] = jnp.zeros((MOE_SUB, o_ref.shape[1]), o_ref.dtype)


def moe_down(act, w2, tile_expert, tile_rows, n_used, tm, tn):
    ff = act.shape[1]
    d = w2.shape[2]
    tn = _pick(d, tn)
    nn = d // tn
    n_tiles = tile_expert.shape[0]
    est = 2 * 2 * (tm * ff + ff * tn) + 2 * 4 * tm * tn + 4 * MOE_SUB * tn
    return pl.pallas_call(
        functools.partial(_moe_down_kernel, tm=tm),
        grid_spec=pltpu.PrefetchScalarGridSpec(
            num_scalar_prefetch=3,
            grid=(n_tiles, nn),
            in_specs=[pl.BlockSpec((tm, ff), lambda i, j, te, tr, nu: (jnp.minimum(i, nu[0] - 1), 0)),
                      pl.BlockSpec((1, ff, tn),
                                   lambda i, j, te, tr, nu: (te[i], 0, jnp.where(i < nu[0], j, nn - 1)))],
            out_specs=pl.BlockSpec((tm, tn), lambda i, j, te, tr, nu: (i, j))),
        out_shape=jax.ShapeDtypeStruct((n_tiles * tm, d), F32),
        compiler_params=_params(("arbitrary", "arbitrary"), est),
        name="moe_down",
    )(tile_expert, tile_rows, n_used, act, w2)


def _moe_combine_kernel(p1_ref, p2_ref, x_ref, r_ref, y_hbm, o_ref, ya_ref, yb_ref, sem_a, sem_b, *, tq):
    i = pl.program_id(0)
    slot = i % 2

    def start(step, s):
        _gather_start(p1_ref, step * tq, y_hbm, ya_ref.at[s], sem_a.at[s], tq)
        _gather_start(p2_ref, step * tq, y_hbm, yb_ref.at[s], sem_b.at[s], tq)

    @pl.when(i == 0)
    def _():
        start(0, 0)

    @pl.when(i + 1 < pl.num_programs(0))
    def _():
        start(i + 1, 1 - slot)

    _gather_wait(y_hbm, ya_ref.at[slot], sem_a.at[slot], tq)
    _gather_wait(y_hbm, yb_ref.at[slot], sem_b.at[slot], tq)
    r = r_ref[...]
    o_ref[...] = x_ref[...] + (r[:, 2:3] * ya_ref[slot] + r[:, 3:4] * yb_ref[slot])


def moe_combine(x, route, y, p1, p2, tq=256):
    m, d = x.shape
    tq = _pick(m, tq, 8)
    return pl.pallas_call(
        functools.partial(_moe_combine_kernel, tq=tq),
        grid_spec=pltpu.PrefetchScalarGridSpec(
            num_scalar_prefetch=2,
            grid=(m // tq,),
            in_specs=[pl.BlockSpec((tq, d), lambda i, a, b: (i, 0)),
                      pl.BlockSpec((tq, LANES), lambda i, a, b: (i, 0)),
                      pl.BlockSpec(memory_space=pl.ANY)],
            out_specs=pl.BlockSpec((tq, d), lambda i, a, b: (i, 0)),
            scratch_shapes=[pltpu.VMEM((2, tq, d), F32), pltpu.VMEM((2, tq, d), F32),
                            pltpu.SemaphoreType.DMA((2,)), pltpu.SemaphoreType.DMA((2,))]),
        out_shape=jax.ShapeDtypeStruct((m, d), F32),
        compiler_params=_params(("arbitrary",), 9 * 4 * tq * d),
        name="moe_combine",
    )(p1, p2, x, route, y)


def _pad_to(a, axis, size):
    if a.shape[axis] == size:
        return a
    pad = [(0, 0)] * a.ndim
    pad[axis] = (0, size - a.shape[axis])
    return jnp.pad(a, pad)


def _even_layer(x2d, batch, seq, tables, norm_mix, w_in, conv_w, q_norm, k_norm, pe_k, pe_v, w_cmp_k, w_cmp_v,
                w_out, norm_ffn, w1, w3, w2):
    d = x2d.shape[1]
    ch = d // 2
    heads = ch // HEAD_DIM
    hpg = heads // KV_HEADS
    kv_w = KV_HEADS * HEAD_DIM
    main = 3 * ch + heads * HEAD_DIM + 6 * kv_w
    assert main + heads * N_BRANCH == w_in.shape[1]

    h = rmsnorm(x2d, norm_mix, BF16)
    proj = matmul([(h, w_in.astype(BF16), 0)], None, F32, 1024, 1024, "in_proj", n=main)
    wg = w_in[:, main:].reshape(d, KV_HEADS, hpg * N_BRANCH)
    wg = _pad_to(wg, 2, LANES).reshape(d, KV_HEADS * LANES).astype(BF16)
    gates = matmul([(h, wg, 0)], None, F32, 1024, KV_HEADS * LANES, "gate_proj")

    y_a = short_conv(proj, conv_w.T, ch, seq)
    base = (3 * ch + heads * HEAD_DIM) // HEAD_DIM
    step = kv_w // HEAD_DIM
    blk = {"q": 3 * ch // HEAD_DIM, "kc": base, "vc": base + step, "ks": base + 2 * step,
           "vs": base + 3 * step, "kw": base + 4 * step, "vw": base + 5 * step}
    kn = k_norm.reshape(1, HEAD_DIM)
    kc, vc = compress(proj, blk["kc"], blk["vc"], pe_k, pe_v, w_cmp_k.astype(BF16), w_cmp_v.astype(BF16),
                      kn, batch, seq)
    y_b = nsa_attention(proj, gates, kc, vc, tables, q_norm.reshape(1, HEAD_DIM), kn, batch, seq, hpg, blk)

    wo = w_out.astype(BF16)
    x2d = matmul([(y_a, wo, 0), (y_b, wo, 1)], x2d, F32, 1024, 512, "out_proj")

    h2 = rmsnorm(x2d, norm_ffn, BF16)
    act = glu_matmul(h2, w1.astype(BF16), w3.astype(BF16), 1024, 256)
    return matmul_acc(act, w2.astype(BF16), x2d, 512, 1024, 5632, "dense_down")


def _odd_layer(x2d, seq, norm_mix, w_pool, pool_scale, norm_ffn, w_router, w1, w3, w2):
    m, d = x2d.shape
    x3, route = pool_and_route(x2d, norm_mix.reshape(1, d), w_pool.astype(BF16), pool_scale.reshape(1, d),
                               norm_ffn.reshape(1, d), w_router.T, seq)
    tm = min(512, m // 8)
    n_tiles = 2 * m // tm + N_EXPERTS
    tile_expert, tile_rows, next_expert, row_token, n_used, p1, p2 = _route_tables(route, tm, n_tiles)
    hs = moe_gather(x3, norm_ffn.reshape(1, d), row_token, n_used, tm)
    act = moe_glu(hs, w1, w3, tile_expert, tile_rows, n_used, next_expert, tm, 512)
    y = moe_down(act, w2.astype(BF16), tile_expert, tile_rows, n_used, tm, 1024)
    return moe_combine(x3, route, y, p1, p2)


def kernel(x, norm_mix_even, w_in_even, conv_w_even, q_norm_even, k_norm_even, cmp_pe_k_even, cmp_pe_v_even,
           w_cmp_k_even, w_cmp_v_even, w_out_even, norm_ffn_even, w1_dense, w3_dense, w2_dense,
           norm_mix_odd, w_pool_odd, pool_scale_odd, norm_ffn_odd, w_router_odd,
           w1_moe, w3_moe, w2_moe, rel_bias):
    batch, seq, d = x.shape
    depth = norm_mix_even.shape[0] + norm_mix_odd.shape[0]
    tables = bias_tables(rel_bias, seq)
    x2d = x.reshape(batch * seq, d)
    for layer in range(depth):
        i = layer // 2
        if layer % 2 == 0:
            x2d = _even_layer(x2d, batch, seq, tables, norm_mix_even[i], w_in_even[i], conv_w_even[i],
                              q_norm_even[i], k_norm_even[i], cmp_pe_k_even[i], cmp_pe_v_even[i],
                              w_cmp_k_even[i], w_cmp_v_even[i], w_out_even[i], norm_ffn_even[i],
                              w1_dense[i], w3_dense[i], w2_dense[i])
        else:
            x2d = _odd_layer(x2d, seq, norm_mix_odd[i], w_pool_odd[i], pool_scale_odd[i], norm_ffn_odd[i],
                             w_router_odd[i], w1_moe[i], w3_moe[i], w2_moe[i])
    return x2d.reshape(batch, seq, d)
```

```python
import functools
import math

import numpy as np
import jax
import jax.numpy as jnp
from jax import lax
from jax.experimental import pallas as pl
from jax.experimental.pallas import tpu as pltpu

F32 = jnp.float32
BF16 = jnp.bfloat16

HEAD_DIM = 128
KV_HEADS = 4
CONV_TAPS = 3
N_BRANCH = 3
CMP_BLOCK = 32
CMP_STRIDE = 16
SEL_BLOCK = 64
N_SELECT = 16
WINDOW = 512
FORCE_SCORE = 1e6
NEG_INF = -1e30
REL_BUCKETS = 32
REL_MAX_DIST = 128
POOL_WINDOWS = (2, 4, 8, 16)
N_EXPERTS = 8
EPS = 1e-6

LANES = 128
POOL_HALO = 16
CONV_HALO = 8
VMEM_CAP = 60 * 1024 * 1024
NT_DIMS = (((1,), (1,)), ((), ()))


def _pick(n, pref, mult=LANES):
    t = min(pref, n)
    while n % t or t % mult:
        t -= mult
    return t


def _params(sem, est_bytes):
    limit = int(min(max(est_bytes * 5 // 4 + (4 << 20), 16 << 20), VMEM_CAP))
    return pltpu.CompilerParams(dimension_semantics=sem, vmem_limit_bytes=limit)


def _rms(x, g):
    return x * lax.rsqrt(jnp.mean(x * x, axis=-1, keepdims=True) + EPS) * g


def _rmsnorm_kernel(x_ref, g_ref, o_ref):
    o_ref[...] = _rms(x_ref[...], g_ref[...]).astype(o_ref.dtype)


def rmsnorm(x, g, out_dtype, tm=256):
    m, d = x.shape
    tm = _pick(m, tm, 8)
    return pl.pallas_call(
        _rmsnorm_kernel,
        grid=(m // tm,),
        in_specs=[pl.BlockSpec((tm, d), lambda i: (i, 0)), pl.BlockSpec((1, d), lambda i: (0, 0))],
        out_specs=pl.BlockSpec((tm, d), lambda i: (i, 0)),
        out_shape=jax.ShapeDtypeStruct((m, d), out_dtype),
        compiler_params=_params(("parallel",), 2 * tm * d * 8),
        name="rmsnorm",
    )(x, g.reshape(1, d))


def _mm_kernel(*refs, n_pairs, has_res):
    o_ref = refs[-1]
    acc = None
    for p in range(n_pairs):
        d = jnp.dot(refs[2 * p][...], refs[2 * p + 1][...], preferred_element_type=F32)
        acc = d if acc is None else acc + d
    if has_res:
        acc = refs[2 * n_pairs][...] + acc
    o_ref[...] = acc.astype(o_ref.dtype)


def matmul(pairs, res, out_dtype, tm, tn, name, n=None):
    m = pairs[0][0].shape[0]
    n = pairs[0][1].shape[1] if n is None else n
    tm = _pick(m, tm)
    tn = _pick(n, tn)
    in_specs, args, est = [], [], 0
    for x, w, r in pairs:
        k = x.shape[1]
        in_specs += [pl.BlockSpec((tm, k), lambda i, j: (i, 0)), pl.BlockSpec((k, tn), lambda i, j, r=r: (r, j))]
        args += [x, w]
        est += 2 * 2 * (tm * k + k * tn)
    if res is not None:
        in_specs.append(pl.BlockSpec((tm, tn), lambda i, j: (i, j)))
        args.append(res)
        est += 2 * 4 * tm * tn
    est += 2 * 4 * tm * tn + 4 * tm * tn
    return pl.pallas_call(
        functools.partial(_mm_kernel, n_pairs=len(pairs), has_res=res is not None),
        grid=(m // tm, n // tn),
        in_specs=in_specs,
        out_specs=pl.BlockSpec((tm, tn), lambda i, j: (i, j)),
        out_shape=jax.ShapeDtypeStruct((m, n), out_dtype),
        compiler_params=_params(("parallel", "parallel"), est),
        name=name,
    )(*args)


def _glu_kernel(h_ref, w1_ref, w3_ref, o_ref):
    h = h_ref[...]
    a = jnp.dot(h, w1_ref[...].astype(BF16), preferred_element_type=F32)
    b = jnp.dot(h, w3_ref[...].astype(BF16), preferred_element_type=F32)
    o_ref[...] = (a * jax.nn.sigmoid(a) * b).astype(o_ref.dtype)


def glu_matmul(h, w1, w3, tm, tn):
    m, k = h.shape
    n = w1.shape[1]
    tm = _pick(m, tm)
    tn = _pick(n, tn)
    est = 2 * 2 * (tm * k + tm * tn) + 2 * 2 * 4 * k * tn + 2 * 2 * k * tn + 3 * 4 * tm * tn
    return pl.pallas_call(
        _glu_kernel,
        grid=(m // tm, n // tn),
        in_specs=[pl.BlockSpec((tm, k), lambda i, j: (i, 0)),
                  pl.BlockSpec((k, tn), lambda i, j: (0, j)),
                  pl.BlockSpec((k, tn), lambda i, j: (0, j))],
        out_specs=pl.BlockSpec((tm, tn), lambda i, j: (i, j)),
        out_shape=jax.ShapeDtypeStruct((m, n), BF16),
        compiler_params=_params(("parallel", "parallel"), est),
        name="dense_glu",
    )(h, w1, w3)


def _mm_acc_kernel(x_ref, w_ref, r_ref, o_ref, acc_ref):
    k = pl.program_id(2)

    @pl.when(k == 0)
    def _():
        acc_ref[...] = jnp.zeros_like(acc_ref)

    acc_ref[...] += jnp.dot(x_ref[...], w_ref[...], preferred_element_type=F32)

    @pl.when(k == pl.num_programs(2) - 1)
    def _():
        o_ref[...] = r_ref[...] + acc_ref[...]


def matmul_acc(x, w, res, tm, tn, tk, name):
    m, k = x.shape
    n = w.shape[1]
    tm, tn, tk = _pick(m, tm), _pick(n, tn), _pick(k, tk)
    est = 2 * 2 * (tm * tk + tk * tn) + 5 * 4 * tm * tn
    return pl.pallas_call(
        _mm_acc_kernel,
        grid=(m // tm, n // tn, k // tk),
        in_specs=[pl.BlockSpec((tm, tk), lambda i, j, kk: (i, kk)),
                  pl.BlockSpec((tk, tn), lambda i, j, kk: (kk, j)),
                  pl.BlockSpec((tm, tn), lambda i, j, kk: (i, j))],
        out_specs=pl.BlockSpec((tm, tn), lambda i, j, kk: (i, j)),
        out_shape=jax.ShapeDtypeStruct((m, n), F32),
        scratch_shapes=[pltpu.VMEM((tm, tn), F32)],
        compiler_params=_params(("parallel", "parallel", "arbitrary"), est),
        name=name,
    )(x, w, res)


def _conv_kernel(ab_ref, ac_ref, au_ref, hc_ref, hu_ref, w_ref, o_ref, *, tq, seq):
    i = pl.program_id(0)
    first = (i * tq) % seq == 0
    v = ac_ref[...] * au_ref[...]
    hv = jnp.where(first, 0.0, hc_ref[...] * hu_ref[...])
    rows = lax.broadcasted_iota(jnp.int32, v.shape, 0)
    v1 = jnp.where(rows == 0, hv[CONV_HALO - 1:CONV_HALO], pltpu.roll(v, 1, 0))
    v2 = jnp.where(rows == 0, hv[CONV_HALO - 2:CONV_HALO - 1],
                   jnp.where(rows == 1, hv[CONV_HALO - 1:CONV_HALO], pltpu.roll(v, 2, 0)))
    y = w_ref[0:1, :] * v2 + w_ref[1:2, :] * v1 + w_ref[2:3, :] * v
    o_ref[...] = (ab_ref[...] * y).astype(o_ref.dtype)


def short_conv(proj, conv_w_t, ch, seq, tq=512, tc=1024):
    m = proj.shape[0]
    tq = _pick(seq, tq, CONV_HALO)
    tc = _pick(ch, tc)
    nj = ch // tc
    hb = tq // CONV_HALO
    halo = lambda off: pl.BlockSpec((CONV_HALO, tc), lambda i, j: (jnp.maximum(i * hb - 1, 0), off + j))
    return pl.pallas_call(
        functools.partial(_conv_kernel, tq=tq, seq=seq),
        grid=(m // tq, nj),
        in_specs=[pl.BlockSpec((tq, tc), lambda i, j: (i, j)),
                  pl.BlockSpec((tq, tc), lambda i, j: (i, nj + j)),
                  pl.BlockSpec((tq, tc), lambda i, j: (i, 2 * nj + j)),
                  halo(nj), halo(2 * nj),
                  pl.BlockSpec((CONV_TAPS, tc), lambda i, j: (0, j))],
        out_specs=pl.BlockSpec((tq, tc), lambda i, j: (i, j)),
        out_shape=jax.ShapeDtypeStruct((m, ch), BF16),
        compiler_params=_params(("parallel", "parallel"), 2 * 4 * 4 * tq * tc + 6 * 4 * tq * tc),
        name="short_conv",
    )(proj, proj, proj, proj, proj, conv_w_t)


def _rel_bucket_np(dist):
    dist = np.maximum(dist, 0)
    exact = REL_BUCKETS // 2
    d = np.maximum(dist, exact).astype(np.float32)
    large = exact + (np.log(d / np.float32(exact)) / np.float32(math.log(REL_MAX_DIST / exact))
                     * np.float32(REL_BUCKETS - exact)).astype(np.int32)
    return np.where(dist < exact, dist, np.minimum(large, REL_BUCKETS - 1)).astype(np.int32)


WIN_TILES = WINDOW // LANES
N_BIAS_TILES = 2 * WIN_TILES + 1


def _bias_index_tables(seq):
    r = np.arange(LANES)[None, :]
    c = np.arange(LANES)[:, None]
    far = np.full((LANES, LANES), REL_BUCKETS - 1, np.int32)
    zero = np.zeros((LANES, LANES), np.float32)
    neg = np.full((LANES, LANES), NEG_INF, np.float32)
    assert _rel_bucket_np(np.arange(LANES + 1, 4 * seq)).min() == REL_BUCKETS - 1
    idx, add = [], []
    for d in range(-WIN_TILES, WIN_TILES + 1):
        if d < 0:
            idx.append(far), add.append(neg)
        elif d == 0:
            idx.append(_rel_bucket_np(r - c)), add.append(np.where(r >= c, zero, neg))
        elif d == 1:
            idx.append(_rel_bucket_np(LANES + r - c)), add.append(zero)
        elif d < WIN_TILES:
            idx.append(far), add.append(zero)
        else:
            idx.append(far), add.append(np.where(c > r, zero, neg))
    n_tile_rows = len(idx) * LANES
    for qt in range(seq // LANES):
        idx.append(_rel_bucket_np(qt * LANES + r - (c * CMP_STRIDE + CMP_BLOCK - 1)))
        add.append(zero)
    idx = np.concatenate(idx, axis=0).astype(np.int32)
    add = np.concatenate(add, axis=0).astype(np.float32)
    shift = (np.arange(idx.shape[0])[:, None] < n_tile_rows).astype(np.float32) * np.ones((1, LANES), np.float32)
    return idx, shift, add


def _bias_kernel(tbl_ref, idx_ref, shift_ref, add_ref, o_ref, *, hpg):
    g = pl.program_id(0)
    idx = idx_ref[...]
    for h in range(hpg):
        head = g * hpg + h
        acc = jnp.zeros(idx.shape, F32)
        for b in range(REL_BUCKETS):
            acc = jnp.where(idx == b, tbl_ref[b, head], acc)
        o_ref[0, :, h * LANES:(h + 1) * LANES] = (acc - shift_ref[...] * tbl_ref[REL_BUCKETS - 1, head]
                                                   + add_ref[...])


def bias_tables(rel_bias, seq):
    heads = rel_bias.shape[1]
    hpg = heads // KV_HEADS
    idx, shift, add = (jnp.asarray(a) for a in _bias_index_tables(seq))
    rows = idx.shape[0]
    full = pl.BlockSpec((rows, LANES), lambda g: (0, 0))
    return pl.pallas_call(
        functools.partial(_bias_kernel, hpg=hpg),
        grid=(KV_HEADS,),
        in_specs=[pl.BlockSpec(memory_space=pltpu.SMEM), full, full, full],
        out_specs=pl.BlockSpec((1, rows, hpg * LANES), lambda g: (g, 0, 0)),
        out_shape=jax.ShapeDtypeStruct((KV_HEADS, rows, hpg * LANES), F32),
        compiler_params=_params(("parallel",), (6 + 2 * hpg + 4) * 4 * rows * LANES),
        name="rel_bias_tables",
    )(rel_bias, idx, shift, add)


def _compress_kernel(k_ref, v_ref, pek_ref, pev_ref, wk_ref, wv_ref, kn_ref, kc_ref, vc_ref, *, nch):
    half = CMP_BLOCK // CMP_STRIDE
    assert half == 2

    def comp(x_ref, pe_ref, w_ref):
        lo = jnp.zeros((nch, HEAD_DIM), F32)
        hi = jnp.zeros((nch, HEAD_DIM), F32)
        for l in range(CMP_STRIDE):
            rows = x_ref[pl.ds(l, nch, stride=CMP_STRIDE), :]
            lo += jnp.dot((rows + pe_ref[l:l + 1, :]).astype(BF16), w_ref[l], preferred_element_type=F32)
            hi += jnp.dot((rows + pe_ref[CMP_STRIDE + l:CMP_STRIDE + l + 1, :]).astype(BF16),
                          w_ref[CMP_STRIDE + l], preferred_element_type=F32)
        return lo + pltpu.roll(hi, nch - 1, 0)

    kc_ref[0, 0] = _rms(comp(k_ref, pek_ref, wk_ref), kn_ref[...])
    vc_ref[0, 0] = comp(v_ref, pev_ref, wv_ref).T


def compress(proj, kc_blk, vc_blk, pe_k, pe_v, w_k, w_v, k_norm, batch, seq):
    nch = seq // CMP_STRIDE
    wspec = pl.BlockSpec((CMP_BLOCK, HEAD_DIM, HEAD_DIM), lambda b, g: (0, 0, 0))
    pespec = pl.BlockSpec((CMP_BLOCK, HEAD_DIM), lambda b, g: (0, 0))
    ospec = pl.BlockSpec((1, 1, nch, HEAD_DIM), lambda b, g: (b, g, 0, 0))
    oshape = jax.ShapeDtypeStruct((batch, KV_HEADS, nch, HEAD_DIM), F32)
    return pl.pallas_call(
        functools.partial(_compress_kernel, nch=nch),
        grid=(batch, KV_HEADS),
        in_specs=[pl.BlockSpec((seq, HEAD_DIM), lambda b, g: (b, kc_blk + g)),
                  pl.BlockSpec((seq, HEAD_DIM), lambda b, g: (b, vc_blk + g)),
                  pespec, pespec, wspec, wspec,
                  pl.BlockSpec((1, HEAD_DIM), lambda b, g: (0, 0))],
        out_specs=[ospec, ospec],
        out_shape=[oshape, oshape],
        compiler_params=_params(("parallel", "parallel"), 4 * 4 * seq * HEAD_DIM + (4 << 20)),
        name="nsa_compress",
    )(proj, proj, pe_k, pe_v, w_k, w_v, k_norm)


SEL_CHUNK = 1024


def _nsa_kernel(q_ref, ks_ref, vs_ref, kw_ref, vw_ref, kc_ref, vct_ref, wt_ref, bc_ref, gt_ref,
                qn_ref, kn_ref, cov_ref, oh_ref, o_ref, ksa, vst, kwn, vwt, *, hpg, n_cmp, n_top):
    qi = pl.program_id(2)
    cols_all = hpg * LANES
    scale = HEAD_DIM ** -0.5
    n_sel_blocks = cov_ref.shape[0]
    chunk_tiles = SEL_CHUNK // LANES
    seq = ks_ref.shape[0]

    @pl.when(qi == 0)
    def _():
        ksa[:, :HEAD_DIM] = _rms(ks_ref[...], kn_ref[...]).astype(BF16)
        ksa[:, HEAD_DIM:] = oh_ref[...]
        kwn[...] = _rms(kw_ref[...], kn_ref[...]).astype(BF16)
        for c in range(seq // LANES):
            blk = slice(c * LANES, (c + 1) * LANES)
            vst[:, blk] = vs_ref[blk, :].T.astype(BF16)
            vwt[:, blk] = vw_ref[blk, :].T.astype(BF16)

    qs = [_rms(q_ref[:, h * HEAD_DIM:(h + 1) * HEAD_DIM], qn_ref[...]).astype(BF16) for h in range(hpg)]
    q = jnp.concatenate(qs, axis=0) if hpg > 1 else qs[0]

    def bias_tile(d):
        off = pl.multiple_of((d + WIN_TILES) * LANES, LANES)
        return wt_ref[0, pl.ds(off, LANES), :]

    blk_row = lax.broadcasted_iota(jnp.int32, (LANES, cols_all), 0)
    qry = lax.broadcasted_iota(jnp.int32, (LANES, cols_all), 1) & (LANES - 1)

    lc = lax.dot_general(kc_ref[0, 0].astype(BF16), q, NT_DIMS, preferred_element_type=F32) * scale + bc_ref[0]
    mc = ((qi * LANES + qry - (blk_row * CMP_STRIDE + CMP_BLOCK - 1)) >= 0) & (blk_row < n_cmp)
    zc = jnp.where(mc, lc, NEG_INF)
    ec = jnp.exp(zc - jnp.max(zc, axis=0, keepdims=True))
    pc = ec * (1.0 / jnp.sum(ec, axis=0, keepdims=True)) * mc.astype(F32)
    o_cmp = jnp.dot(vct_ref[0, 0].astype(BF16), pc.astype(BF16), preferred_element_type=F32)

    ps = pc[:, 0:LANES]
    for h in range(1, hpg):
        ps = ps + pc[:, h * LANES:(h + 1) * LANES]
    p1 = ps.astype(BF16)
    r1 = ps - p1.astype(F32)
    p2 = r1.astype(BF16)
    p3 = (r1 - p2.astype(F32)).astype(BF16)
    cov = cov_ref[...]
    score = (jnp.dot(cov, p1, preferred_element_type=F32) + jnp.dot(cov, p2, preferred_element_type=F32)
             + jnp.dot(cov, p3, preferred_element_type=F32))
    jj = lax.broadcasted_iota(jnp.int32, (n_sel_blocks, LANES), 0)
    ql = lax.broadcasted_iota(jnp.int32, (n_sel_blocks, LANES), 1)
    cur = (LANES // SEL_BLOCK) * qi + ql // SEL_BLOCK
    forced = (jj == 0) | (jj == cur) | (jj == cur - 1)
    score = jnp.where(forced, FORCE_SCORE, jnp.where(jj > cur, -FORCE_SCORE, score))
    rank = jnp.zeros((n_sel_blocks, LANES), F32)
    for j2 in range(n_sel_blocks):
        other = score[j2:j2 + 1, :]
        rank += ((other > score) | ((other == score) & (j2 < jj))).astype(F32)
    sel_t = ((rank < n_top) & (jj <= cur)).astype(BF16)
    sel_t = jnp.concatenate([sel_t, jnp.ones((LANES - n_sel_blocks, LANES), BF16)], axis=0)
    eye = (lax.broadcasted_iota(jnp.int32, (LANES, LANES), 0)
           == lax.broadcasted_iota(jnp.int32, (LANES, LANES), 1)).astype(BF16)
    selq = lax.dot_general(eye, sel_t, NT_DIMS, preferred_element_type=F32)
    sel_neg = ((selq - 1.0) * (-NEG_INF)).astype(BF16)
    sel_neg = jnp.concatenate([sel_neg] * hpg, axis=0) if hpg > 1 else sel_neg
    qa = jnp.concatenate([q, sel_neg], axis=1)

    def sel_chunk(c, carry, near):
        m, l, acc = carry
        off = pl.multiple_of(c * SEL_CHUNK, SEL_CHUNK)
        s = lax.dot_general(ksa[pl.ds(off, SEL_CHUNK), :], qa, NT_DIMS, preferred_element_type=F32) * scale
        if near:
            s = jnp.concatenate([s[u * LANES:(u + 1) * LANES]
                                 + bias_tile(jnp.clip(qi - (c * chunk_tiles + u), 0, 2))
                                 for u in range(chunk_tiles)], axis=0)
        m_new = jnp.maximum(m, jnp.max(s, axis=0, keepdims=True))
        alpha = jnp.exp(m - m_new)
        p = jnp.exp(s - m_new)
        l = alpha * l + jnp.sum(p, axis=0, keepdims=True)
        acc = alpha * acc + jnp.dot(vst[:, pl.ds(off, SEL_CHUNK)], p.astype(BF16), preferred_element_type=F32)
        return m_new, l, acc

    init = (jnp.full((1, cols_all), NEG_INF, F32), jnp.zeros((1, cols_all), F32),
            jnp.zeros((HEAD_DIM, cols_all), F32))
    n_far = (jnp.maximum(qi, 1) - 1) // chunk_tiles
    carry = lax.fori_loop(0, n_far, functools.partial(sel_chunk, near=False), init)
    _, l_s, a_s = lax.fori_loop(n_far, qi // chunk_tiles + 1, functools.partial(sel_chunk, near=True), carry)

    win_keys = WINDOW + LANES
    kt0 = jnp.maximum(qi - WIN_TILES, 0)
    woff = pl.multiple_of(kt0 * LANES, LANES)
    sw = lax.dot_general(kwn[pl.ds(woff, win_keys), :], q, NT_DIMS, preferred_element_type=F32) * scale
    zw = jnp.concatenate([sw[u * LANES:(u + 1) * LANES] + bias_tile(qi - (kt0 + u))
                          for u in range(win_keys // LANES)], axis=0)
    pw = jnp.exp(zw - jnp.max(zw, axis=0, keepdims=True))
    l_w = jnp.sum(pw, axis=0, keepdims=True)
    a_w = jnp.dot(vwt[:, pl.ds(woff, win_keys)], pw.astype(BF16), preferred_element_type=F32)

    gt = jax.nn.sigmoid(gt_ref[...]).T
    inv_s = 1.0 / l_s
    inv_w = 1.0 / l_w
    for h in range(hpg):
        sl = slice(h * LANES, (h + 1) * LANES)
        c0 = h * N_BRANCH
        o = (gt[c0:c0 + 1] * o_cmp[:, sl] + (gt[c0 + 1:c0 + 2] * inv_s[:, sl]) * a_s[:, sl]
             + (gt[c0 + 2:c0 + 3] * inv_w[:, sl]) * a_w[:, sl])
        o_ref[:, h * HEAD_DIM:(h + 1) * HEAD_DIM] = o.T.astype(o_ref.dtype)


def _cover_t(seq):
    nc = seq // CMP_STRIDE - CMP_BLOCK // CMP_STRIDE + 1
    ns = seq // SEL_BLOCK
    c_start = np.arange(nc) * CMP_STRIDE
    c_end = c_start + CMP_BLOCK - 1
    s_start = np.arange(ns) * SEL_BLOCK
    cover = (c_start[:, None] <= s_start[None, :] + SEL_BLOCK - 1) & (c_end[:, None] >= s_start[None, :])
    out = np.zeros((ns, seq // CMP_STRIDE), np.float32)
    out[:, :nc] = cover.T
    return out


def nsa_attention(proj, gates, kc, vc, tables, q_norm, k_norm, batch, seq, hpg, blk):
    heads = KV_HEADS * hpg
    nq = seq // LANES
    nch = seq // CMP_STRIDE
    assert nch == LANES, "compressed keys must fill exactly one lane tile"
    n_cmp = nch - CMP_BLOCK // CMP_STRIDE + 1
    ns = seq // SEL_BLOCK
    assert seq % SEL_CHUNK == 0 and ns <= LANES
    cov = jnp.asarray(_cover_t(seq), BF16)
    onehot = jnp.asarray(np.arange(seq)[:, None] // SEL_BLOCK == np.arange(LANES)[None, :], BF16)
    qw = hpg * HEAD_DIM
    q_blk = blk["q"] * HEAD_DIM // qw
    kv = lambda name: pl.BlockSpec((seq, HEAD_DIM), lambda b, g, i: (b, blk[name] + g))
    cspec = pl.BlockSpec((1, 1, nch, HEAD_DIM), lambda b, g, i: (b, g, 0, 0))
    vec = pl.BlockSpec((1, HEAD_DIM), lambda b, g, i: (0, 0))
    rows_all = hpg * LANES
    est = (2 * 4 * 4 * seq * HEAD_DIM + 5 * 2 * seq * HEAD_DIM + 2 * 2 * seq * LANES
           + 2 * 4 * hpg * (N_BIAS_TILES + 1) * LANES * LANES + 6 * 4 * rows_all * (WINDOW + LANES) + (8 << 20))
    return pl.pallas_call(
        functools.partial(_nsa_kernel, hpg=hpg, n_cmp=n_cmp, n_top=min(N_SELECT, ns)),
        grid=(batch, KV_HEADS, nq),
        in_specs=[pl.BlockSpec((LANES, qw), lambda b, g, i: (b * nq + i, q_blk + g)),
                  kv("ks"), kv("vs"), kv("kw"), kv("vw"), cspec, cspec,
                  pl.BlockSpec((1, N_BIAS_TILES * LANES, qw), lambda b, g, i: (g, 0, 0)),
                  pl.BlockSpec((1, LANES, qw), lambda b, g, i: (g, N_BIAS_TILES + i, 0)),
                  pl.BlockSpec((LANES, LANES), lambda b, g, i: (b * nq + i, g)),
                  vec, vec,
                  pl.BlockSpec((ns, nch), lambda b, g, i: (0, 0)),
                  pl.BlockSpec((seq, LANES), lambda b, g, i: (0, 0))],
        out_specs=pl.BlockSpec((LANES, qw), lambda b, g, i: (b * nq + i, g)),
        out_shape=jax.ShapeDtypeStruct((batch * seq, heads * HEAD_DIM), BF16),
        scratch_shapes=[pltpu.VMEM((seq, 2 * HEAD_DIM), BF16), pltpu.VMEM((HEAD_DIM, seq), BF16),
                        pltpu.VMEM((seq, HEAD_DIM), BF16), pltpu.VMEM((HEAD_DIM, seq), BF16)],
        compiler_params=_params(("parallel", "parallel", "arbitrary"), est),
        name="nsa_attention",
    )(proj, proj, proj, proj, proj, kc, vc, tables, tables, gates, q_norm, k_norm, cov, onehot)


def _pool_kernel(x_ref, halo_ref, gm_ref, wp_ref, ps_ref, gf_ref, wr_ref, x_out, route_out, *, tq, seq, cg):
    i = pl.program_id(0)
    start = (i * tq) % seq
    x = x_ref[...]
    h = _rms(x, gm_ref[...])
    hh = jnp.where(start == 0, 0.0, _rms(halo_ref[...], gm_ref[...]))
    t1 = (start + lax.broadcasted_iota(jnp.int32, (tq, 1), 0) + 1).astype(F32)
    ys = []
    for gi, w in enumerate(POOL_WINDOWS):
        sl = slice(gi * cg, (gi + 1) * cg)
        s = jnp.concatenate([hh[:, sl], h[:, sl]], axis=0)
        span = 1
        while span < w:
            s = s + pltpu.roll(s, span, 0)
            span *= 2
        dm = s[POOL_HALO:] / jnp.minimum(t1, float(w)) - h[:, sl]
        ys.append(jnp.dot(dm.astype(BF16), wp_ref[gi], preferred_element_type=F32))
    x3 = x + jnp.concatenate(ys, axis=1) * ps_ref[...]
    x_out[...] = x3

    h4 = _rms(x3, gf_ref[...])
    lane = lax.broadcasted_iota(jnp.int32, (tq, LANES), 1)
    logits = jnp.full((tq, LANES), -jnp.inf, F32)
    for e in range(N_EXPERTS):
        logits = jnp.where(lane == e, jnp.sum(h4 * wr_ref[e:e + 1, :], axis=-1, keepdims=True), logits)
    m1 = jnp.max(logits, axis=-1, keepdims=True)
    i1 = jnp.min(jnp.where(logits == m1, lane, LANES), axis=-1, keepdims=True)
    rest = jnp.where(lane == i1, -jnp.inf, logits)
    m2 = jnp.max(rest, axis=-1, keepdims=True)
    i2 = jnp.min(jnp.where(rest == m2, lane, LANES), axis=-1, keepdims=True)
    e2 = jnp.exp(m2 - m1)
    den = 1.0 + e2
    route = jnp.where(lane == 0, i1.astype(F32),
                      jnp.where(lane == 1, i2.astype(F32),
                                jnp.where(lane == 2, 1.0 / den, jnp.where(lane == 3, e2 / den, 0.0))))
    route_out[...] = route


def pool_and_route(x, g_mix, w_pool, pool_scale, g_ffn, w_router_t, seq, tq=256):
    m, d = x.shape
    cg = d // len(POOL_WINDOWS)
    tq = _pick(seq, tq, POOL_HALO)
    hb = tq // POOL_HALO
    vec = pl.BlockSpec((1, d), lambda i: (0, 0))
    est = 2 * 2 * 4 * tq * d + 2 * 2 * len(POOL_WINDOWS) * cg * cg + 8 * 4 * tq * d
    return pl.pallas_call(
        functools.partial(_pool_kernel, tq=tq, seq=seq, cg=cg),
        grid=(m // tq,),
        in_specs=[pl.BlockSpec((tq, d), lambda i: (i, 0)),
                  pl.BlockSpec((POOL_HALO, d), lambda i: (jnp.maximum(i * hb - 1, 0), 0)),
                  vec,
                  pl.BlockSpec((len(POOL_WINDOWS), cg, cg), lambda i: (0, 0, 0)),
                  vec, vec,
                  pl.BlockSpec((N_EXPERTS, d), lambda i: (0, 0))],
        out_specs=[pl.BlockSpec((tq, d), lambda i: (i, 0)), pl.BlockSpec((tq, LANES), lambda i: (i, 0))],
        out_shape=[jax.ShapeDtypeStruct((m, d), F32), jax.ShapeDtypeStruct((m, LANES), F32)],
        compiler_params=_params(("parallel",), est),
        name="pool_mixer_router",
    )(x, x, g_mix, w_pool, pool_scale, g_ffn, w_router_t)


def _route_tables(route, tm, n_tiles):
    n = route.shape[0]
    i1 = route[:, 0].astype(jnp.int32)
    i2 = route[:, 1].astype(jnp.int32)
    onehot = jax.nn.one_hot(i1, N_EXPERTS, dtype=jnp.int32) + jax.nn.one_hot(i2, N_EXPERTS, dtype=jnp.int32)
    count = jnp.sum(onehot, axis=0)
    padded = (count + tm - 1) // tm * tm
    end = jnp.cumsum(padded)
    pos = (end - padded)[None, :] + jnp.cumsum(onehot, axis=0) - onehot
    p1 = jnp.take_along_axis(pos, i1[:, None], axis=1)[:, 0]
    p2 = jnp.take_along_axis(pos, i2[:, None], axis=1)[:, 0]
    tok = jnp.arange(n, dtype=jnp.int32)
    rows = n_tiles * tm
    row_token = jnp.zeros((rows,), jnp.int32).at[jnp.concatenate([p1, p2])].set(jnp.concatenate([tok, tok]))
    n_used = (end[-1] // tm).astype(jnp.int32)
    tile = jnp.arange(n_tiles, dtype=jnp.int32)
    tile_expert = jnp.searchsorted(end, tile * tm, side="right").astype(jnp.int32)
    tile_expert = jnp.where(tile < n_used, tile_expert, tile_expert[n_used - 1])
    tile_rows = jnp.clip((count - padded + end)[tile_expert] - tile * tm, 0, tm)
    tile_rows = jnp.where(tile < n_used, tile_rows, 0).astype(jnp.int32)
    ids = jnp.arange(N_EXPERTS, dtype=jnp.int32)
    later = jnp.where((ids[None, :] > ids[:, None]) & (count[None, :] > 0), ids[None, :], N_EXPERTS)
    following = jnp.min(later, axis=1)
    next_expert = jnp.where(following < N_EXPERTS, following, -1)[tile_expert].astype(jnp.int32)
    return (tile_expert, tile_rows, next_expert, row_token, n_used.reshape(1),
            p1.astype(jnp.int32), p2.astype(jnp.int32))


def _gather_start(idx_ref, base, src_hbm, dst_ref, sem, count):
    def issue(r, carry):
        pltpu.make_async_copy(src_hbm.at[pl.ds(idx_ref[base + r], 1), :], dst_ref.at[pl.ds(r, 1), :], sem).start()
        return carry

    lax.fori_loop(0, count, issue, 0, unroll=8)


def _gather_wait(src_hbm, dst_ref, sem, count):
    pltpu.make_async_copy(src_hbm.at[pl.ds(0, count), :], dst_ref, sem).wait()


MOE_SUB = 256


def _moe_gather_kernel(rt_ref, nu_ref, x_hbm, g_ref, o_ref, xg_ref, sem, *, tm):
    i = pl.program_id(0)
    valid = i < nu_ref[0]
    slot = i % 2

    @pl.when(i == 0)
    def _():
        _gather_start(rt_ref, 0, x_hbm, xg_ref.at[0], sem.at[0], tm)

    @pl.when(i + 1 < nu_ref[0])
    def _():
        _gather_start(rt_ref, (i + 1) * tm, x_hbm, xg_ref.at[1 - slot], sem.at[1 - slot], tm)

    @pl.when(valid)
    def _():
        _gather_wait(x_hbm, xg_ref.at[slot], sem.at[slot], tm)
        o_ref[...] = _rms(xg_ref[slot], g_ref[...]).astype(o_ref.dtype)

    @pl.when(jnp.logical_not(valid))
    def _():
        o_ref[...] = jnp.zeros_like(o_ref)


def moe_gather(x, g_ffn, row_token, n_used, tm):
    d = x.shape[1]
    n_tiles = row_token.shape[0] // tm
    return pl.pallas_call(
        functools.partial(_moe_gather_kernel, tm=tm),
        grid_spec=pltpu.PrefetchScalarGridSpec(
            num_scalar_prefetch=2,
            grid=(n_tiles,),
            in_specs=[pl.BlockSpec(memory_space=pl.ANY), pl.BlockSpec((1, d), lambda i, rt, nu: (0, 0))],
            out_specs=pl.BlockSpec((tm, d), lambda i, rt, nu: (i, 0)),
            scratch_shapes=[pltpu.VMEM((2, tm, d), F32), pltpu.SemaphoreType.DMA((2,))]),
        out_shape=jax.ShapeDtypeStruct((n_tiles * tm, d), BF16),
        compiler_params=_params(("arbitrary",), 2 * 4 * tm * d + 2 * 2 * tm * d + 2 * 4 * tm * d),
        name="moe_gather",
    )(row_token, n_used, x, g_ffn)


def _stage_expert_weights(te_ref, nu_ref, nx_ref, w_hbm, stage, work, sem, width):
    sweep = pl.program_id(0)
    i = pl.program_id(1)

    def copies(e, col_tile):
        cols = pl.ds(pl.multiple_of(col_tile * width, width), width)
        return [pltpu.make_async_copy(w.at[e, :, cols], s, sem.at[k]) for k, (w, s) in enumerate(zip(w_hbm, stage))]

    def start(e, col_tile):
        for c in copies(e, col_tile):
            c.start()

    @pl.when((sweep == 0) & (i == 0))
    def _():
        start(te_ref[0], 0)

    @pl.when((i < nu_ref[0]) & ((i == 0) | (te_ref[i] != te_ref[jnp.maximum(i - 1, 0)])))
    def _():
        for c in copies(0, 0):
            c.wait()

        def cast_rows(r, carry):
            rows = pl.ds(pl.multiple_of(r * MOE_SUB, MOE_SUB), MOE_SUB)
            for s, b in zip(stage, work):
                b[rows, :] = s[rows, :].astype(BF16)
            return carry

        lax.fori_loop(0, stage[0].shape[0] // MOE_SUB, cast_rows, 0)
        nxt = nx_ref[i]

        @pl.when(nxt >= 0)
        def _():
            start(nxt, sweep)

        @pl.when((nxt < 0) & (sweep + 1 < pl.num_programs(0)))
        def _():
            start(te_ref[0], sweep + 1)


def _moe_glu_kernel(te_ref, tr_ref, nu_ref, nx_ref, h_ref, w1_hbm, w3_hbm, o_ref, s1, s3, w1b, w3b, sem, *, tm, tf):
    i = pl.program_id(1)
    valid = i < nu_ref[0]
    _stage_expert_weights(te_ref, nu_ref, nx_ref, [w1_hbm, w3_hbm], [s1, s3], [w1b, w3b], sem, tf)

    for sb in range(tm // MOE_SUB):
        rows = pl.ds(sb * MOE_SUB, MOE_SUB)
        live = valid & (sb * MOE_SUB < tr_ref[i])

        @pl.when(live)
        def _():
            h = h_ref[rows, :]
            a = jnp.dot(h, w1b[...], preferred_element_type=F32)
            b = jnp.dot(h, w3b[...], preferred_element_type=F32)
            o_ref[rows, :] = (a * jax.nn.sigmoid(a) * b).astype(o_ref.dtype)

        @pl.when(jnp.logical_not(live))
        def _():
            o_ref[rows, :] = jnp.zeros((MOE_SUB, o_ref.shape[1]), o_ref.dtype)


def moe_glu(h, w1, w3, tile_expert, tile_rows, n_used, next_expert, tm, tf):
    d = h.shape[1]
    ff = w1.shape[2]
    tf = _pick(ff, tf)
    n_tiles = tile_expert.shape[0]
    est = 2 * 4 * d * tf + 2 * 2 * d * tf + 2 * 2 * tm * d + 2 * 2 * tm * tf + 4 * 4 * MOE_SUB * tf + 4 * d * tf
    return pl.pallas_call(
        functools.partial(_moe_glu_kernel, tm=tm, tf=tf),
        grid_spec=pltpu.PrefetchScalarGridSpec(
            num_scalar_prefetch=4,
            grid=(ff // tf, n_tiles),
            in_specs=[pl.BlockSpec((tm, d), lambda f, i, te, tr, nu, nx: (jnp.minimum(i, nu[0] - 1), 0)),
                      pl.BlockSpec(memory_space=pl.ANY), pl.BlockSpec(memory_space=pl.ANY)],
            out_specs=pl.BlockSpec((tm, tf), lambda f, i, te, tr, nu, nx: (i, f)),
            scratch_shapes=[pltpu.VMEM((d, tf), F32), pltpu.VMEM((d, tf), F32),
                            pltpu.VMEM((d, tf), BF16), pltpu.VMEM((d, tf), BF16),
                            pltpu.SemaphoreType.DMA((2,))]),
        out_shape=jax.ShapeDtypeStruct((n_tiles * tm, ff), BF16),
        compiler_params=_params(("arbitrary", "arbitrary"), est),
        name="moe_glu",
    )(tile_expert, tile_rows, n_used, next_expert, h, w1, w3)


def _moe_down_kernel(te_ref, tr_ref, nu_ref, nx_ref, a_ref, w_hbm, o_ref, s2, w2b, sem, *, tm, tn):
    i = pl.program_id(1)
    valid = i < nu_ref[0]
    _stage_expert_weights(te_ref, nu_ref, nx_ref, [w_hbm], [s2], [w2b], sem, tn)
    for sb in range(tm // MOE_SUB):
        rows = pl.ds(sb * MOE_SUB, MOE_SUB)
        live = valid & (sb * MOE_SUB < tr_ref[i])

        @pl.when(live)
        def _():
            o_ref[rows, :] = jnp.dot(a_ref[rows, :], w2b[...], preferred_element_type=F32)

        @pl.when(jnp.logical_not(live))
        def _():
            o_ref[rows, :] = jnp.zeros((MOE_SUB, o_ref.shape[1]), o_ref.dtype)


def moe_down(act, w2, tile_expert, tile_rows, n_used, next_expert, tm, tn):
    ff = act.shape[1]
    d = w2.shape[2]
    tn = _pick(d, tn)
    n_tiles = tile_expert.shape[0]
    est = 4 * ff * tn + 2 * ff * tn + 2 * 2 * tm * ff + 2 * 4 * tm * tn + 4 * MOE_SUB * tn + 4 * MOE_SUB * tn
    return pl.pallas_call(
        functools.partial(_moe_down_kernel, tm=tm, tn=tn),
        grid_spec=pltpu.PrefetchScalarGridSpec(
            num_scalar_prefetch=4,
            grid=(d // tn, n_tiles),
            in_specs=[pl.BlockSpec((tm, ff), lambda j, i, te, tr, nu, nx: (jnp.minimum(i, nu[0] - 1), 0)),
                      pl.BlockSpec(memory_space=pl.ANY)],
            out_specs=pl.BlockSpec((tm, tn), lambda j, i, te, tr, nu, nx: (i, j)),
            scratch_shapes=[pltpu.VMEM((ff, tn), F32), pltpu.VMEM((ff, tn), BF16),
                            pltpu.SemaphoreType.DMA((1,))]),
        out_shape=jax.ShapeDtypeStruct((n_tiles * tm, d), F32),
        compiler_params=_params(("arbitrary", "arbitrary"), est),
        name="moe_down",
    )(tile_expert, tile_rows, n_used, next_expert, act, w2)


def _moe_combine_kernel(p1_ref, p2_ref, x_ref, r_ref, y_hbm, o_ref, ya_ref, yb_ref, sem_a, sem_b, *, tq):
    i = pl.program_id(0)
    slot = i % 2

    def start(step, s):
        _gather_start(p1_ref, step * tq, y_hbm, ya_ref.at[s], sem_a.at[s], tq)
        _gather_start(p2_ref, step * tq, y_hbm, yb_ref.at[s], sem_b.at[s], tq)

    @pl.when(i == 0)
    def _():
        start(0, 0)

    @pl.when(i + 1 < pl.num_programs(0))
    def _():
        start(i + 1, 1 - slot)

    _gather_wait(y_hbm, ya_ref.at[slot], sem_a.at[slot], tq)
    _gather_wait(y_hbm, yb_ref.at[slot], sem_b.at[slot], tq)
    r = r_ref[...]
    o_ref[...] = x_ref[...] + (r[:, 2:3] * ya_ref[slot] + r[:, 3:4] * yb_ref[slot])


def moe_combine(x, route, y, p1, p2, tq=256):
    m, d = x.shape
    tq = _pick(m, tq, 8)
    return pl.pallas_call(
        functools.partial(_moe_combine_kernel, tq=tq),
        grid_spec=pltpu.PrefetchScalarGridSpec(
            num_scalar_prefetch=2,
            grid=(m // tq,),
            in_specs=[pl.BlockSpec((tq, d), lambda i, a, b: (i, 0)),
                      pl.BlockSpec((tq, LANES), lambda i, a, b: (i, 0)),
                      pl.BlockSpec(memory_space=pl.ANY)],
            out_specs=pl.BlockSpec((tq, d), lambda i, a, b: (i, 0)),
            scratch_shapes=[pltpu.VMEM((2, tq, d), F32), pltpu.VMEM((2, tq, d), F32),
                            pltpu.SemaphoreType.DMA((2,)), pltpu.SemaphoreType.DMA((2,))]),
        out_shape=jax.ShapeDtypeStruct((m, d), F32),
        compiler_params=_params(("arbitrary",), 9 * 4 * tq * d),
        name="moe_combine",
    )(p1, p2, x, route, y)


def _pad_to(a, axis, size):
    if a.shape[axis] == size:
        return a
    pad = [(0, 0)] * a.ndim
    pad[axis] = (0, size - a.shape[axis])
    return jnp.pad(a, pad)


def _even_layer(x2d, batch, seq, tables, norm_mix, w_in, conv_w, q_norm, k_norm, pe_k, pe_v, w_cmp_k, w_cmp_v,
                w_out, norm_ffn, w1, w3, w2):
    d = x2d.shape[1]
    ch = d // 2
    heads = ch // HEAD_DIM
    hpg = heads // KV_HEADS
    kv_w = KV_HEADS * HEAD_DIM
    main = 3 * ch + heads * HEAD_DIM + 6 * kv_w
    assert main + heads * N_BRANCH == w_in.shape[1]

    h = rmsnorm(x2d, norm_mix, BF16)
    proj = matmul([(h, w_in.astype(BF16), 0)], None, F32, 1024, 1024, "in_proj", n=main)
    wg = w_in[:, main:].reshape(d, KV_HEADS, hpg * N_BRANCH)
    wg = _pad_to(wg, 2, LANES).reshape(d, KV_HEADS * LANES).astype(BF16)
    gates = matmul([(h, wg, 0)], None, F32, 1024, KV_HEADS * LANES, "gate_proj")

    y_a = short_conv(proj, conv_w.T, ch, seq)
    base = (3 * ch + heads * HEAD_DIM) // HEAD_DIM
    step = kv_w // HEAD_DIM
    blk = {"q": 3 * ch // HEAD_DIM, "kc": base, "vc": base + step, "ks": base + 2 * step,
           "vs": base + 3 * step, "kw": base + 4 * step, "vw": base + 5 * step}
    kn = k_norm.reshape(1, HEAD_DIM)
    kc, vc = compress(proj, blk["kc"], blk["vc"], pe_k, pe_v, w_cmp_k.astype(BF16), w_cmp_v.astype(BF16),
                      kn, batch, seq)
    y_b = nsa_attention(proj, gates, kc, vc, tables, q_norm.reshape(1, HEAD_DIM), kn, batch, seq, hpg, blk)

    wo = w_out.astype(BF16)
    x2d = matmul([(y_a, wo, 0), (y_b, wo, 1)], x2d, F32, 1024, 512, "out_proj")

    h2 = rmsnorm(x2d, norm_ffn, BF16)
    act = glu_matmul(h2, w1, w3, 1024, 256)
    return matmul_acc(act, w2.astype(BF16), x2d, 512, 1024, 5632, "dense_down")


def _odd_layer(x2d, seq, norm_mix, w_pool, pool_scale, norm_ffn, w_router, w1, w3, w2):
    m, d = x2d.shape
    x3, route = pool_and_route(x2d, norm_mix.reshape(1, d), w_pool.astype(BF16), pool_scale.reshape(1, d),
                               norm_ffn.reshape(1, d), w_router.T, seq)
    tm = min(512, m // 8)
    n_tiles = 2 * m // tm + N_EXPERTS
    tile_expert, tile_rows, next_expert, row_token, n_used, p1, p2 = _route_tables(route, tm, n_tiles)
    hs = moe_gather(x3, norm_ffn.reshape(1, d), row_token, n_used, tm)
    act = moe_glu(hs, w1, w3, tile_expert, tile_rows, n_used, next_expert, tm, 512)
    y = moe_down(act, w2, tile_expert, tile_rows, n_used, next_expert, tm, 512)
    return moe_combine(x3, route, y, p1, p2)


def kernel(x, norm_mix_even, w_in_even, conv_w_even, q_norm_even, k_norm_even, cmp_pe_k_even, cmp_pe_v_even,
           w_cmp_k_even, w_cmp_v_even, w_out_even, norm_ffn_even, w1_dense, w3_dense, w2_dense,
           norm_mix_odd, w_pool_odd, pool_scale_odd, norm_ffn_odd, w_router_odd,
           w1_moe, w3_moe, w2_moe, rel_bias):
    batch, seq, d = x.shape
    depth = norm_mix_even.shape[0] + norm_mix_odd.shape[0]
    tables = bias_tables(rel_bias, seq)
    x2d = x.reshape(batch * seq, d)
    for layer in range(depth):
        i = layer // 2
        if layer % 2 == 0:
            x2d = _even_layer(x2d, batch, seq, tables, norm_mix_even[i], w_in_even[i], conv_w_even[i],
                              q_norm_even[i], k_norm_even[i], cmp_pe_k_even[i], cmp_pe_v_even[i],
                              w_cmp_k_even[i], w_cmp_v_even[i], w_out_even[i], norm_ffn_even[i],
                              w1_dense[i], w3_dense[i], w2_dense[i])
        else:
            x2d = _odd_layer(x2d, seq, norm_mix_odd[i], w_pool_odd[i], pool_scale_odd[i], norm_ffn_odd[i],
                             w_router_odd[i], w1_moe[i], w3_moe[i], w2_moe[i])
    return x2d.reshape(batch, seq, d)
```

```python
import functools
import math

import numpy as np
import jax
import jax.numpy as jnp
from jax import lax
from jax.experimental import pallas as pl
from jax.experimental.pallas import tpu as pltpu

F32 = jnp.float32
BF16 = jnp.bfloat16

HEAD_DIM = 128
KV_HEADS = 4
CONV_TAPS = 3
N_BRANCH = 3
CMP_BLOCK = 32
CMP_STRIDE = 16
SEL_BLOCK = 64
N_SELECT = 16
WINDOW = 512
FORCE_SCORE = 1e6
NEG_INF = -1e30
REL_BUCKETS = 32
REL_MAX_DIST = 128
POOL_WINDOWS = (2, 4, 8, 16)
N_EXPERTS = 8
EPS = 1e-6

LANES = 128
POOL_HALO = 16
CONV_HALO = 8
VMEM_CAP = 60 * 1024 * 1024
NT_DIMS = (((1,), (1,)), ((), ()))


def _pick(n, pref, mult=LANES):
    t = min(pref, n)
    while n % t or t % mult:
        t -= mult
    return t


def _params(sem, est_bytes):
    limit = int(min(max(est_bytes * 5 // 4 + (4 << 20), 16 << 20), VMEM_CAP))
    return pltpu.CompilerParams(dimension_semantics=sem, vmem_limit_bytes=limit)


def _rms(x, g):
    return x * lax.rsqrt(jnp.mean(x * x, axis=-1, keepdims=True) + EPS) * g


def _rmsnorm_kernel(x_ref, g_ref, o_ref):
    o_ref[...] = _rms(x_ref[...], g_ref[...]).astype(o_ref.dtype)


def rmsnorm(x, g, out_dtype, tm=256):
    m, d = x.shape
    tm = _pick(m, tm, 8)
    return pl.pallas_call(
        _rmsnorm_kernel,
        grid=(m // tm,),
        in_specs=[pl.BlockSpec((tm, d), lambda i: (i, 0)), pl.BlockSpec((1, d), lambda i: (0, 0))],
        out_specs=pl.BlockSpec((tm, d), lambda i: (i, 0)),
        out_shape=jax.ShapeDtypeStruct((m, d), out_dtype),
        compiler_params=_params(("parallel",), 2 * tm * d * 8),
        name="rmsnorm",
    )(x, g.reshape(1, d))


def _mm_kernel(*refs, n_pairs, has_res):
    o_ref = refs[-1]
    acc = None
    for p in range(n_pairs):
        d = jnp.dot(refs[2 * p][...], refs[2 * p + 1][...].astype(BF16), preferred_element_type=F32)
        acc = d if acc is None else acc + d
    if has_res:
        acc = refs[2 * n_pairs][...] + acc
    o_ref[...] = acc.astype(o_ref.dtype)


def matmul(pairs, res, out_dtype, tm, tn, name, n=None):
    m = pairs[0][0].shape[0]
    n = pairs[0][1].shape[1] if n is None else n
    tm = _pick(m, tm)
    tn = _pick(n, tn)
    in_specs, args, est = [], [], 0
    for x, w, r in pairs:
        k = x.shape[1]
        in_specs += [pl.BlockSpec((tm, k), lambda i, j: (i, 0)), pl.BlockSpec((k, tn), lambda i, j, r=r: (r, j))]
        args += [x, w]
        est += 2 * 2 * tm * k + (2 * w.dtype.itemsize + 2) * k * tn
    if res is not None:
        in_specs.append(pl.BlockSpec((tm, tn), lambda i, j: (i, j)))
        args.append(res)
        est += 2 * 4 * tm * tn
    est += 2 * 4 * tm * tn + 4 * tm * tn
    return pl.pallas_call(
        functools.partial(_mm_kernel, n_pairs=len(pairs), has_res=res is not None),
        grid=(m // tm, n // tn),
        in_specs=in_specs,
        out_specs=pl.BlockSpec((tm, tn), lambda i, j: (i, j)),
        out_shape=jax.ShapeDtypeStruct((m, n), out_dtype),
        compiler_params=_params(("parallel", "parallel"), est),
        name=name,
    )(*args)


def _glu_kernel(h_ref, w1_ref, w3_ref, o_ref):
    h = h_ref[...]
    a = jnp.dot(h, w1_ref[...].astype(BF16), preferred_element_type=F32)
    b = jnp.dot(h, w3_ref[...].astype(BF16), preferred_element_type=F32)
    o_ref[...] = (a * jax.nn.sigmoid(a) * b).astype(o_ref.dtype)


def glu_matmul(h, w1, w3, tm, tn):
    m, k = h.shape
    n = w1.shape[1]
    tm = _pick(m, tm)
    tn = _pick(n, tn)
    est = 2 * 2 * (tm * k + tm * tn) + 2 * 2 * 4 * k * tn + 2 * 2 * k * tn + 3 * 4 * tm * tn
    return pl.pallas_call(
        _glu_kernel,
        grid=(m // tm, n // tn),
        in_specs=[pl.BlockSpec((tm, k), lambda i, j: (i, 0)),
                  pl.BlockSpec((k, tn), lambda i, j: (0, j)),
                  pl.BlockSpec((k, tn), lambda i, j: (0, j))],
        out_specs=pl.BlockSpec((tm, tn), lambda i, j: (i, j)),
        out_shape=jax.ShapeDtypeStruct((m, n), BF16),
        compiler_params=_params(("parallel", "parallel"), est),
        name="dense_glu",
    )(h, w1, w3)


def _mm_acc_kernel(x_ref, w_ref, r_ref, o_ref, acc_ref):
    k = pl.program_id(2)

    @pl.when(k == 0)
    def _():
        acc_ref[...] = jnp.zeros_like(acc_ref)

    acc_ref[...] += jnp.dot(x_ref[...], w_ref[...], preferred_element_type=F32)

    @pl.when(k == pl.num_programs(2) - 1)
    def _():
        o_ref[...] = r_ref[...] + acc_ref[...]


def matmul_acc(x, w, res, tm, tn, tk, name):
    m, k = x.shape
    n = w.shape[1]
    tm, tn, tk = _pick(m, tm), _pick(n, tn), _pick(k, tk)
    est = 2 * 2 * (tm * tk + tk * tn) + 5 * 4 * tm * tn
    return pl.pallas_call(
        _mm_acc_kernel,
        grid=(m // tm, n // tn, k // tk),
        in_specs=[pl.BlockSpec((tm, tk), lambda i, j, kk: (i, kk)),
                  pl.BlockSpec((tk, tn), lambda i, j, kk: (kk, j)),
                  pl.BlockSpec((tm, tn), lambda i, j, kk: (i, j))],
        out_specs=pl.BlockSpec((tm, tn), lambda i, j, kk: (i, j)),
        out_shape=jax.ShapeDtypeStruct((m, n), F32),
        scratch_shapes=[pltpu.VMEM((tm, tn), F32)],
        compiler_params=_params(("parallel", "parallel", "arbitrary"), est),
        name=name,
    )(x, w, res)


def _conv_kernel(ab_ref, ac_ref, au_ref, hc_ref, hu_ref, w_ref, o_ref, *, tq, seq):
    i = pl.program_id(0)
    first = (i * tq) % seq == 0
    v = ac_ref[...] * au_ref[...]
    hv = jnp.where(first, 0.0, hc_ref[...] * hu_ref[...])
    rows = lax.broadcasted_iota(jnp.int32, v.shape, 0)
    v1 = jnp.where(rows == 0, hv[CONV_HALO - 1:CONV_HALO], pltpu.roll(v, 1, 0))
    v2 = jnp.where(rows == 0, hv[CONV_HALO - 2:CONV_HALO - 1],
                   jnp.where(rows == 1, hv[CONV_HALO - 1:CONV_HALO], pltpu.roll(v, 2, 0)))
    y = w_ref[0:1, :] * v2 + w_ref[1:2, :] * v1 + w_ref[2:3, :] * v
    o_ref[...] = (ab_ref[...] * y).astype(o_ref.dtype)


def short_conv(proj, conv_w_t, ch, seq, tq=512, tc=1024):
    m = proj.shape[0]
    tq = _pick(seq, tq, CONV_HALO)
    tc = _pick(ch, tc)
    nj = ch // tc
    hb = tq // CONV_HALO
    halo = lambda off: pl.BlockSpec((CONV_HALO, tc), lambda i, j: (jnp.maximum(i * hb - 1, 0), off + j))
    return pl.pallas_call(
        functools.partial(_conv_kernel, tq=tq, seq=seq),
        grid=(m // tq, nj),
        in_specs=[pl.BlockSpec((tq, tc), lambda i, j: (i, j)),
                  pl.BlockSpec((tq, tc), lambda i, j: (i, nj + j)),
                  pl.BlockSpec((tq, tc), lambda i, j: (i, 2 * nj + j)),
                  halo(nj), halo(2 * nj),
                  pl.BlockSpec((CONV_TAPS, tc), lambda i, j: (0, j))],
        out_specs=pl.BlockSpec((tq, tc), lambda i, j: (i, j)),
        out_shape=jax.ShapeDtypeStruct((m, ch), BF16),
        compiler_params=_params(("parallel", "parallel"), 2 * 4 * 4 * tq * tc + 6 * 4 * tq * tc),
        name="short_conv",
    )(proj, proj, proj, proj, proj, conv_w_t)


def _rel_bucket_np(dist):
    dist = np.maximum(dist, 0)
    exact = REL_BUCKETS // 2
    d = np.maximum(dist, exact).astype(np.float32)
    large = exact + (np.log(d / np.float32(exact)) / np.float32(math.log(REL_MAX_DIST / exact))
                     * np.float32(REL_BUCKETS - exact)).astype(np.int32)
    return np.where(dist < exact, dist, np.minimum(large, REL_BUCKETS - 1)).astype(np.int32)


WIN_TILES = WINDOW // LANES
N_BIAS_TILES = 2 * WIN_TILES + 1


def _bias_index_tables(seq):
    r = np.arange(LANES)[None, :]
    c = np.arange(LANES)[:, None]
    far = np.full((LANES, LANES), REL_BUCKETS - 1, np.int32)
    zero = np.zeros((LANES, LANES), np.float32)
    neg = np.full((LANES, LANES), NEG_INF, np.float32)
    assert _rel_bucket_np(np.arange(LANES + 1, 4 * seq)).min() == REL_BUCKETS - 1
    idx, add = [], []
    for d in range(-WIN_TILES, WIN_TILES + 1):
        if d < 0:
            idx.append(far), add.append(neg)
        elif d == 0:
            idx.append(_rel_bucket_np(r - c)), add.append(np.where(r >= c, zero, neg))
        elif d == 1:
            idx.append(_rel_bucket_np(LANES + r - c)), add.append(zero)
        elif d < WIN_TILES:
            idx.append(far), add.append(zero)
        else:
            idx.append(far), add.append(np.where(c > r, zero, neg))
    n_tile_rows = len(idx) * LANES
    for qt in range(seq // LANES):
        idx.append(_rel_bucket_np(qt * LANES + r - (c * CMP_STRIDE + CMP_BLOCK - 1)))
        add.append(zero)
    idx = np.concatenate(idx, axis=0).astype(np.int32)
    add = np.concatenate(add, axis=0).astype(np.float32)
    shift = (np.arange(idx.shape[0])[:, None] < n_tile_rows).astype(np.float32) * np.ones((1, LANES), np.float32)
    return idx, shift, add


def _bias_kernel(tbl_ref, idx_ref, shift_ref, add_ref, o_ref, *, hpg):
    g = pl.program_id(0)
    idx = idx_ref[...]
    for h in range(hpg):
        head = g * hpg + h
        acc = jnp.zeros(idx.shape, F32)
        for b in range(REL_BUCKETS):
            acc = jnp.where(idx == b, tbl_ref[b, head], acc)
        o_ref[0, :, h * LANES:(h + 1) * LANES] = (acc - shift_ref[...] * tbl_ref[REL_BUCKETS - 1, head]
                                                   + add_ref[...])


def bias_tables(rel_bias, seq):
    heads = rel_bias.shape[1]
    hpg = heads // KV_HEADS
    idx, shift, add = (jnp.asarray(a) for a in _bias_index_tables(seq))
    rows = idx.shape[0]
    full = pl.BlockSpec((rows, LANES), lambda g: (0, 0))
    return pl.pallas_call(
        functools.partial(_bias_kernel, hpg=hpg),
        grid=(KV_HEADS,),
        in_specs=[pl.BlockSpec(memory_space=pltpu.SMEM), full, full, full],
        out_specs=pl.BlockSpec((1, rows, hpg * LANES), lambda g: (g, 0, 0)),
        out_shape=jax.ShapeDtypeStruct((KV_HEADS, rows, hpg * LANES), F32),
        compiler_params=_params(("parallel",), (6 + 2 * hpg + 4) * 4 * rows * LANES),
        name="rel_bias_tables",
    )(rel_bias, idx, shift, add)


def _compress_kernel(k_ref, v_ref, pek_ref, pev_ref, wk_ref, wv_ref, kn_ref, kc_ref, vc_ref, *, nch):
    half = CMP_BLOCK // CMP_STRIDE
    assert half == 2

    def comp(x_ref, pe_ref, w_ref):
        lo = jnp.zeros((nch, HEAD_DIM), F32)
        hi = jnp.zeros((nch, HEAD_DIM), F32)
        for l in range(CMP_STRIDE):
            rows = x_ref[pl.ds(l, nch, stride=CMP_STRIDE), :]
            lo += jnp.dot((rows + pe_ref[l:l + 1, :]).astype(BF16), w_ref[l], preferred_element_type=F32)
            hi += jnp.dot((rows + pe_ref[CMP_STRIDE + l:CMP_STRIDE + l + 1, :]).astype(BF16),
                          w_ref[CMP_STRIDE + l], preferred_element_type=F32)
        return lo + pltpu.roll(hi, nch - 1, 0)

    kc_ref[0, 0] = _rms(comp(k_ref, pek_ref, wk_ref), kn_ref[...])
    vc_ref[0, 0] = comp(v_ref, pev_ref, wv_ref).T


def compress(proj, kc_blk, vc_blk, pe_k, pe_v, w_k, w_v, k_norm, batch, seq):
    nch = seq // CMP_STRIDE
    wspec = pl.BlockSpec((CMP_BLOCK, HEAD_DIM, HEAD_DIM), lambda b, g: (0, 0, 0))
    pespec = pl.BlockSpec((CMP_BLOCK, HEAD_DIM), lambda b, g: (0, 0))
    ospec = pl.BlockSpec((1, 1, nch, HEAD_DIM), lambda b, g: (b, g, 0, 0))
    oshape = jax.ShapeDtypeStruct((batch, KV_HEADS, nch, HEAD_DIM), F32)
    return pl.pallas_call(
        functools.partial(_compress_kernel, nch=nch),
        grid=(batch, KV_HEADS),
        in_specs=[pl.BlockSpec((seq, HEAD_DIM), lambda b, g: (b, kc_blk + g)),
                  pl.BlockSpec((seq, HEAD_DIM), lambda b, g: (b, vc_blk + g)),
                  pespec, pespec, wspec, wspec,
                  pl.BlockSpec((1, HEAD_DIM), lambda b, g: (0, 0))],
        out_specs=[ospec, ospec],
        out_shape=[oshape, oshape],
        compiler_params=_params(("parallel", "parallel"), 4 * 4 * seq * HEAD_DIM + (4 << 20)),
        name="nsa_compress",
    )(proj, proj, pe_k, pe_v, w_k, w_v, k_norm)


SEL_CHUNK = 1024


def _nsa_kernel(q_ref, ks_ref, vs_ref, kw_ref, vw_ref, kc_ref, vct_ref, wt_ref, bc_ref, gt_ref,
                qn_ref, kn_ref, cov_ref, oh_ref, o_ref, ksa, vst, kwn, vwt, *, hpg, n_cmp, n_top):
    qi = pl.program_id(2)
    cols_all = hpg * LANES
    scale = HEAD_DIM ** -0.5
    n_sel_blocks = cov_ref.shape[0]
    chunk_tiles = SEL_CHUNK // LANES
    seq = ks_ref.shape[0]

    @pl.when(qi == 0)
    def _():
        ksa[:, :HEAD_DIM] = _rms(ks_ref[...], kn_ref[...]).astype(BF16)
        ksa[:, HEAD_DIM:] = oh_ref[...]
        kwn[...] = _rms(kw_ref[...], kn_ref[...]).astype(BF16)
        for c in range(seq // LANES):
            blk = slice(c * LANES, (c + 1) * LANES)
            vst[:, blk] = vs_ref[blk, :].T.astype(BF16)
            vwt[:, blk] = vw_ref[blk, :].T.astype(BF16)

    qs = [_rms(q_ref[:, h * HEAD_DIM:(h + 1) * HEAD_DIM], qn_ref[...]).astype(BF16) for h in range(hpg)]
    q = jnp.concatenate(qs, axis=0) if hpg > 1 else qs[0]

    def bias_tile(d):
        off = pl.multiple_of((d + WIN_TILES) * LANES, LANES)
        return wt_ref[0, pl.ds(off, LANES), :]

    blk_row = lax.broadcasted_iota(jnp.int32, (LANES, cols_all), 0)
    qry = lax.broadcasted_iota(jnp.int32, (LANES, cols_all), 1) & (LANES - 1)

    lc = lax.dot_general(kc_ref[0, 0].astype(BF16), q, NT_DIMS, preferred_element_type=F32) * scale + bc_ref[0]
    mc = ((qi * LANES + qry - (blk_row * CMP_STRIDE + CMP_BLOCK - 1)) >= 0) & (blk_row < n_cmp)
    zc = jnp.where(mc, lc, NEG_INF)
    ec = jnp.exp(zc - jnp.max(zc, axis=0, keepdims=True))
    pc = ec * (1.0 / jnp.sum(ec, axis=0, keepdims=True)) * mc.astype(F32)
    o_cmp = jnp.dot(vct_ref[0, 0].astype(BF16), pc.astype(BF16), preferred_element_type=F32)

    ps = pc[:, 0:LANES]
    for h in range(1, hpg):
        ps = ps + pc[:, h * LANES:(h + 1) * LANES]
    p1 = ps.astype(BF16)
    r1 = ps - p1.astype(F32)
    p2 = r1.astype(BF16)
    p3 = (r1 - p2.astype(F32)).astype(BF16)
    cov = cov_ref[...]
    score = (jnp.dot(cov, p1, preferred_element_type=F32) + jnp.dot(cov, p2, preferred_element_type=F32)
             + jnp.dot(cov, p3, preferred_element_type=F32))
    jj = lax.broadcasted_iota(jnp.int32, (n_sel_blocks, LANES), 0)
    ql = lax.broadcasted_iota(jnp.int32, (n_sel_blocks, LANES), 1)
    cur = (LANES // SEL_BLOCK) * qi + ql // SEL_BLOCK
    forced = (jj == 0) | (jj == cur) | (jj == cur - 1)
    score = jnp.where(forced, FORCE_SCORE, jnp.where(jj > cur, -FORCE_SCORE, score))
    rank = jnp.zeros((n_sel_blocks, LANES), F32)
    for j2 in range(n_sel_blocks):
        other = score[j2:j2 + 1, :]
        rank += ((other > score) | ((other == score) & (j2 < jj))).astype(F32)
    sel_t = ((rank < n_top) & (jj <= cur)).astype(BF16)
    sel_t = jnp.concatenate([sel_t, jnp.ones((LANES - n_sel_blocks, LANES), BF16)], axis=0)
    eye = (lax.broadcasted_iota(jnp.int32, (LANES, LANES), 0)
           == lax.broadcasted_iota(jnp.int32, (LANES, LANES), 1)).astype(BF16)
    selq = lax.dot_general(eye, sel_t, NT_DIMS, preferred_element_type=F32)
    sel_neg = ((selq - 1.0) * (-NEG_INF)).astype(BF16)
    sel_neg = jnp.concatenate([sel_neg] * hpg, axis=0) if hpg > 1 else sel_neg
    qa = jnp.concatenate([q, sel_neg], axis=1)

    def sel_chunk(c, carry, near):
        m, l, acc = carry
        off = pl.multiple_of(c * SEL_CHUNK, SEL_CHUNK)
        s = lax.dot_general(ksa[pl.ds(off, SEL_CHUNK), :], qa, NT_DIMS, preferred_element_type=F32) * scale
        if near:
            s = jnp.concatenate([s[u * LANES:(u + 1) * LANES]
                                 + bias_tile(jnp.clip(qi - (c * chunk_tiles + u), 0, 2))
                                 for u in range(chunk_tiles)], axis=0)
        m_new = jnp.maximum(m, jnp.max(s, axis=0, keepdims=True))
        alpha = jnp.exp(m - m_new)
        p = jnp.exp(s - m_new)
        l = alpha * l + jnp.sum(p, axis=0, keepdims=True)
        acc = alpha * acc + jnp.dot(vst[:, pl.ds(off, SEL_CHUNK)], p.astype(BF16), preferred_element_type=F32)
        return m_new, l, acc

    init = (jnp.full((1, cols_all), NEG_INF, F32), jnp.zeros((1, cols_all), F32),
            jnp.zeros((HEAD_DIM, cols_all), F32))
    n_far = (jnp.maximum(qi, 1) - 1) // chunk_tiles
    carry = lax.fori_loop(0, n_far, functools.partial(sel_chunk, near=False), init)
    _, l_s, a_s = lax.fori_loop(n_far, qi // chunk_tiles + 1, functools.partial(sel_chunk, near=True), carry)

    win_keys = WINDOW + LANES
    kt0 = jnp.maximum(qi - WIN_TILES, 0)
    woff = pl.multiple_of(kt0 * LANES, LANES)
    sw = lax.dot_general(kwn[pl.ds(woff, win_keys), :], q, NT_DIMS, preferred_element_type=F32) * scale
    zw = jnp.concatenate([sw[u * LANES:(u + 1) * LANES] + bias_tile(qi - (kt0 + u))
                          for u in range(win_keys // LANES)], axis=0)
    pw = jnp.exp(zw - jnp.max(zw, axis=0, keepdims=True))
    l_w = jnp.sum(pw, axis=0, keepdims=True)
    a_w = jnp.dot(vwt[:, pl.ds(woff, win_keys)], pw.astype(BF16), preferred_element_type=F32)

    gt = jax.nn.sigmoid(gt_ref[...]).T
    inv_s = 1.0 / l_s
    inv_w = 1.0 / l_w
    for h in range(hpg):
        sl = slice(h * LANES, (h + 1) * LANES)
        c0 = h * N_BRANCH
        o = (gt[c0:c0 + 1] * o_cmp[:, sl] + (gt[c0 + 1:c0 + 2] * inv_s[:, sl]) * a_s[:, sl]
             + (gt[c0 + 2:c0 + 3] * inv_w[:, sl]) * a_w[:, sl])
        o_ref[:, h * HEAD_DIM:(h + 1) * HEAD_DIM] = o.T.astype(o_ref.dtype)


def _cover_t(seq):
    nc = seq // CMP_STRIDE - CMP_BLOCK // CMP_STRIDE + 1
    ns = seq // SEL_BLOCK
    c_start = np.arange(nc) * CMP_STRIDE
    c_end = c_start + CMP_BLOCK - 1
    s_start = np.arange(ns) * SEL_BLOCK
    cover = (c_start[:, None] <= s_start[None, :] + SEL_BLOCK - 1) & (c_end[:, None] >= s_start[None, :])
    out = np.zeros((ns, seq // CMP_STRIDE), np.float32)
    out[:, :nc] = cover.T
    return out


def nsa_attention(proj, gates, kc, vc, tables, q_norm, k_norm, batch, seq, hpg, blk):
    heads = KV_HEADS * hpg
    nq = seq // LANES
    nch = seq // CMP_STRIDE
    assert nch == LANES, "compressed keys must fill exactly one lane tile"
    n_cmp = nch - CMP_BLOCK // CMP_STRIDE + 1
    ns = seq // SEL_BLOCK
    assert seq % SEL_CHUNK == 0 and ns <= LANES
    cov = jnp.asarray(_cover_t(seq), BF16)
    onehot = jnp.asarray(np.arange(seq)[:, None] // SEL_BLOCK == np.arange(LANES)[None, :], BF16)
    qw = hpg * HEAD_DIM
    q_blk = blk["q"] * HEAD_DIM // qw
    kv = lambda name: pl.BlockSpec((seq, HEAD_DIM), lambda b, g, i: (b, blk[name] + g))
    cspec = pl.BlockSpec((1, 1, nch, HEAD_DIM), lambda b, g, i: (b, g, 0, 0))
    vec = pl.BlockSpec((1, HEAD_DIM), lambda b, g, i: (0, 0))
    rows_all = hpg * LANES
    est = (2 * 4 * 4 * seq * HEAD_DIM + 5 * 2 * seq * HEAD_DIM + 2 * 2 * seq * LANES
           + 2 * 4 * hpg * (N_BIAS_TILES + 1) * LANES * LANES + 6 * 4 * rows_all * (WINDOW + LANES) + (8 << 20))
    return pl.pallas_call(
        functools.partial(_nsa_kernel, hpg=hpg, n_cmp=n_cmp, n_top=min(N_SELECT, ns)),
        grid=(batch, KV_HEADS, nq),
        in_specs=[pl.BlockSpec((LANES, qw), lambda b, g, i: (b * nq + i, q_blk + g)),
                  kv("ks"), kv("vs"), kv("kw"), kv("vw"), cspec, cspec,
                  pl.BlockSpec((1, N_BIAS_TILES * LANES, qw), lambda b, g, i: (g, 0, 0)),
                  pl.BlockSpec((1, LANES, qw), lambda b, g, i: (g, N_BIAS_TILES + i, 0)),
                  pl.BlockSpec((LANES, LANES), lambda b, g, i: (b * nq + i, g)),
                  vec, vec,
                  pl.BlockSpec((ns, nch), lambda b, g, i: (0, 0)),
                  pl.BlockSpec((seq, LANES), lambda b, g, i: (0, 0))],
        out_specs=pl.BlockSpec((LANES, qw), lambda b, g, i: (b * nq + i, g)),
        out_shape=jax.ShapeDtypeStruct((batch * seq, heads * HEAD_DIM), BF16),
        scratch_shapes=[pltpu.VMEM((seq, 2 * HEAD_DIM), BF16), pltpu.VMEM((HEAD_DIM, seq), BF16),
                        pltpu.VMEM((seq, HEAD_DIM), BF16), pltpu.VMEM((HEAD_DIM, seq), BF16)],
        compiler_params=_params(("parallel", "parallel", "arbitrary"), est),
        name="nsa_attention",
    )(proj, proj, proj, proj, proj, kc, vc, tables, tables, gates, q_norm, k_norm, cov, onehot)


def _pool_kernel(x_ref, halo_ref, gm_ref, wp_ref, ps_ref, gf_ref, wrh_ref, wrl_ref, x_out, route_out,
                 *, tq, seq, cg):
    i = pl.program_id(0)
    start = (i * tq) % seq
    x = x_ref[...]
    h = _rms(x, gm_ref[...])
    hh = jnp.where(start == 0, 0.0, _rms(halo_ref[...], gm_ref[...]))
    t1 = (start + lax.broadcasted_iota(jnp.int32, (tq, 1), 0) + 1).astype(F32)
    ys = []
    for gi, w in enumerate(POOL_WINDOWS):
        sl = slice(gi * cg, (gi + 1) * cg)
        s = jnp.concatenate([hh[:, sl], h[:, sl]], axis=0)
        span = 1
        while span < w:
            s = s + pltpu.roll(s, span, 0)
            span *= 2
        dm = s[POOL_HALO:] * (1.0 / jnp.minimum(t1, float(w))) - h[:, sl]
        ys.append(jnp.dot(dm.astype(BF16), wp_ref[gi], preferred_element_type=F32))
    x3 = x + jnp.concatenate(ys, axis=1) * ps_ref[...]
    x_out[...] = x3

    h4 = _rms(x3, gf_ref[...])
    h_hi = h4.astype(BF16)
    h_lo = (h4 - h_hi.astype(F32)).astype(BF16)
    logits = (jnp.dot(h_hi, wrh_ref[...], preferred_element_type=F32)
              + jnp.dot(h_hi, wrl_ref[...], preferred_element_type=F32)
              + jnp.dot(h_lo, wrh_ref[...], preferred_element_type=F32))
    lane = lax.broadcasted_iota(jnp.int32, (tq, LANES), 1)
    logits = jnp.where(lane < N_EXPERTS, logits, -jnp.inf)
    m1 = jnp.max(logits, axis=-1, keepdims=True)
    i1 = jnp.min(jnp.where(logits == m1, lane, LANES), axis=-1, keepdims=True)
    rest = jnp.where(lane == i1, -jnp.inf, logits)
    m2 = jnp.max(rest, axis=-1, keepdims=True)
    i2 = jnp.min(jnp.where(rest == m2, lane, LANES), axis=-1, keepdims=True)
    e2 = jnp.exp(m2 - m1)
    den = 1.0 + e2
    route = jnp.where(lane == 0, i1.astype(F32),
                      jnp.where(lane == 1, i2.astype(F32),
                                jnp.where(lane == 2, 1.0 / den, jnp.where(lane == 3, e2 / den, 0.0))))
    route_out[...] = route


def pool_and_route(x, g_mix, w_pool, pool_scale, g_ffn, w_router, seq, tq=256):
    m, d = x.shape
    cg = d // len(POOL_WINDOWS)
    tq = _pick(seq, tq, POOL_HALO)
    hb = tq // POOL_HALO
    vec = pl.BlockSpec((1, d), lambda i: (0, 0))
    wr = _pad_to(w_router, 1, LANES)
    wr_hi = wr.astype(BF16)
    wr_lo = (wr - wr_hi.astype(F32)).astype(BF16)
    wspec = pl.BlockSpec((d, LANES), lambda i: (0, 0))
    est = 2 * 2 * 4 * tq * d + 2 * 2 * len(POOL_WINDOWS) * cg * cg + 8 * 4 * tq * d + 4 * 2 * d * LANES
    return pl.pallas_call(
        functools.partial(_pool_kernel, tq=tq, seq=seq, cg=cg),
        grid=(m // tq,),
        in_specs=[pl.BlockSpec((tq, d), lambda i: (i, 0)),
                  pl.BlockSpec((POOL_HALO, d), lambda i: (jnp.maximum(i * hb - 1, 0), 0)),
                  vec,
                  pl.BlockSpec((len(POOL_WINDOWS), cg, cg), lambda i: (0, 0, 0)),
                  vec, vec, wspec, wspec],
        out_specs=[pl.BlockSpec((tq, d), lambda i: (i, 0)), pl.BlockSpec((tq, LANES), lambda i: (i, 0))],
        out_shape=[jax.ShapeDtypeStruct((m, d), F32), jax.ShapeDtypeStruct((m, LANES), F32)],
        compiler_params=_params(("parallel",), est),
        name="pool_mixer_router",
    )(x, x, g_mix, w_pool, pool_scale, g_ffn, wr_hi, wr_lo)


def _route_tables(route, tm, n_tiles):
    n = route.shape[0]
    i1 = route[:, 0].astype(jnp.int32)
    i2 = route[:, 1].astype(jnp.int32)
    onehot = jax.nn.one_hot(i1, N_EXPERTS, dtype=jnp.int32) + jax.nn.one_hot(i2, N_EXPERTS, dtype=jnp.int32)
    count = jnp.sum(onehot, axis=0)
    padded = (count + tm - 1) // tm * tm
    end = jnp.cumsum(padded)
    pos = (end - padded)[None, :] + jnp.cumsum(onehot, axis=0) - onehot
    p1 = jnp.take_along_axis(pos, i1[:, None], axis=1)[:, 0]
    p2 = jnp.take_along_axis(pos, i2[:, None], axis=1)[:, 0]
    tok = jnp.arange(n, dtype=jnp.int32)
    rows = n_tiles * tm
    row_token = jnp.zeros((rows,), jnp.int32).at[jnp.concatenate([p1, p2])].set(jnp.concatenate([tok, tok]))
    n_used = (end[-1] // tm).astype(jnp.int32)
    tile = jnp.arange(n_tiles, dtype=jnp.int32)
    tile_expert = jnp.searchsorted(end, tile * tm, side="right").astype(jnp.int32)
    tile_expert = jnp.where(tile < n_used, tile_expert, tile_expert[n_used - 1])
    tile_rows = jnp.clip((count - padded + end)[tile_expert] - tile * tm, 0, tm)
    tile_rows = jnp.where(tile < n_used, tile_rows, 0).astype(jnp.int32)
    ids = jnp.arange(N_EXPERTS, dtype=jnp.int32)
    later = jnp.where((ids[None, :] > ids[:, None]) & (count[None, :] > 0), ids[None, :], N_EXPERTS)
    following = jnp.min(later, axis=1)
    next_expert = jnp.where(following < N_EXPERTS, following, -1)[tile_expert].astype(jnp.int32)
    return (tile_expert, tile_rows, next_expert, row_token, n_used.reshape(1),
            p1.astype(jnp.int32), p2.astype(jnp.int32))


def _gather_start(idx_ref, base, src_hbm, dst_ref, sem, count):
    def issue(r, carry):
        pltpu.make_async_copy(src_hbm.at[pl.ds(idx_ref[base + r], 1), :], dst_ref.at[pl.ds(r, 1), :], sem).start()
        return carry

    lax.fori_loop(0, count, issue, 0, unroll=8)


def _gather_wait(src_hbm, dst_ref, sem, count):
    pltpu.make_async_copy(src_hbm.at[pl.ds(0, count), :], dst_ref, sem).wait()


MOE_SUB = 256


def _moe_gather_kernel(rt_ref, nu_ref, x_hbm, g_ref, o_ref, xg_ref, sem, *, tm):
    i = pl.program_id(0)
    valid = i < nu_ref[0]
    slot = i % 2

    @pl.when(i == 0)
    def _():
        _gather_start(rt_ref, 0, x_hbm, xg_ref.at[0], sem.at[0], tm)

    @pl.when(i + 1 < nu_ref[0])
    def _():
        _gather_start(rt_ref, (i + 1) * tm, x_hbm, xg_ref.at[1 - slot], sem.at[1 - slot], tm)

    @pl.when(valid)
    def _():
        _gather_wait(x_hbm, xg_ref.at[slot], sem.at[slot], tm)
        o_ref[...] = _rms(xg_ref[slot], g_ref[...]).astype(o_ref.dtype)

    @pl.when(jnp.logical_not(valid))
    def _():
        o_ref[...] = jnp.zeros_like(o_ref)


def moe_gather(x, g_ffn, row_token, n_used, tm):
    d = x.shape[1]
    n_tiles = row_token.shape[0] // tm
    return pl.pallas_call(
        functools.partial(_moe_gather_kernel, tm=tm),
        grid_spec=pltpu.PrefetchScalarGridSpec(
            num_scalar_prefetch=2,
            grid=(n_tiles,),
            in_specs=[pl.BlockSpec(memory_space=pl.ANY), pl.BlockSpec((1, d), lambda i, rt, nu: (0, 0))],
            out_specs=pl.BlockSpec((tm, d), lambda i, rt, nu: (i, 0)),
            scratch_shapes=[pltpu.VMEM((2, tm, d), F32), pltpu.SemaphoreType.DMA((2,))]),
        out_shape=jax.ShapeDtypeStruct((n_tiles * tm, d), BF16),
        compiler_params=_params(("arbitrary",), 2 * 4 * tm * d + 2 * 2 * tm * d + 2 * 4 * tm * d),
        name="moe_gather",
    )(row_token, n_used, x, g_ffn)


def _stage_expert_weights(te_ref, nu_ref, nx_ref, w_hbm, stage, work, sem, width):
    sweep = pl.program_id(0)
    i = pl.program_id(1)

    def copies(e, col_tile):
        cols = pl.ds(pl.multiple_of(col_tile * width, width), width)
        return [pltpu.make_async_copy(w.at[e, :, cols], s, sem.at[k]) for k, (w, s) in enumerate(zip(w_hbm, stage))]

    def start(e, col_tile):
        for c in copies(e, col_tile):
            c.start()

    @pl.when((sweep == 0) & (i == 0))
    def _():
        start(te_ref[0], 0)

    @pl.when((i < nu_ref[0]) & ((i == 0) | (te_ref[i] != te_ref[jnp.maximum(i - 1, 0)])))
    def _():
        for c in copies(0, 0):
            c.wait()

        def cast_rows(r, carry):
            rows = pl.ds(pl.multiple_of(r * MOE_SUB, MOE_SUB), MOE_SUB)
            for s, b in zip(stage, work):
                b[rows, :] = s[rows, :].astype(BF16)
            return carry

        lax.fori_loop(0, stage[0].shape[0] // MOE_SUB, cast_rows, 0)
        nxt = nx_ref[i]

        @pl.when(nxt >= 0)
        def _():
            start(nxt, sweep)

        @pl.when((nxt < 0) & (sweep + 1 < pl.num_programs(0)))
        def _():
            start(te_ref[0], sweep + 1)


def _moe_glu_kernel(te_ref, tr_ref, nu_ref, nx_ref, h_ref, w1_hbm, w3_hbm, o_ref, s1, s3, w1b, w3b, sem, *, tm, tf):
    i = pl.program_id(1)
    valid = i < nu_ref[0]
    _stage_expert_weights(te_ref, nu_ref, nx_ref, [w1_hbm, w3_hbm], [s1, s3], [w1b, w3b], sem, tf)

    for sb in range(tm // MOE_SUB):
        rows = pl.ds(sb * MOE_SUB, MOE_SUB)
        live = valid & (sb * MOE_SUB < tr_ref[i])

        @pl.when(live)
        def _():
            h = h_ref[rows, :]
            a = jnp.dot(h, w1b[...], preferred_element_type=F32)
            b = jnp.dot(h, w3b[...], preferred_element_type=F32)
            o_ref[rows, :] = (a * jax.nn.sigmoid(a) * b).astype(o_ref.dtype)

        @pl.when(jnp.logical_not(live))
        def _():
            o_ref[rows, :] = jnp.zeros((MOE_SUB, o_ref.shape[1]), o_ref.dtype)


def moe_glu(h, w1, w3, tile_expert, tile_rows, n_used, next_expert, tm, tf):
    d = h.shape[1]
    ff = w1.shape[2]
    tf = _pick(ff, tf)
    n_tiles = tile_expert.shape[0]
    est = 2 * 4 * d * tf + 2 * 2 * d * tf + 2 * 2 * tm * d + 2 * 2 * tm * tf + 4 * 4 * MOE_SUB * tf + 4 * d * tf
    return pl.pallas_call(
        functools.partial(_moe_glu_kernel, tm=tm, tf=tf),
        grid_spec=pltpu.PrefetchScalarGridSpec(
            num_scalar_prefetch=4,
            grid=(ff // tf, n_tiles),
            in_specs=[pl.BlockSpec((tm, d), lambda f, i, te, tr, nu, nx: (jnp.minimum(i, nu[0] - 1), 0)),
                      pl.BlockSpec(memory_space=pl.ANY), pl.BlockSpec(memory_space=pl.ANY)],
            out_specs=pl.BlockSpec((tm, tf), lambda f, i, te, tr, nu, nx: (i, f)),
            scratch_shapes=[pltpu.VMEM((d, tf), F32), pltpu.VMEM((d, tf), F32),
                            pltpu.VMEM((d, tf), BF16), pltpu.VMEM((d, tf), BF16),
                            pltpu.SemaphoreType.DMA((2,))]),
        out_shape=jax.ShapeDtypeStruct((n_tiles * tm, ff), BF16),
        compiler_params=_params(("arbitrary", "arbitrary"), est),
        name="moe_glu",
    )(tile_expert, tile_rows, n_used, next_expert, h, w1, w3)


def _moe_down_kernel(te_ref, tr_ref, nu_ref, nx_ref, a_ref, w_hbm, o_ref, s2, w2b, sem, *, tm, tn):
    i = pl.program_id(1)
    valid = i < nu_ref[0]
    _stage_expert_weights(te_ref, nu_ref, nx_ref, [w_hbm], [s2], [w2b], sem, tn)
    for sb in range(tm // MOE_SUB):
        rows = pl.ds(sb * MOE_SUB, MOE_SUB)
        live = valid & (sb * MOE_SUB < tr_ref[i])

        @pl.when(live)
        def _():
            o_ref[rows, :] = jnp.dot(a_ref[rows, :], w2b[...], preferred_element_type=F32)

        @pl.when(jnp.logical_not(live))
        def _():
            o_ref[rows, :] = jnp.zeros((MOE_SUB, o_ref.shape[1]), o_ref.dtype)


def moe_down(act, w2, tile_expert, tile_rows, n_used, next_expert, tm, tn):
    ff = act.shape[1]
    d = w2.shape[2]
    tn = _pick(d, tn)
    n_tiles = tile_expert.shape[0]
    est = 4 * ff * tn + 2 * ff * tn + 2 * 2 * tm * ff + 2 * 4 * tm * tn + 4 * MOE_SUB * tn + 4 * MOE_SUB * tn
    return pl.pallas_call(
        functools.partial(_moe_down_kernel, tm=tm, tn=tn),
        grid_spec=pltpu.PrefetchScalarGridSpec(
            num_scalar_prefetch=4,
            grid=(d // tn, n_tiles),
            in_specs=[pl.BlockSpec((tm, ff), lambda j, i, te, tr, nu, nx: (jnp.minimum(i, nu[0] - 1), 0)),
                      pl.BlockSpec(memory_space=pl.ANY)],
            out_specs=pl.BlockSpec((tm, tn), lambda j, i, te, tr, nu, nx: (i, j)),
            scratch_shapes=[pltpu.VMEM((ff, tn), F32), pltpu.VMEM((ff, tn), BF16),
                            pltpu.SemaphoreType.DMA((1,))]),
        out_shape=jax.ShapeDtypeStruct((n_tiles * tm, d), F32),
        compiler_params=_params(("arbitrary", "arbitrary"), est),
        name="moe_down",
    )(tile_expert, tile_rows, n_used, next_expert, act, w2)


def _moe_combine_kernel(p1_ref, p2_ref, x_ref, r_ref, y_hbm, o_ref, ya_ref, yb_ref, sem_a, sem_b, *, tq):
    i = pl.program_id(0)
    slot = i % 2

    def start(step, s):
        _gather_start(p1_ref, step * tq, y_hbm, ya_ref.at[s], sem_a.at[s], tq)
        _gather_start(p2_ref, step * tq, y_hbm, yb_ref.at[s], sem_b.at[s], tq)

    @pl.when(i == 0)
    def _():
        start(0, 0)

    @pl.when(i + 1 < pl.num_programs(0))
    def _():
        start(i + 1, 1 - slot)

    _gather_wait(y_hbm, ya_ref.at[slot], sem_a.at[slot], tq)
    _gather_wait(y_hbm, yb_ref.at[slot], sem_b.at[slot], tq)
    r = r_ref[...]
    o_ref[...] = x_ref[...] + (r[:, 2:3] * ya_ref[slot] + r[:, 3:4] * yb_ref[slot])


def moe_combine(x, route, y, p1, p2, tq=256):
    m, d = x.shape
    tq = _pick(m, tq, 8)
    return pl.pallas_call(
        functools.partial(_moe_combine_kernel, tq=tq),
        grid_spec=pltpu.PrefetchScalarGridSpec(
            num_scalar_prefetch=2,
            grid=(m // tq,),
            in_specs=[pl.BlockSpec((tq, d), lambda i, a, b: (i, 0)),
                      pl.BlockSpec((tq, LANES), lambda i, a, b: (i, 0)),
                      pl.BlockSpec(memory_space=pl.ANY)],
            out_specs=pl.BlockSpec((tq, d), lambda i, a, b: (i, 0)),
            scratch_shapes=[pltpu.VMEM((2, tq, d), F32), pltpu.VMEM((2, tq, d), F32),
                            pltpu.SemaphoreType.DMA((2,)), pltpu.SemaphoreType.DMA((2,))]),
        out_shape=jax.ShapeDtypeStruct((m, d), F32),
        compiler_params=_params(("arbitrary",), 9 * 4 * tq * d),
        name="moe_combine",
    )(p1, p2, x, route, y)


def _pad_to(a, axis, size):
    if a.shape[axis] == size:
        return a
    pad = [(0, 0)] * a.ndim
    pad[axis] = (0, size - a.shape[axis])
    return jnp.pad(a, pad)


def _even_layer(x2d, batch, seq, tables, norm_mix, w_in, conv_w, q_norm, k_norm, pe_k, pe_v, w_cmp_k, w_cmp_v,
                w_out, norm_ffn, w1, w3, w2):
    d = x2d.shape[1]
    ch = d // 2
    heads = ch // HEAD_DIM
    hpg = heads // KV_HEADS
    kv_w = KV_HEADS * HEAD_DIM
    main = 3 * ch + heads * HEAD_DIM + 6 * kv_w
    assert main + heads * N_BRANCH == w_in.shape[1]

    h = rmsnorm(x2d, norm_mix, BF16)
    proj = matmul([(h, w_in, 0)], None, F32, 1024, 512, "in_proj", n=main)
    wg = w_in[:, main:].reshape(d, KV_HEADS, hpg * N_BRANCH)
    wg = _pad_to(wg, 2, LANES).reshape(d, KV_HEADS * LANES).astype(BF16)
    gates = matmul([(h, wg, 0)], None, F32, 1024, KV_HEADS * LANES, "gate_proj")

    y_a = short_conv(proj, conv_w.T, ch, seq)
    base = (3 * ch + heads * HEAD_DIM) // HEAD_DIM
    step = kv_w // HEAD_DIM
    blk = {"q": 3 * ch // HEAD_DIM, "kc": base, "vc": base + step, "ks": base + 2 * step,
           "vs": base + 3 * step, "kw": base + 4 * step, "vw": base + 5 * step}
    kn = k_norm.reshape(1, HEAD_DIM)
    kc, vc = compress(proj, blk["kc"], blk["vc"], pe_k, pe_v, w_cmp_k.astype(BF16), w_cmp_v.astype(BF16),
                      kn, batch, seq)
    y_b = nsa_attention(proj, gates, kc, vc, tables, q_norm.reshape(1, HEAD_DIM), kn, batch, seq, hpg, blk)

    x2d = matmul([(y_a, w_out, 0), (y_b, w_out, 1)], x2d, F32, 1024, 512, "out_proj")

    h2 = rmsnorm(x2d, norm_ffn, BF16)
    act = glu_matmul(h2, w1, w3, 1024, 256)
    return matmul_acc(act, w2.astype(BF16), x2d, 512, 1024, 5632, "dense_down")


def _odd_layer(x2d, seq, norm_mix, w_pool, pool_scale, norm_ffn, w_router, w1, w3, w2):
    m, d = x2d.shape
    x3, route = pool_and_route(x2d, norm_mix.reshape(1, d), w_pool.astype(BF16), pool_scale.reshape(1, d),
                               norm_ffn.reshape(1, d), w_router, seq)
    tm = min(512, m // 8)
    n_tiles = 2 * m // tm + N_EXPERTS
    tile_expert, tile_rows, next_expert, row_token, n_used, p1, p2 = _route_tables(route, tm, n_tiles)
    hs = moe_gather(x3, norm_ffn.reshape(1, d), row_token, n_used, tm)
    act = moe_glu(hs, w1, w3, tile_expert, tile_rows, n_used, next_expert, tm, 512)
    y = moe_down(act, w2, tile_expert, tile_rows, n_used, next_expert, tm, 512)
    return moe_combine(x3, route, y, p1, p2)


def kernel(x, norm_mix_even, w_in_even, conv_w_even, q_norm_even, k_norm_even, cmp_pe_k_even, cmp_pe_v_even,
           w_cmp_k_even, w_cmp_v_even, w_out_even, norm_ffn_even, w1_dense, w3_dense, w2_dense,
           norm_mix_odd, w_pool_odd, pool_scale_odd, norm_ffn_odd, w_router_odd,
           w1_moe, w3_moe, w2_moe, rel_bias):
    batch, seq, d = x.shape
    depth = norm_mix_even.shape[0] + norm_mix_odd.shape[0]
    tables = bias_tables(rel_bias, seq)
    x2d = x.reshape(batch * seq, d)
    for layer in range(depth):
        i = layer // 2
        if layer % 2 == 0:
            x2d = _even_layer(x2d, batch, seq, tables, norm_mix_even[i], w_in_even[i], conv_w_even[i],
                              q_norm_even[i], k_norm_even[i], cmp_pe_k_even[i], cmp_pe_v_even[i],
                              w_cmp_k_even[i], w_cmp_v_even[i], w_out_even[i], norm_ffn_even[i],
                              w1_dense[i], w3_dense[i], w2_dense[i])
        else:
            x2d = _odd_layer(x2d, seq, norm_mix_odd[i], w_pool_odd[i], pool_scale_odd[i], norm_ffn_odd[i],
                             w_router_odd[i], w1_moe[i], w3_moe[i], w2_moe[i])
    return x2d.reshape(batch, seq, d)
```

```python
import functools
import math

import numpy as np
import jax
import jax.numpy as jnp
from jax import lax
from jax.experimental import pallas as pl
from jax.experimental.pallas import tpu as pltpu

F32 = jnp.float32
BF16 = jnp.bfloat16

HEAD_DIM = 128
KV_HEADS = 4
CONV_TAPS = 3
N_BRANCH = 3
CMP_BLOCK = 32
CMP_STRIDE = 16
SEL_BLOCK = 64
N_SELECT = 16
WINDOW = 512
FORCE_SCORE = 1e6
NEG_INF = -1e30
REL_BUCKETS = 32
REL_MAX_DIST = 128
POOL_WINDOWS = (2, 4, 8, 16)
N_EXPERTS = 8
EPS = 1e-6

LANES = 128
POOL_HALO = 16
CONV_HALO = 8
VMEM_CAP = 60 * 1024 * 1024
NT_DIMS = (((1,), (1,)), ((), ()))


def _pick(n, pref, mult=LANES):
    t = min(pref, n)
    while n % t or t % mult:
        t -= mult
    return t


def _params(sem, est_bytes):
    limit = int(min(max(est_bytes * 5 // 4 + (4 << 20), 16 << 20), VMEM_CAP))
    return pltpu.CompilerParams(dimension_semantics=sem, vmem_limit_bytes=limit)


def _rms(x, g):
    return x * lax.rsqrt(jnp.mean(x * x, axis=-1, keepdims=True) + EPS) * g


def _rmsnorm_kernel(x_ref, g_ref, o_ref):
    o_ref[...] = _rms(x_ref[...], g_ref[...]).astype(o_ref.dtype)


def rmsnorm(x, g, out_dtype, tm=256):
    m, d = x.shape
    tm = _pick(m, tm, 8)
    return pl.pallas_call(
        _rmsnorm_kernel,
        grid=(m // tm,),
        in_specs=[pl.BlockSpec((tm, d), lambda i: (i, 0)), pl.BlockSpec((1, d), lambda i: (0, 0))],
        out_specs=pl.BlockSpec((tm, d), lambda i: (i, 0)),
        out_shape=jax.ShapeDtypeStruct((m, d), out_dtype),
        compiler_params=_params(("parallel",), 2 * tm * d * 8),
        name="rmsnorm",
    )(x, g.reshape(1, d))


def _mm_kernel(*refs, n_pairs, has_res):
    o_ref = refs[-1]
    acc = None
    for p in range(n_pairs):
        d = jnp.dot(refs[2 * p][...], refs[2 * p + 1][...], preferred_element_type=F32)
        acc = d if acc is None else acc + d
    if has_res:
        acc = refs[2 * n_pairs][...] + acc
    o_ref[...] = acc.astype(o_ref.dtype)


def matmul(pairs, res, out_dtype, tm, tn, name, n=None):
    m = pairs[0][0].shape[0]
    n = pairs[0][1].shape[1] if n is None else n
    tm = _pick(m, tm)
    tn = _pick(n, tn)
    in_specs, args, est = [], [], 0
    for x, w, r in pairs:
        k = x.shape[1]
        in_specs += [pl.BlockSpec((tm, k), lambda i, j: (i, 0)), pl.BlockSpec((k, tn), lambda i, j, r=r: (r, j))]
        args += [x, w]
        est += 2 * 2 * (tm * k + k * tn)
    if res is not None:
        in_specs.append(pl.BlockSpec((tm, tn), lambda i, j: (i, j)))
        args.append(res)
        est += 2 * 4 * tm * tn
    est += 2 * 4 * tm * tn + 4 * tm * tn
    return pl.pallas_call(
        functools.partial(_mm_kernel, n_pairs=len(pairs), has_res=res is not None),
        grid=(m // tm, n // tn),
        in_specs=in_specs,
        out_specs=pl.BlockSpec((tm, tn), lambda i, j: (i, j)),
        out_shape=jax.ShapeDtypeStruct((m, n), out_dtype),
        compiler_params=_params(("parallel", "parallel"), est),
        name=name,
    )(*args)


def _glu_kernel(h_ref, w1_ref, w3_ref, o_ref):
    h = h_ref[...]
    a = jnp.dot(h, w1_ref[...].astype(BF16), preferred_element_type=F32)
    b = jnp.dot(h, w3_ref[...].astype(BF16), preferred_element_type=F32)
    o_ref[...] = (a * jax.nn.sigmoid(a) * b).astype(o_ref.dtype)


def glu_matmul(h, w1, w3, tm, tn):
    m, k = h.shape
    n = w1.shape[1]
    tm = _pick(m, tm)
    tn = _pick(n, tn)
    est = 2 * 2 * (tm * k + tm * tn) + 2 * 2 * 4 * k * tn + 2 * 2 * k * tn + 3 * 4 * tm * tn
    return pl.pallas_call(
        _glu_kernel,
        grid=(m // tm, n // tn),
        in_specs=[pl.BlockSpec((tm, k), lambda i, j: (i, 0)),
                  pl.BlockSpec((k, tn), lambda i, j: (0, j)),
                  pl.BlockSpec((k, tn), lambda i, j: (0, j))],
        out_specs=pl.BlockSpec((tm, tn), lambda i, j: (i, j)),
        out_shape=jax.ShapeDtypeStruct((m, n), BF16),
        compiler_params=_params(("parallel", "parallel"), est),
        name="dense_glu",
    )(h, w1, w3)


def _mm_acc_kernel(x_ref, w_ref, r_ref, o_ref, acc_ref):
    k = pl.program_id(2)

    @pl.when(k == 0)
    def _():
        acc_ref[...] = jnp.zeros_like(acc_ref)

    acc_ref[...] += jnp.dot(x_ref[...], w_ref[...], preferred_element_type=F32)

    @pl.when(k == pl.num_programs(2) - 1)
    def _():
        o_ref[...] = r_ref[...] + acc_ref[...]


def matmul_acc(x, w, res, tm, tn, tk, name):
    m, k = x.shape
    n = w.shape[1]
    tm, tn, tk = _pick(m, tm), _pick(n, tn), _pick(k, tk)
    est = 2 * 2 * (tm * tk + tk * tn) + 5 * 4 * tm * tn
    return pl.pallas_call(
        _mm_acc_kernel,
        grid=(m // tm, n // tn, k // tk),
        in_specs=[pl.BlockSpec((tm, tk), lambda i, j, kk: (i, kk)),
                  pl.BlockSpec((tk, tn), lambda i, j, kk: (kk, j)),
                  pl.BlockSpec((tm, tn), lambda i, j, kk: (i, j))],
        out_specs=pl.BlockSpec((tm, tn), lambda i, j, kk: (i, j)),
        out_shape=jax.ShapeDtypeStruct((m, n), F32),
        scratch_shapes=[pltpu.VMEM((tm, tn), F32)],
        compiler_params=_params(("parallel", "parallel", "arbitrary"), est),
        name=name,
    )(x, w, res)


def _conv_kernel(ab_ref, ac_ref, au_ref, hc_ref, hu_ref, w_ref, o_ref, *, tq, seq):
    i = pl.program_id(0)
    first = (i * tq) % seq == 0
    v = ac_ref[...] * au_ref[...]
    hv = jnp.where(first, 0.0, hc_ref[...] * hu_ref[...])
    rows = lax.broadcasted_iota(jnp.int32, v.shape, 0)
    v1 = jnp.where(rows == 0, hv[CONV_HALO - 1:CONV_HALO], pltpu.roll(v, 1, 0))
    v2 = jnp.where(rows == 0, hv[CONV_HALO - 2:CONV_HALO - 1],
                   jnp.where(rows == 1, hv[CONV_HALO - 1:CONV_HALO], pltpu.roll(v, 2, 0)))
    y = w_ref[0:1, :] * v2 + w_ref[1:2, :] * v1 + w_ref[2:3, :] * v
    o_ref[...] = (ab_ref[...] * y).astype(o_ref.dtype)


def short_conv(proj, conv_w_t, ch, seq, tq=512, tc=1024):
    m = proj.shape[0]
    tq = _pick(seq, tq, CONV_HALO)
    tc = _pick(ch, tc)
    nj = ch // tc
    hb = tq // CONV_HALO
    halo = lambda off: pl.BlockSpec((CONV_HALO, tc), lambda i, j: (jnp.maximum(i * hb - 1, 0), off + j))
    return pl.pallas_call(
        functools.partial(_conv_kernel, tq=tq, seq=seq),
        grid=(m // tq, nj),
        in_specs=[pl.BlockSpec((tq, tc), lambda i, j: (i, j)),
                  pl.BlockSpec((tq, tc), lambda i, j: (i, nj + j)),
                  pl.BlockSpec((tq, tc), lambda i, j: (i, 2 * nj + j)),
                  halo(nj), halo(2 * nj),
                  pl.BlockSpec((CONV_TAPS, tc), lambda i, j: (0, j))],
        out_specs=pl.BlockSpec((tq, tc), lambda i, j: (i, j)),
        out_shape=jax.ShapeDtypeStruct((m, ch), BF16),
        compiler_params=_params(("parallel", "parallel"), 2 * 4 * 4 * tq * tc + 6 * 4 * tq * tc),
        name="short_conv",
    )(proj, proj, proj, proj, proj, conv_w_t)


def _rel_bucket_np(dist):
    dist = np.maximum(dist, 0)
    exact = REL_BUCKETS // 2
    d = np.maximum(dist, exact).astype(np.float32)
    large = exact + (np.log(d / np.float32(exact)) / np.float32(math.log(REL_MAX_DIST / exact))
                     * np.float32(REL_BUCKETS - exact)).astype(np.int32)
    return np.where(dist < exact, dist, np.minimum(large, REL_BUCKETS - 1)).astype(np.int32)


WIN_TILES = WINDOW // LANES
N_BIAS_TILES = 2 * WIN_TILES + 1


def _bias_index_tables(seq):
    r = np.arange(LANES)[None, :]
    c = np.arange(LANES)[:, None]
    far = np.full((LANES, LANES), REL_BUCKETS - 1, np.int32)
    zero = np.zeros((LANES, LANES), np.float32)
    neg = np.full((LANES, LANES), NEG_INF, np.float32)
    assert _rel_bucket_np(np.arange(LANES + 1, 4 * seq)).min() == REL_BUCKETS - 1
    idx, add = [], []
    for d in range(-WIN_TILES, WIN_TILES + 1):
        if d < 0:
            idx.append(far), add.append(neg)
        elif d == 0:
            idx.append(_rel_bucket_np(r - c)), add.append(np.where(r >= c, zero, neg))
        elif d == 1:
            idx.append(_rel_bucket_np(LANES + r - c)), add.append(zero)
        elif d < WIN_TILES:
            idx.append(far), add.append(zero)
        else:
            idx.append(far), add.append(np.where(c > r, zero, neg))
    n_tile_rows = len(idx) * LANES
    for qt in range(seq // LANES):
        idx.append(_rel_bucket_np(qt * LANES + r - (c * CMP_STRIDE + CMP_BLOCK - 1)))
        add.append(zero)
    idx = np.concatenate(idx, axis=0).astype(np.int32)
    add = np.concatenate(add, axis=0).astype(np.float32)
    shift = (np.arange(idx.shape[0])[:, None] < n_tile_rows).astype(np.float32) * np.ones((1, LANES), np.float32)
    return idx, shift, add


def _bias_kernel(tbl_ref, idx_ref, shift_ref, add_ref, o_ref, *, hpg):
    g = pl.program_id(0)
    idx = idx_ref[...]
    for h in range(hpg):
        head = g * hpg + h
        acc = jnp.zeros(idx.shape, F32)
        for b in range(REL_BUCKETS):
            acc = jnp.where(idx == b, tbl_ref[b, head], acc)
        o_ref[0, :, h * LANES:(h + 1) * LANES] = (acc - shift_ref[...] * tbl_ref[REL_BUCKETS - 1, head]
                                                   + add_ref[...])


def bias_tables(rel_bias, seq):
    heads = rel_bias.shape[1]
    hpg = heads // KV_HEADS
    idx, shift, add = (jnp.asarray(a) for a in _bias_index_tables(seq))
    rows = idx.shape[0]
    full = pl.BlockSpec((rows, LANES), lambda g: (0, 0))
    return pl.pallas_call(
        functools.partial(_bias_kernel, hpg=hpg),
        grid=(KV_HEADS,),
        in_specs=[pl.BlockSpec(memory_space=pltpu.SMEM), full, full, full],
        out_specs=pl.BlockSpec((1, rows, hpg * LANES), lambda g: (g, 0, 0)),
        out_shape=jax.ShapeDtypeStruct((KV_HEADS, rows, hpg * LANES), F32),
        compiler_params=_params(("parallel",), (6 + 2 * hpg + 4) * 4 * rows * LANES),
        name="rel_bias_tables",
    )(rel_bias, idx, shift, add)


def _compress_kernel(k_ref, v_ref, pek_ref, pev_ref, wk_ref, wv_ref, kn_ref, kc_ref, vc_ref, *, nch):
    half = CMP_BLOCK // CMP_STRIDE
    assert half == 2

    def comp(x_ref, pe_ref, w_ref):
        lo = jnp.zeros((nch, HEAD_DIM), F32)
        hi = jnp.zeros((nch, HEAD_DIM), F32)
        for l in range(CMP_STRIDE):
            rows = x_ref[pl.ds(l, nch, stride=CMP_STRIDE), :]
            lo += jnp.dot((rows + pe_ref[l:l + 1, :]).astype(BF16), w_ref[l], preferred_element_type=F32)
            hi += jnp.dot((rows + pe_ref[CMP_STRIDE + l:CMP_STRIDE + l + 1, :]).astype(BF16),
                          w_ref[CMP_STRIDE + l], preferred_element_type=F32)
        return lo + pltpu.roll(hi, nch - 1, 0)

    kc_ref[0, 0] = _rms(comp(k_ref, pek_ref, wk_ref), kn_ref[...])
    vc_ref[0, 0] = comp(v_ref, pev_ref, wv_ref).T


def compress(proj, kc_blk, vc_blk, pe_k, pe_v, w_k, w_v, k_norm, batch, seq):
    nch = seq // CMP_STRIDE
    wspec = pl.BlockSpec((CMP_BLOCK, HEAD_DIM, HEAD_DIM), lambda b, g: (0, 0, 0))
    pespec = pl.BlockSpec((CMP_BLOCK, HEAD_DIM), lambda b, g: (0, 0))
    ospec = pl.BlockSpec((1, 1, nch, HEAD_DIM), lambda b, g: (b, g, 0, 0))
    oshape = jax.ShapeDtypeStruct((batch, KV_HEADS, nch, HEAD_DIM), F32)
    return pl.pallas_call(
        functools.partial(_compress_kernel, nch=nch),
        grid=(batch, KV_HEADS),
        in_specs=[pl.BlockSpec((seq, HEAD_DIM), lambda b, g: (b, kc_blk + g)),
                  pl.BlockSpec((seq, HEAD_DIM), lambda b, g: (b, vc_blk + g)),
                  pespec, pespec, wspec, wspec,
                  pl.BlockSpec((1, HEAD_DIM), lambda b, g: (0, 0))],
        out_specs=[ospec, ospec],
        out_shape=[oshape, oshape],
        compiler_params=_params(("parallel", "parallel"), 4 * 4 * seq * HEAD_DIM + (4 << 20)),
        name="nsa_compress",
    )(proj, proj, pe_k, pe_v, w_k, w_v, k_norm)


SEL_CHUNK = 1024


def _nsa_kernel(q_ref, ks_ref, vs_ref, kw_ref, vw_ref, kc_ref, vct_ref, wt_ref, bc_ref, gt_ref,
                qn_ref, kn_ref, cov_ref, oh_ref, o_ref, ksa, vst, kwn, vwt, *, hpg, n_cmp, n_top):
    qi = pl.program_id(2)
    cols_all = hpg * LANES
    scale = HEAD_DIM ** -0.5
    n_sel_blocks = cov_ref.shape[0]
    chunk_tiles = SEL_CHUNK // LANES
    seq = ks_ref.shape[0]

    @pl.when(qi == 0)
    def _():
        ksa[:, :HEAD_DIM] = _rms(ks_ref[...], kn_ref[...]).astype(BF16)
        ksa[:, HEAD_DIM:] = oh_ref[...]
        kwn[...] = _rms(kw_ref[...], kn_ref[...]).astype(BF16)
        for c in range(seq // LANES):
            blk = slice(c * LANES, (c + 1) * LANES)
            vst[:, blk] = vs_ref[blk, :].T.astype(BF16)
            vwt[:, blk] = vw_ref[blk, :].T.astype(BF16)

    qs = [_rms(q_ref[:, h * HEAD_DIM:(h + 1) * HEAD_DIM], qn_ref[...]).astype(BF16) for h in range(hpg)]
    q = jnp.concatenate(qs, axis=0) if hpg > 1 else qs[0]

    def bias_tile(d):
        off = pl.multiple_of((d + WIN_TILES) * LANES, LANES)
        return wt_ref[0, pl.ds(off, LANES), :]

    blk_row = lax.broadcasted_iota(jnp.int32, (LANES, cols_all), 0)
    qry = lax.broadcasted_iota(jnp.int32, (LANES, cols_all), 1) & (LANES - 1)

    lc = lax.dot_general(kc_ref[0, 0].astype(BF16), q, NT_DIMS, preferred_element_type=F32) * scale + bc_ref[0]
    mc = ((qi * LANES + qry - (blk_row * CMP_STRIDE + CMP_BLOCK - 1)) >= 0) & (blk_row < n_cmp)
    zc = jnp.where(mc, lc, NEG_INF)
    ec = jnp.exp(zc - jnp.max(zc, axis=0, keepdims=True))
    pc = ec * (1.0 / jnp.sum(ec, axis=0, keepdims=True)) * mc.astype(F32)
    o_cmp = jnp.dot(vct_ref[0, 0].astype(BF16), pc.astype(BF16), preferred_element_type=F32)

    ps = pc[:, 0:LANES]
    for h in range(1, hpg):
        ps = ps + pc[:, h * LANES:(h + 1) * LANES]
    p1 = ps.astype(BF16)
    r1 = ps - p1.astype(F32)
    p2 = r1.astype(BF16)
    p3 = (r1 - p2.astype(F32)).astype(BF16)
    cov = cov_ref[...]
    score = (jnp.dot(cov, p1, preferred_element_type=F32) + jnp.dot(cov, p2, preferred_element_type=F32)
             + jnp.dot(cov, p3, preferred_element_type=F32))
    jj = lax.broadcasted_iota(jnp.int32, (n_sel_blocks, LANES), 0)
    ql = lax.broadcasted_iota(jnp.int32, (n_sel_blocks, LANES), 1)
    cur = (LANES // SEL_BLOCK) * qi + ql // SEL_BLOCK
    forced = (jj == 0) | (jj == cur) | (jj == cur - 1)
    score = jnp.where(forced, FORCE_SCORE, jnp.where(jj > cur, -FORCE_SCORE, score))
    rank = jnp.zeros((n_sel_blocks, LANES), F32)
    for j2 in range(n_sel_blocks):
        other = score[j2:j2 + 1, :]
        rank += ((other > score) | ((other == score) & (j2 < jj))).astype(F32)
    sel_t = ((rank < n_top) & (jj <= cur)).astype(BF16)
    sel_t = jnp.concatenate([sel_t, jnp.ones((LANES - n_sel_blocks, LANES), BF16)], axis=0)
    eye = (lax.broadcasted_iota(jnp.int32, (LANES, LANES), 0)
           == lax.broadcasted_iota(jnp.int32, (LANES, LANES), 1)).astype(BF16)
    selq = lax.dot_general(eye, sel_t, NT_DIMS, preferred_element_type=F32)
    sel_neg = ((selq - 1.0) * (-NEG_INF)).astype(BF16)
    sel_neg = jnp.concatenate([sel_neg] * hpg, axis=0) if hpg > 1 else sel_neg
    qa = jnp.concatenate([q, sel_neg], axis=1)

    def sel_chunk(c, carry, near):
        m, l, acc = carry
        off = pl.multiple_of(c * SEL_CHUNK, SEL_CHUNK)
        s = lax.dot_general(ksa[pl.ds(off, SEL_CHUNK), :], qa, NT_DIMS, preferred_element_type=F32) * scale
        if near:
            s = jnp.concatenate([s[u * LANES:(u + 1) * LANES]
                                 + bias_tile(jnp.clip(qi - (c * chunk_tiles + u), 0, 2))
                                 for u in range(chunk_tiles)], axis=0)
        m_new = jnp.maximum(m, jnp.max(s, axis=0, keepdims=True))
        alpha = jnp.exp(m - m_new)
        p = jnp.exp(s - m_new)
        l = alpha * l + jnp.sum(p, axis=0, keepdims=True)
        acc = alpha * acc + jnp.dot(vst[:, pl.ds(off, SEL_CHUNK)], p.astype(BF16), preferred_element_type=F32)
        return m_new, l, acc

    init = (jnp.full((1, cols_all), NEG_INF, F32), jnp.zeros((1, cols_all), F32),
            jnp.zeros((HEAD_DIM, cols_all), F32))
    n_far = (jnp.maximum(qi, 1) - 1) // chunk_tiles
    carry = lax.fori_loop(0, n_far, functools.partial(sel_chunk, near=False), init)
    _, l_s, a_s = lax.fori_loop(n_far, qi // chunk_tiles + 1, functools.partial(sel_chunk, near=True), carry)

    win_keys = WINDOW + LANES
    kt0 = jnp.maximum(qi - WIN_TILES, 0)
    woff = pl.multiple_of(kt0 * LANES, LANES)
    sw = lax.dot_general(kwn[pl.ds(woff, win_keys), :], q, NT_DIMS, preferred_element_type=F32) * scale
    zw = jnp.concatenate([sw[u * LANES:(u + 1) * LANES] + bias_tile(qi - (kt0 + u))
                          for u in range(win_keys // LANES)], axis=0)
    pw = jnp.exp(zw - jnp.max(zw, axis=0, keepdims=True))
    l_w = jnp.sum(pw, axis=0, keepdims=True)
    a_w = jnp.dot(vwt[:, pl.ds(woff, win_keys)], pw.astype(BF16), preferred_element_type=F32)

    gt = jax.nn.sigmoid(gt_ref[...]).T
    inv_s = 1.0 / l_s
    inv_w = 1.0 / l_w
    for h in range(hpg):
        sl = slice(h * LANES, (h + 1) * LANES)
        c0 = h * N_BRANCH
        o = (gt[c0:c0 + 1] * o_cmp[:, sl] + (gt[c0 + 1:c0 + 2] * inv_s[:, sl]) * a_s[:, sl]
             + (gt[c0 + 2:c0 + 3] * inv_w[:, sl]) * a_w[:, sl])
        o_ref[:, h * HEAD_DIM:(h + 1) * HEAD_DIM] = o.T.astype(o_ref.dtype)


def _cover_t(seq):
    nc = seq // CMP_STRIDE - CMP_BLOCK // CMP_STRIDE + 1
    ns = seq // SEL_BLOCK
    c_start = np.arange(nc) * CMP_STRIDE
    c_end = c_start + CMP_BLOCK - 1
    s_start = np.arange(ns) * SEL_BLOCK
    cover = (c_start[:, None] <= s_start[None, :] + SEL_BLOCK - 1) & (c_end[:, None] >= s_start[None, :])
    out = np.zeros((ns, seq // CMP_STRIDE), np.float32)
    out[:, :nc] = cover.T
    return out


def nsa_attention(proj, gates, kc, vc, tables, q_norm, k_norm, batch, seq, hpg, blk):
    heads = KV_HEADS * hpg
    nq = seq // LANES
    nch = seq // CMP_STRIDE
    assert nch == LANES, "compressed keys must fill exactly one lane tile"
    n_cmp = nch - CMP_BLOCK // CMP_STRIDE + 1
    ns = seq // SEL_BLOCK
    assert seq % SEL_CHUNK == 0 and ns <= LANES
    cov = jnp.asarray(_cover_t(seq), BF16)
    onehot = jnp.asarray(np.arange(seq)[:, None] // SEL_BLOCK == np.arange(LANES)[None, :], BF16)
    qw = hpg * HEAD_DIM
    q_blk = blk["q"] * HEAD_DIM // qw
    kv = lambda name: pl.BlockSpec((seq, HEAD_DIM), lambda b, g, i: (b, blk[name] + g))
    cspec = pl.BlockSpec((1, 1, nch, HEAD_DIM), lambda b, g, i: (b, g, 0, 0))
    vec = pl.BlockSpec((1, HEAD_DIM), lambda b, g, i: (0, 0))
    rows_all = hpg * LANES
    est = (2 * 4 * 4 * seq * HEAD_DIM + 5 * 2 * seq * HEAD_DIM + 2 * 2 * seq * LANES
           + 2 * 4 * hpg * (N_BIAS_TILES + 1) * LANES * LANES + 6 * 4 * rows_all * (WINDOW + LANES) + (8 << 20))
    return pl.pallas_call(
        functools.partial(_nsa_kernel, hpg=hpg, n_cmp=n_cmp, n_top=min(N_SELECT, ns)),
        grid=(batch, KV_HEADS, nq),
        in_specs=[pl.BlockSpec((LANES, qw), lambda b, g, i: (b * nq + i, q_blk + g)),
                  kv("ks"), kv("vs"), kv("kw"), kv("vw"), cspec, cspec,
                  pl.BlockSpec((1, N_BIAS_TILES * LANES, qw), lambda b, g, i: (g, 0, 0)),
                  pl.BlockSpec((1, LANES, qw), lambda b, g, i: (g, N_BIAS_TILES + i, 0)),
                  pl.BlockSpec((LANES, LANES), lambda b, g, i: (b * nq + i, g)),
                  vec, vec,
                  pl.BlockSpec((ns, nch), lambda b, g, i: (0, 0)),
                  pl.BlockSpec((seq, LANES), lambda b, g, i: (0, 0))],
        out_specs=pl.BlockSpec((LANES, qw), lambda b, g, i: (b * nq + i, g)),
        out_shape=jax.ShapeDtypeStruct((batch * seq, heads * HEAD_DIM), BF16),
        scratch_shapes=[pltpu.VMEM((seq, 2 * HEAD_DIM), BF16), pltpu.VMEM((HEAD_DIM, seq), BF16),
                        pltpu.VMEM((seq, HEAD_DIM), BF16), pltpu.VMEM((HEAD_DIM, seq), BF16)],
        compiler_params=_params(("parallel", "parallel", "arbitrary"), est),
        name="nsa_attention",
    )(proj, proj, proj, proj, proj, kc, vc, tables, tables, gates, q_norm, k_norm, cov, onehot)


def _pool_kernel(x_ref, halo_ref, gm_ref, wp_ref, ps_ref, gf_ref, wrh_ref, wrl_ref, x_out, route_out,
                 *, tq, seq, cg):
    i = pl.program_id(0)
    start = (i * tq) % seq
    x = x_ref[...]
    h = _rms(x, gm_ref[...])
    hh = jnp.where(start == 0, 0.0, _rms(halo_ref[...], gm_ref[...]))
    t1 = (start + lax.broadcasted_iota(jnp.int32, (tq, 1), 0) + 1).astype(F32)
    ys = []
    for gi, w in enumerate(POOL_WINDOWS):
        sl = slice(gi * cg, (gi + 1) * cg)
        s = jnp.concatenate([hh[:, sl], h[:, sl]], axis=0)
        span = 1
        while span < w:
            s = s + pltpu.roll(s, span, 0)
            span *= 2
        dm = s[POOL_HALO:] * (1.0 / jnp.minimum(t1, float(w))) - h[:, sl]
        ys.append(jnp.dot(dm.astype(BF16), wp_ref[gi], preferred_element_type=F32))
    x3 = x + jnp.concatenate(ys, axis=1) * ps_ref[...]
    x_out[...] = x3

    h4 = _rms(x3, gf_ref[...])
    h_hi = h4.astype(BF16)
    h_lo = (h4 - h_hi.astype(F32)).astype(BF16)
    logits = (jnp.dot(h_hi, wrh_ref[...], preferred_element_type=F32)
              + jnp.dot(h_hi, wrl_ref[...], preferred_element_type=F32)
              + jnp.dot(h_lo, wrh_ref[...], preferred_element_type=F32))
    lane = lax.broadcasted_iota(jnp.int32, (tq, LANES), 1)
    logits = jnp.where(lane < N_EXPERTS, logits, -jnp.inf)
    m1 = jnp.max(logits, axis=-1, keepdims=True)
    i1 = jnp.min(jnp.where(logits == m1, lane, LANES), axis=-1, keepdims=True)
    rest = jnp.where(lane == i1, -jnp.inf, logits)
    m2 = jnp.max(rest, axis=-1, keepdims=True)
    i2 = jnp.min(jnp.where(rest == m2, lane, LANES), axis=-1, keepdims=True)
    e2 = jnp.exp(m2 - m1)
    den = 1.0 + e2
    route = jnp.where(lane == 0, i1.astype(F32),
                      jnp.where(lane == 1, i2.astype(F32),
                                jnp.where(lane == 2, 1.0 / den, jnp.where(lane == 3, e2 / den, 0.0))))
    route_out[...] = route


def pool_and_route(x, g_mix, w_pool, pool_scale, g_ffn, w_router, seq, tq=256):
    m, d = x.shape
    cg = d // len(POOL_WINDOWS)
    tq = _pick(seq, tq, POOL_HALO)
    hb = tq // POOL_HALO
    vec = pl.BlockSpec((1, d), lambda i: (0, 0))
    wr = _pad_to(w_router, 1, LANES)
    wr_hi = wr.astype(BF16)
    wr_lo = (wr - wr_hi.astype(F32)).astype(BF16)
    wspec = pl.BlockSpec((d, LANES), lambda i: (0, 0))
    est = 2 * 2 * 4 * tq * d + 2 * 2 * len(POOL_WINDOWS) * cg * cg + 8 * 4 * tq * d + 4 * 2 * d * LANES
    return pl.pallas_call(
        functools.partial(_pool_kernel, tq=tq, seq=seq, cg=cg),
        grid=(m // tq,),
        in_specs=[pl.BlockSpec((tq, d), lambda i: (i, 0)),
                  pl.BlockSpec((POOL_HALO, d), lambda i: (jnp.maximum(i * hb - 1, 0), 0)),
                  vec,
                  pl.BlockSpec((len(POOL_WINDOWS), cg, cg), lambda i: (0, 0, 0)),
                  vec, vec, wspec, wspec],
        out_specs=[pl.BlockSpec((tq, d), lambda i: (i, 0)), pl.BlockSpec((tq, LANES), lambda i: (i, 0))],
        out_shape=[jax.ShapeDtypeStruct((m, d), F32), jax.ShapeDtypeStruct((m, LANES), F32)],
        compiler_params=_params(("parallel",), est),
        name="pool_mixer_router",
    )(x, x, g_mix, w_pool, pool_scale, g_ffn, wr_hi, wr_lo)


def _route_tables(route, tm, n_tiles):
    n = route.shape[0]
    i1 = route[:, 0].astype(jnp.int32)
    i2 = route[:, 1].astype(jnp.int32)
    onehot = jax.nn.one_hot(i1, N_EXPERTS, dtype=jnp.int32) + jax.nn.one_hot(i2, N_EXPERTS, dtype=jnp.int32)
    count = jnp.sum(onehot, axis=0)
    padded = (count + tm - 1) // tm * tm
    end = jnp.cumsum(padded)
    pos = (end - padded)[None, :] + jnp.cumsum(onehot, axis=0) - onehot
    p1 = jnp.take_along_axis(pos, i1[:, None], axis=1)[:, 0]
    p2 = jnp.take_along_axis(pos, i2[:, None], axis=1)[:, 0]
    tok = jnp.arange(n, dtype=jnp.int32)
    rows = n_tiles * tm
    row_token = jnp.zeros((rows,), jnp.int32).at[jnp.concatenate([p1, p2])].set(jnp.concatenate([tok, tok]))
    n_used = (end[-1] // tm).astype(jnp.int32)
    tile = jnp.arange(n_tiles, dtype=jnp.int32)
    tile_expert = jnp.searchsorted(end, tile * tm, side="right").astype(jnp.int32)
    tile_expert = jnp.where(tile < n_used, tile_expert, tile_expert[n_used - 1])
    tile_rows = jnp.clip((count - padded + end)[tile_expert] - tile * tm, 0, tm)
    tile_rows = jnp.where(tile < n_used, tile_rows, 0).astype(jnp.int32)
    ids = jnp.arange(N_EXPERTS, dtype=jnp.int32)
    later = jnp.where((ids[None, :] > ids[:, None]) & (count[None, :] > 0), ids[None, :], N_EXPERTS)
    following = jnp.min(later, axis=1)
    next_expert = jnp.where(following < N_EXPERTS, following, -1)[tile_expert].astype(jnp.int32)
    return (tile_expert, tile_rows, next_expert, row_token, n_used.reshape(1),
            p1.astype(jnp.int32), p2.astype(jnp.int32))


def _gather_start(idx_ref, base, src_hbm, dst_ref, sem, count):
    def issue(r, carry):
        pltpu.make_async_copy(src_hbm.at[pl.ds(idx_ref[base + r], 1), :], dst_ref.at[pl.ds(r, 1), :], sem).start()
        return carry

    lax.fori_loop(0, count, issue, 0, unroll=8)


def _gather_wait(src_hbm, dst_ref, sem, count):
    pltpu.make_async_copy(src_hbm.at[pl.ds(0, count), :], dst_ref, sem).wait()


MOE_SUB = 256


def _moe_gather_kernel(rt_ref, nu_ref, x_hbm, g_ref, o_ref, xg_ref, sem, *, tm):
    i = pl.program_id(0)
    valid = i < nu_ref[0]
    slot = i % 2

    @pl.when(i == 0)
    def _():
        _gather_start(rt_ref, 0, x_hbm, xg_ref.at[0], sem.at[0], tm)

    @pl.when(i + 1 < nu_ref[0])
    def _():
        _gather_start(rt_ref, (i + 1) * tm, x_hbm, xg_ref.at[1 - slot], sem.at[1 - slot], tm)

    @pl.when(valid)
    def _():
        _gather_wait(x_hbm, xg_ref.at[slot], sem.at[slot], tm)
        o_ref[...] = _rms(xg_ref[slot], g_ref[...]).astype(o_ref.dtype)

    @pl.when(jnp.logical_not(valid))
    def _():
        o_ref[...] = jnp.zeros_like(o_ref)


def moe_gather(x, g_ffn, row_token, n_used, tm):
    d = x.shape[1]
    n_tiles = row_token.shape[0] // tm
    return pl.pallas_call(
        functools.partial(_moe_gather_kernel, tm=tm),
        grid_spec=pltpu.PrefetchScalarGridSpec(
            num_scalar_prefetch=2,
            grid=(n_tiles,),
            in_specs=[pl.BlockSpec(memory_space=pl.ANY), pl.BlockSpec((1, d), lambda i, rt, nu: (0, 0))],
            out_specs=pl.BlockSpec((tm, d), lambda i, rt, nu: (i, 0)),
            scratch_shapes=[pltpu.VMEM((2, tm, d), F32), pltpu.SemaphoreType.DMA((2,))]),
        out_shape=jax.ShapeDtypeStruct((n_tiles * tm, d), BF16),
        compiler_params=_params(("arbitrary",), 2 * 4 * tm * d + 2 * 2 * tm * d + 2 * 4 * tm * d),
        name="moe_gather",
    )(row_token, n_used, x, g_ffn)


def _stage_expert_weights(te_ref, nu_ref, nx_ref, w_hbm, stage, work, sem, width):
    sweep = pl.program_id(0)
    i = pl.program_id(1)

    def copies(e, col_tile):
        cols = pl.ds(pl.multiple_of(col_tile * width, width), width)
        return [pltpu.make_async_copy(w.at[e, :, cols], s, sem.at[k]) for k, (w, s) in enumerate(zip(w_hbm, stage))]

    def start(e, col_tile):
        for c in copies(e, col_tile):
            c.start()

    @pl.when((sweep == 0) & (i == 0))
    def _():
        start(te_ref[0], 0)

    @pl.when((i < nu_ref[0]) & ((i == 0) | (te_ref[i] != te_ref[jnp.maximum(i - 1, 0)])))
    def _():
        for c in copies(0, 0):
            c.wait()

        def cast_rows(r, carry):
            rows = pl.ds(pl.multiple_of(r * MOE_SUB, MOE_SUB), MOE_SUB)
            for s, b in zip(stage, work):
                b[rows, :] = s[rows, :].astype(BF16)
            return carry

        lax.fori_loop(0, stage[0].shape[0] // MOE_SUB, cast_rows, 0)
        nxt = nx_ref[i]

        @pl.when(nxt >= 0)
        def _():
            start(nxt, sweep)

        @pl.when((nxt < 0) & (sweep + 1 < pl.num_programs(0)))
        def _():
            start(te_ref[0], sweep + 1)


def _moe_glu_kernel(te_ref, tr_ref, nu_ref, nx_ref, h_ref, w1_hbm, w3_hbm, o_ref, s1, s3, w1b, w3b, sem, *, tm, tf):
    i = pl.program_id(1)
    valid = i < nu_ref[0]
    _stage_expert_weights(te_ref, nu_ref, nx_ref, [w1_hbm, w3_hbm], [s1, s3], [w1b, w3b], sem, tf)

    for sb in range(tm // MOE_SUB):
        rows = pl.ds(sb * MOE_SUB, MOE_SUB)
        live = valid & (sb * MOE_SUB < tr_ref[i])

        @pl.when(live)
        def _():
            h = h_ref[rows, :]
            a = jnp.dot(h, w1b[...], preferred_element_type=F32)
            b = jnp.dot(h, w3b[...], preferred_element_type=F32)
            o_ref[rows, :] = (a * jax.nn.sigmoid(a) * b).astype(o_ref.dtype)

        @pl.when(jnp.logical_not(live))
        def _():
            o_ref[rows, :] = jnp.zeros((MOE_SUB, o_ref.shape[1]), o_ref.dtype)


def moe_glu(h, w1, w3, tile_expert, tile_rows, n_used, next_expert, tm, tf):
    d = h.shape[1]
    ff = w1.shape[2]
    tf = _pick(ff, tf)
    n_tiles = tile_expert.shape[0]
    est = 2 * 4 * d * tf + 2 * 2 * d * tf + 2 * 2 * tm * d + 2 * 2 * tm * tf + 4 * 4 * MOE_SUB * tf + 4 * d * tf
    return pl.pallas_call(
        functools.partial(_moe_glu_kernel, tm=tm, tf=tf),
        grid_spec=pltpu.PrefetchScalarGridSpec(
            num_scalar_prefetch=4,
            grid=(ff // tf, n_tiles),
            in_specs=[pl.BlockSpec((tm, d), lambda f, i, te, tr, nu, nx: (jnp.minimum(i, nu[0] - 1), 0)),
                      pl.BlockSpec(memory_space=pl.ANY), pl.BlockSpec(memory_space=pl.ANY)],
            out_specs=pl.BlockSpec((tm, tf), lambda f, i, te, tr, nu, nx: (i, f)),
            scratch_shapes=[pltpu.VMEM((d, tf), F32), pltpu.VMEM((d, tf), F32),
                            pltpu.VMEM((d, tf), BF16), pltpu.VMEM((d, tf), BF16),
                            pltpu.SemaphoreType.DMA((2,))]),
        out_shape=jax.ShapeDtypeStruct((n_tiles * tm, ff), BF16),
        compiler_params=_params(("arbitrary", "arbitrary"), est),
        name="moe_glu",
    )(tile_expert, tile_rows, n_used, next_expert, h, w1, w3)


def _moe_down_kernel(te_ref, tr_ref, nu_ref, nx_ref, a_ref, w_hbm, o_ref, s2, w2b, sem, *, tm, tn):
    i = pl.program_id(1)
    valid = i < nu_ref[0]
    _stage_expert_weights(te_ref, nu_ref, nx_ref, [w_hbm], [s2], [w2b], sem, tn)
    for sb in range(tm // MOE_SUB):
        rows = pl.ds(sb * MOE_SUB, MOE_SUB)
        live = valid & (sb * MOE_SUB < tr_ref[i])

        @pl.when(live)
        def _():
            o_ref[rows, :] = jnp.dot(a_ref[rows, :], w2b[...], preferred_element_type=F32)

        @pl.when(jnp.logical_not(live))
        def _():
            o_ref[rows, :] = jnp.zeros((MOE_SUB, o_ref.shape[1]), o_ref.dtype)


def moe_down(act, w2, tile_expert, tile_rows, n_used, next_expert, tm, tn):
    ff = act.shape[1]
    d = w2.shape[2]
    tn = _pick(d, tn)
    n_tiles = tile_expert.shape[0]
    est = 4 * ff * tn + 2 * ff * tn + 2 * 2 * tm * ff + 2 * 4 * tm * tn + 4 * MOE_SUB * tn + 4 * MOE_SUB * tn
    return pl.pallas_call(
        functools.partial(_moe_down_kernel, tm=tm, tn=tn),
        grid_spec=pltpu.PrefetchScalarGridSpec(
            num_scalar_prefetch=4,
            grid=(d // tn, n_tiles),
            in_specs=[pl.BlockSpec((tm, ff), lambda j, i, te, tr, nu, nx: (jnp.minimum(i, nu[0] - 1), 0)),
                      pl.BlockSpec(memory_space=pl.ANY)],
            out_specs=pl.BlockSpec((tm, tn), lambda j, i, te, tr, nu, nx: (i, j)),
            scratch_shapes=[pltpu.VMEM((ff, tn), F32), pltpu.VMEM((ff, tn), BF16),
                            pltpu.SemaphoreType.DMA((1,))]),
        out_shape=jax.ShapeDtypeStruct((n_tiles * tm, d), F32),
        compiler_params=_params(("arbitrary", "arbitrary"), est),
        name="moe_down",
    )(tile_expert, tile_rows, n_used, next_expert, act, w2)


def _moe_combine_kernel(p1_ref, p2_ref, x_ref, r_ref, y_hbm, o_ref, ya_ref, yb_ref, sem_a, sem_b, *, tq):
    i = pl.program_id(0)
    slot = i % 2

    def start(step, s):
        _gather_start(p1_ref, step * tq, y_hbm, ya_ref.at[s], sem_a.at[s], tq)
        _gather_start(p2_ref, step * tq, y_hbm, yb_ref.at[s], sem_b.at[s], tq)

    @pl.when(i == 0)
    def _():
        start(0, 0)

    @pl.when(i + 1 < pl.num_programs(0))
    def _():
        start(i + 1, 1 - slot)

    _gather_wait(y_hbm, ya_ref.at[slot], sem_a.at[slot], tq)
    _gather_wait(y_hbm, yb_ref.at[slot], sem_b.at[slot], tq)
    r = r_ref[...]
    o_ref[...] = x_ref[...] + (r[:, 2:3] * ya_ref[slot] + r[:, 3:4] * yb_ref[slot])


def moe_combine(x, route, y, p1, p2, tq=256):
    m, d = x.shape
    tq = _pick(m, tq, 8)
    return pl.pallas_call(
        functools.partial(_moe_combine_kernel, tq=tq),
        grid_spec=pltpu.PrefetchScalarGridSpec(
            num_scalar_prefetch=2,
            grid=(m // tq,),
            in_specs=[pl.BlockSpec((tq, d), lambda i, a, b: (i, 0)),
                      pl.BlockSpec((tq, LANES), lambda i, a, b: (i, 0)),
                      pl.BlockSpec(memory_space=pl.ANY)],
            out_specs=pl.BlockSpec((tq, d), lambda i, a, b: (i, 0)),
            scratch_shapes=[pltpu.VMEM((2, tq, d), F32), pltpu.VMEM((2, tq, d), F32),
                            pltpu.SemaphoreType.DMA((2,)), pltpu.SemaphoreType.DMA((2,))]),
        out_shape=jax.ShapeDtypeStruct((m, d), F32),
        compiler_params=_params(("arbitrary",), 9 * 4 * tq * d),
        name="moe_combine",
    )(p1, p2, x, route, y)


def _pad_to(a, axis, size):
    if a.shape[axis] == size:
        return a
    pad = [(0, 0)] * a.ndim
    pad[axis] = (0, size - a.shape[axis])
    return jnp.pad(a, pad)


def _even_layer(x2d, batch, seq, tables, norm_mix, w_in, conv_w, q_norm, k_norm, pe_k, pe_v, w_cmp_k, w_cmp_v,
                w_out, norm_ffn, w1, w3, w2):
    d = x2d.shape[1]
    ch = d // 2
    heads = ch // HEAD_DIM
    hpg = heads // KV_HEADS
    kv_w = KV_HEADS * HEAD_DIM
    main = 3 * ch + heads * HEAD_DIM + 6 * kv_w
    assert main + heads * N_BRANCH == w_in.shape[1]

    h = rmsnorm(x2d, norm_mix, BF16)
    proj = matmul([(h, w_in.astype(BF16), 0)], None, F32, 1024, 1024, "in_proj", n=main)
    wg = w_in[:, main:].reshape(d, KV_HEADS, hpg * N_BRANCH)
    wg = _pad_to(wg, 2, LANES).reshape(d, KV_HEADS * LANES).astype(BF16)
    gates = matmul([(h, wg, 0)], None, F32, 1024, KV_HEADS * LANES, "gate_proj")

    y_a = short_conv(proj, conv_w.T, ch, seq)
    base = (3 * ch + heads * HEAD_DIM) // HEAD_DIM
    step = kv_w // HEAD_DIM
    blk = {"q": 3 * ch // HEAD_DIM, "kc": base, "vc": base + step, "ks": base + 2 * step,
           "vs": base + 3 * step, "kw": base + 4 * step, "vw": base + 5 * step}
    kn = k_norm.reshape(1, HEAD_DIM)
    kc, vc = compress(proj, blk["kc"], blk["vc"], pe_k, pe_v, w_cmp_k.astype(BF16), w_cmp_v.astype(BF16),
                      kn, batch, seq)
    y_b = nsa_attention(proj, gates, kc, vc, tables, q_norm.reshape(1, HEAD_DIM), kn, batch, seq, hpg, blk)

    wo = w_out.astype(BF16)
    x2d = matmul([(y_a, wo, 0), (y_b, wo, 1)], x2d, F32, 1024, 512, "out_proj")

    h2 = rmsnorm(x2d, norm_ffn, BF16)
    act = glu_matmul(h2, w1, w3, 1024, 256)
    return matmul_acc(act, w2.astype(BF16), x2d, 512, 1024, 5632, "dense_down")


def _odd_layer(x2d, seq, norm_mix, w_pool, pool_scale, norm_ffn, w_router, w1, w3, w2):
    m, d = x2d.shape
    x3, route = pool_and_route(x2d, norm_mix.reshape(1, d), w_pool.astype(BF16), pool_scale.reshape(1, d),
                               norm_ffn.reshape(1, d), w_router, seq)
    tm = min(512, m // 8)
    n_tiles = 2 * m // tm + N_EXPERTS
    tile_expert, tile_rows, next_expert, row_token, n_used, p1, p2 = _route_tables(route, tm, n_tiles)
    hs = moe_gather(x3, norm_ffn.reshape(1, d), row_token, n_used, tm)
    act = moe_glu(hs, w1, w3, tile_expert, tile_rows, n_used, next_expert, tm, 512)
    y = moe_down(act, w2, tile_expert, tile_rows, n_used, next_expert, tm, 512)
    return moe_combine(x3, route, y, p1, p2)


def kernel(x, norm_mix_even, w_in_even, conv_w_even, q_norm_even, k_norm_even, cmp_pe_k_even, cmp_pe_v_even,
           w_cmp_k_even, w_cmp_v_even, w_out_even, norm_ffn_even, w1_dense, w3_dense, w2_dense,
           norm_mix_odd, w_pool_odd, pool_scale_odd, norm_ffn_odd, w_router_odd,
           w1_moe, w3_moe, w2_moe, rel_bias):
    batch, seq, d = x.shape
    depth = norm_mix_even.shape[0] + norm_mix_odd.shape[0]
    tables = bias_tables(rel_bias, seq)
    x2d = x.reshape(batch * seq, d)
    for layer in range(depth):
        i = layer // 2
        if layer % 2 == 0:
            x2d = _even_layer(x2d, batch, seq, tables, norm_mix_even[i], w_in_even[i], conv_w_even[i],
                              q_norm_even[i], k_norm_even[i], cmp_pe_k_even[i], cmp_pe_v_even[i],
                              w_cmp_k_even[i], w_cmp_v_even[i], w_out_even[i], norm_ffn_even[i],
                              w1_dense[i], w3_dense[i], w2_dense[i])
        else:
            x2d = _odd_layer(x2d, seq, norm_mix_odd[i], w_pool_odd[i], pool_scale_odd[i], norm_ffn_odd[i],
                             w_router_odd[i], w1_moe[i], w3_moe[i], w2_moe[i])
    return x2d.reshape(batch, seq, d)
```

```python
import functools
import math

import numpy as np
import jax
import jax.numpy as jnp
from jax import lax
from jax.experimental import pallas as pl
from jax.experimental.pallas import tpu as pltpu

F32 = jnp.float32
BF16 = jnp.bfloat16

HEAD_DIM = 128
KV_HEADS = 4
CONV_TAPS = 3
N_BRANCH = 3
CMP_BLOCK = 32
CMP_STRIDE = 16
SEL_BLOCK = 64
N_SELECT = 16
WINDOW = 512
FORCE_SCORE = 1e6
NEG_INF = -1e30
REL_BUCKETS = 32
REL_MAX_DIST = 128
POOL_WINDOWS = (2, 4, 8, 16)
N_EXPERTS = 8
EPS = 1e-6

LANES = 128
POOL_HALO = 16
CONV_HALO = 8
VMEM_CAP = 60 * 1024 * 1024
NT_DIMS = (((1,), (1,)), ((), ()))


def _pick(n, pref, mult=LANES):
    t = min(pref, n)
    while n % t or t % mult:
        t -= mult
    return t


def _params(sem, est_bytes):
    limit = int(min(max(est_bytes * 5 // 4 + (4 << 20), 16 << 20), VMEM_CAP))
    return pltpu.CompilerParams(dimension_semantics=sem, vmem_limit_bytes=limit)


def _rms(x, g):
    return x * lax.rsqrt(jnp.mean(x * x, axis=-1, keepdims=True) + EPS) * g


def _rmsnorm_kernel(x_ref, g_ref, o_ref):
    o_ref[...] = _rms(x_ref[...], g_ref[...]).astype(o_ref.dtype)


def rmsnorm(x, g, out_dtype, tm=256):
    m, d = x.shape
    tm = _pick(m, tm, 8)
    return pl.pallas_call(
        _rmsnorm_kernel,
        grid=(m // tm,),
        in_specs=[pl.BlockSpec((tm, d), lambda i: (i, 0)), pl.BlockSpec((1, d), lambda i: (0, 0))],
        out_specs=pl.BlockSpec((tm, d), lambda i: (i, 0)),
        out_shape=jax.ShapeDtypeStruct((m, d), out_dtype),
        compiler_params=_params(("parallel",), 2 * tm * d * 8),
        name="rmsnorm",
    )(x, g.reshape(1, d))


def _mm_kernel(*refs, n_pairs, has_res):
    o_ref = refs[-1]
    acc = None
    for p in range(n_pairs):
        d = jnp.dot(refs[2 * p][...], refs[2 * p + 1][...], preferred_element_type=F32)
        acc = d if acc is None else acc + d
    if has_res:
        acc = refs[2 * n_pairs][...] + acc
    o_ref[...] = acc.astype(o_ref.dtype)


def matmul(pairs, res, out_dtype, tm, tn, name, n=None):
    m = pairs[0][0].shape[0]
    n = pairs[0][1].shape[1] if n is None else n
    tm = _pick(m, tm)
    tn = _pick(n, tn)
    in_specs, args, est = [], [], 0
    for x, w, r in pairs:
        k = x.shape[1]
        in_specs += [pl.BlockSpec((tm, k), lambda i, j: (i, 0)), pl.BlockSpec((k, tn), lambda i, j, r=r: (r, j))]
        args += [x, w]
        est += 2 * 2 * (tm * k + k * tn)
    if res is not None:
        in_specs.append(pl.BlockSpec((tm, tn), lambda i, j: (i, j)))
        args.append(res)
        est += 2 * 4 * tm * tn
    est += 2 * 4 * tm * tn + 4 * tm * tn
    return pl.pallas_call(
        functools.partial(_mm_kernel, n_pairs=len(pairs), has_res=res is not None),
        grid=(m // tm, n // tn),
        in_specs=in_specs,
        out_specs=pl.BlockSpec((tm, tn), lambda i, j: (i, j)),
        out_shape=jax.ShapeDtypeStruct((m, n), out_dtype),
        compiler_params=_params(("parallel", "parallel"), est),
        name=name,
    )(*args)


def _glu_kernel(h_ref, w1_ref, w3_ref, o_ref):
    h = h_ref[...]
    a = jnp.dot(h, w1_ref[...].astype(BF16), preferred_element_type=F32)
    b = jnp.dot(h, w3_ref[...].astype(BF16), preferred_element_type=F32)
    o_ref[...] = (a * jax.nn.sigmoid(a) * b).astype(o_ref.dtype)


def glu_matmul(h, w1, w3, tm, tn):
    m, k = h.shape
    n = w1.shape[1]
    tm = _pick(m, tm)
    tn = _pick(n, tn)
    est = 2 * 2 * (tm * k + tm * tn) + 2 * 2 * 4 * k * tn + 2 * 2 * k * tn + 3 * 4 * tm * tn
    return pl.pallas_call(
        _glu_kernel,
        grid=(m // tm, n // tn),
        in_specs=[pl.BlockSpec((tm, k), lambda i, j: (i, 0)),
                  pl.BlockSpec((k, tn), lambda i, j: (0, j)),
                  pl.BlockSpec((k, tn), lambda i, j: (0, j))],
        out_specs=pl.BlockSpec((tm, tn), lambda i, j: (i, j)),
        out_shape=jax.ShapeDtypeStruct((m, n), BF16),
        compiler_params=_params(("parallel", "parallel"), est),
        name="dense_glu",
    )(h, w1, w3)


def _conv_kernel(ab_ref, ac_ref, au_ref, hc_ref, hu_ref, w_ref, o_ref, *, tq, seq):
    i = pl.program_id(0)
    first = (i * tq) % seq == 0
    v = ac_ref[...] * au_ref[...]
    hv = jnp.where(first, 0.0, hc_ref[...] * hu_ref[...])
    rows = lax.broadcasted_iota(jnp.int32, v.shape, 0)
    v1 = jnp.where(rows == 0, hv[CONV_HALO - 1:CONV_HALO], pltpu.roll(v, 1, 0))
    v2 = jnp.where(rows == 0, hv[CONV_HALO - 2:CONV_HALO - 1],
                   jnp.where(rows == 1, hv[CONV_HALO - 1:CONV_HALO], pltpu.roll(v, 2, 0)))
    y = w_ref[0:1, :] * v2 + w_ref[1:2, :] * v1 + w_ref[2:3, :] * v
    o_ref[...] = (ab_ref[...] * y).astype(o_ref.dtype)


def short_conv(proj, conv_w_t, ch, seq, tq=512, tc=1024):
    m = proj.shape[0]
    tq = _pick(seq, tq, CONV_HALO)
    tc = _pick(ch, tc)
    nj = ch // tc
    hb = tq // CONV_HALO
    halo = lambda off: pl.BlockSpec((CONV_HALO, tc), lambda i, j: (jnp.maximum(i * hb - 1, 0), off + j))
    return pl.pallas_call(
        functools.partial(_conv_kernel, tq=tq, seq=seq),
        grid=(m // tq, nj),
        in_specs=[pl.BlockSpec((tq, tc), lambda i, j: (i, j)),
                  pl.BlockSpec((tq, tc), lambda i, j: (i, nj + j)),
                  pl.BlockSpec((tq, tc), lambda i, j: (i, 2 * nj + j)),
                  halo(nj), halo(2 * nj),
                  pl.BlockSpec((CONV_TAPS, tc), lambda i, j: (0, j))],
        out_specs=pl.BlockSpec((tq, tc), lambda i, j: (i, j)),
        out_shape=jax.ShapeDtypeStruct((m, ch), BF16),
        compiler_params=_params(("parallel", "parallel"), 2 * 4 * 4 * tq * tc + 6 * 4 * tq * tc),
        name="short_conv",
    )(proj, proj, proj, proj, proj, conv_w_t)


def _rel_bucket_np(dist):
    dist = np.maximum(dist, 0)
    exact = REL_BUCKETS // 2
    d = np.maximum(dist, exact).astype(np.float32)
    large = exact + (np.log(d / np.float32(exact)) / np.float32(math.log(REL_MAX_DIST / exact))
                     * np.float32(REL_BUCKETS - exact)).astype(np.int32)
    return np.where(dist < exact, dist, np.minimum(large, REL_BUCKETS - 1)).astype(np.int32)


WIN_TILES = WINDOW // LANES
N_BIAS_TILES = 2 * WIN_TILES + 1


def _bias_index_tables(seq):
    r = np.arange(LANES)[None, :]
    c = np.arange(LANES)[:, None]
    far = np.full((LANES, LANES), REL_BUCKETS - 1, np.int32)
    zero = np.zeros((LANES, LANES), np.float32)
    neg = np.full((LANES, LANES), NEG_INF, np.float32)
    assert _rel_bucket_np(np.arange(LANES + 1, 4 * seq)).min() == REL_BUCKETS - 1
    idx, add = [], []
    for d in range(-WIN_TILES, WIN_TILES + 1):
        if d < 0:
            idx.append(far), add.append(neg)
        elif d == 0:
            idx.append(_rel_bucket_np(r - c)), add.append(np.where(r >= c, zero, neg))
        elif d == 1:
            idx.append(_rel_bucket_np(LANES + r - c)), add.append(zero)
        elif d < WIN_TILES:
            idx.append(far), add.append(zero)
        else:
            idx.append(far), add.append(np.where(c > r, zero, neg))
    n_tile_rows = len(idx) * LANES
    for qt in range(seq // LANES):
        idx.append(_rel_bucket_np(qt * LANES + r - (c * CMP_STRIDE + CMP_BLOCK - 1)))
        add.append(zero)
    idx = np.concatenate(idx, axis=0).astype(np.int32)
    add = np.concatenate(add, axis=0).astype(np.float32)
    shift = (np.arange(idx.shape[0])[:, None] < n_tile_rows).astype(np.float32) * np.ones((1, LANES), np.float32)
    return idx, shift, add


def _bias_kernel(tbl_ref, idx_ref, shift_ref, add_ref, o_ref, *, hpg):
    g = pl.program_id(0)
    idx = idx_ref[...]
    for h in range(hpg):
        head = g * hpg + h
        acc = jnp.zeros(idx.shape, F32)
        for b in range(REL_BUCKETS):
            acc = jnp.where(idx == b, tbl_ref[b, head], acc)
        o_ref[0, :, h * LANES:(h + 1) * LANES] = (acc - shift_ref[...] * tbl_ref[REL_BUCKETS - 1, head]
                                                   + add_ref[...])


def bias_tables(rel_bias, seq):
    heads = rel_bias.shape[1]
    hpg = heads // KV_HEADS
    idx, shift, add = (jnp.asarray(a) for a in _bias_index_tables(seq))
    rows = idx.shape[0]
    full = pl.BlockSpec((rows, LANES), lambda g: (0, 0))
    return pl.pallas_call(
        functools.partial(_bias_kernel, hpg=hpg),
        grid=(KV_HEADS,),
        in_specs=[pl.BlockSpec(memory_space=pltpu.SMEM), full, full, full],
        out_specs=pl.BlockSpec((1, rows, hpg * LANES), lambda g: (g, 0, 0)),
        out_shape=jax.ShapeDtypeStruct((KV_HEADS, rows, hpg * LANES), F32),
        compiler_params=_params(("parallel",), (6 + 2 * hpg + 4) * 4 * rows * LANES),
        name="rel_bias_tables",
    )(rel_bias, idx, shift, add)


def _compress_kernel(k_ref, v_ref, pek_ref, pev_ref, wk_ref, wv_ref, kn_ref, kc_ref, vc_ref, *, nch):
    half = CMP_BLOCK // CMP_STRIDE
    assert half == 2

    def comp(x_ref, pe_ref, w_ref):
        lo = jnp.zeros((nch, HEAD_DIM), F32)
        hi = jnp.zeros((nch, HEAD_DIM), F32)
        for l in range(CMP_STRIDE):
            rows = x_ref[pl.ds(l, nch, stride=CMP_STRIDE), :]
            lo += jnp.dot((rows + pe_ref[l:l + 1, :]).astype(BF16), w_ref[l], preferred_element_type=F32)
            hi += jnp.dot((rows + pe_ref[CMP_STRIDE + l:CMP_STRIDE + l + 1, :]).astype(BF16),
                          w_ref[CMP_STRIDE + l], preferred_element_type=F32)
        return lo + pltpu.roll(hi, nch - 1, 0)

    kc_ref[0, 0] = _rms(comp(k_ref, pek_ref, wk_ref), kn_ref[...])
    vc_ref[0, 0] = comp(v_ref, pev_ref, wv_ref).T


def compress(proj, kc_blk, vc_blk, pe_k, pe_v, w_k, w_v, k_norm, batch, seq):
    nch = seq // CMP_STRIDE
    wspec = pl.BlockSpec((CMP_BLOCK, HEAD_DIM, HEAD_DIM), lambda b, g: (0, 0, 0))
    pespec = pl.BlockSpec((CMP_BLOCK, HEAD_DIM), lambda b, g: (0, 0))
    ospec = pl.BlockSpec((1, 1, nch, HEAD_DIM), lambda b, g: (b, g, 0, 0))
    oshape = jax.ShapeDtypeStruct((batch, KV_HEADS, nch, HEAD_DIM), F32)
    return pl.pallas_call(
        functools.partial(_compress_kernel, nch=nch),
        grid=(batch, KV_HEADS),
        in_specs=[pl.BlockSpec((seq, HEAD_DIM), lambda b, g: (b, kc_blk + g)),
                  pl.BlockSpec((seq, HEAD_DIM), lambda b, g: (b, vc_blk + g)),
                  pespec, pespec, wspec, wspec,
                  pl.BlockSpec((1, HEAD_DIM), lambda b, g: (0, 0))],
        out_specs=[ospec, ospec],
        out_shape=[oshape, oshape],
        compiler_params=_params(("parallel", "parallel"), 4 * 4 * seq * HEAD_DIM + (4 << 20)),
        name="nsa_compress",
    )(proj, proj, pe_k, pe_v, w_k, w_v, k_norm)


SEL_CHUNK = 1024


def _nsa_kernel(q_ref, ks_ref, vs_ref, kw_ref, vw_ref, kc_ref, vct_ref, wt_ref, bc_ref, gt_ref,
                qn_ref, kn_ref, cov_ref, oh_ref, o_ref, ksa, vst, kwn, vwt, *, hpg, n_cmp, n_top):
    qi = pl.program_id(2)
    cols_all = hpg * LANES
    scale = HEAD_DIM ** -0.5
    n_sel_blocks = cov_ref.shape[0]
    chunk_tiles = SEL_CHUNK // LANES
    seq = ks_ref.shape[0]

    @pl.when(qi == 0)
    def _():
        ksa[:, :HEAD_DIM] = _rms(ks_ref[...], kn_ref[...]).astype(BF16)
        ksa[:, HEAD_DIM:] = oh_ref[...]
        kwn[...] = _rms(kw_ref[...], kn_ref[...]).astype(BF16)
        for c in range(seq // LANES):
            blk = slice(c * LANES, (c + 1) * LANES)
            vst[:, blk] = vs_ref[blk, :].T.astype(BF16)
            vwt[:, blk] = vw_ref[blk, :].T.astype(BF16)

    qs = [_rms(q_ref[:, h * HEAD_DIM:(h + 1) * HEAD_DIM], qn_ref[...]).astype(BF16) for h in range(hpg)]
    q = jnp.concatenate(qs, axis=0) if hpg > 1 else qs[0]

    def bias_tile(d):
        off = pl.multiple_of((d + WIN_TILES) * LANES, LANES)
        return wt_ref[0, pl.ds(off, LANES), :]

    blk_row = lax.broadcasted_iota(jnp.int32, (LANES, cols_all), 0)
    qry = lax.broadcasted_iota(jnp.int32, (LANES, cols_all), 1) & (LANES - 1)

    lc = lax.dot_general(kc_ref[0, 0].astype(BF16), q, NT_DIMS, preferred_element_type=F32) * scale + bc_ref[0]
    mc = ((qi * LANES + qry - (blk_row * CMP_STRIDE + CMP_BLOCK - 1)) >= 0) & (blk_row < n_cmp)
    zc = jnp.where(mc, lc, NEG_INF)
    ec = jnp.exp(zc - jnp.max(zc, axis=0, keepdims=True))
    pc = ec * (1.0 / jnp.sum(ec, axis=0, keepdims=True)) * mc.astype(F32)
    o_cmp = jnp.dot(vct_ref[0, 0].astype(BF16), pc.astype(BF16), preferred_element_type=F32)

    ps = pc[:, 0:LANES]
    for h in range(1, hpg):
        ps = ps + pc[:, h * LANES:(h + 1) * LANES]
    p1 = ps.astype(BF16)
    r1 = ps - p1.astype(F32)
    p2 = r1.astype(BF16)
    p3 = (r1 - p2.astype(F32)).astype(BF16)
    cov = cov_ref[...]
    score = (jnp.dot(cov, p1, preferred_element_type=F32) + jnp.dot(cov, p2, preferred_element_type=F32)
             + jnp.dot(cov, p3, preferred_element_type=F32))
    jj = lax.broadcasted_iota(jnp.int32, (n_sel_blocks, LANES), 0)
    ql = lax.broadcasted_iota(jnp.int32, (n_sel_blocks, LANES), 1)
    cur = (LANES // SEL_BLOCK) * qi + ql // SEL_BLOCK
    forced = (jj == 0) | (jj == cur) | (jj == cur - 1)
    score = jnp.where(forced, FORCE_SCORE, jnp.where(jj > cur, -FORCE_SCORE, score))
    rank = jnp.zeros((n_sel_blocks, LANES), F32)
    for j2 in range(n_sel_blocks):
        other = score[j2:j2 + 1, :]
        rank += ((other > score) | ((other == score) & (j2 < jj))).astype(F32)
    sel_t = ((rank < n_top) & (jj <= cur)).astype(BF16)
    sel_t = jnp.concatenate([sel_t, jnp.ones((LANES - n_sel_blocks, LANES), BF16)], axis=0)
    eye = (lax.broadcasted_iota(jnp.int32, (LANES, LANES), 0)
           == lax.broadcasted_iota(jnp.int32, (LANES, LANES), 1)).astype(BF16)
    selq = lax.dot_general(eye, sel_t, NT_DIMS, preferred_element_type=F32)
    sel_neg = ((selq - 1.0) * (-NEG_INF)).astype(BF16)
    sel_neg = jnp.concatenate([sel_neg] * hpg, axis=0) if hpg > 1 else sel_neg
    qa = jnp.concatenate([q, sel_neg], axis=1)

    def sel_chunk(c, carry, near):
        m, l, acc = carry
        off = pl.multiple_of(c * SEL_CHUNK, SEL_CHUNK)
        s = lax.dot_general(ksa[pl.ds(off, SEL_CHUNK), :], qa, NT_DIMS, preferred_element_type=F32) * scale
        if near:
            s = jnp.concatenate([s[u * LANES:(u + 1) * LANES]
                                 + bias_tile(jnp.clip(qi - (c * chunk_tiles + u), 0, 2))
                                 for u in range(chunk_tiles)], axis=0)
        m_new = jnp.maximum(m, jnp.max(s, axis=0, keepdims=True))
        alpha = jnp.exp(m - m_new)
        p = jnp.exp(s - m_new)
        l = alpha * l + jnp.sum(p, axis=0, keepdims=True)
        acc = alpha * acc + jnp.dot(vst[:, pl.ds(off, SEL_CHUNK)], p.astype(BF16), preferred_element_type=F32)
        return m_new, l, acc

    init = (jnp.full((1, cols_all), NEG_INF, F32), jnp.zeros((1, cols_all), F32),
            jnp.zeros((HEAD_DIM, cols_all), F32))
    n_far = (jnp.maximum(qi, 1) - 1) // chunk_tiles
    carry = lax.fori_loop(0, n_far, functools.partial(sel_chunk, near=False), init)
    _, l_s, a_s = lax.fori_loop(n_far, qi // chunk_tiles + 1, functools.partial(sel_chunk, near=True), carry)

    win_keys = WINDOW + LANES
    kt0 = jnp.maximum(qi - WIN_TILES, 0)
    woff = pl.multiple_of(kt0 * LANES, LANES)
    sw = lax.dot_general(kwn[pl.ds(woff, win_keys), :], q, NT_DIMS, preferred_element_type=F32) * scale
    zw = jnp.concatenate([sw[u * LANES:(u + 1) * LANES] + bias_tile(qi - (kt0 + u))
                          for u in range(win_keys // LANES)], axis=0)
    pw = jnp.exp(zw - jnp.max(zw, axis=0, keepdims=True))
    l_w = jnp.sum(pw, axis=0, keepdims=True)
    a_w = jnp.dot(vwt[:, pl.ds(woff, win_keys)], pw.astype(BF16), preferred_element_type=F32)

    gt = jax.nn.sigmoid(gt_ref[...]).T
    inv_s = 1.0 / l_s
    inv_w = 1.0 / l_w
    for h in range(hpg):
        sl = slice(h * LANES, (h + 1) * LANES)
        c0 = h * N_BRANCH
        o = (gt[c0:c0 + 1] * o_cmp[:, sl] + (gt[c0 + 1:c0 + 2] * inv_s[:, sl]) * a_s[:, sl]
             + (gt[c0 + 2:c0 + 3] * inv_w[:, sl]) * a_w[:, sl])
        o_ref[:, h * HEAD_DIM:(h + 1) * HEAD_DIM] = o.T.astype(o_ref.dtype)


def _cover_t(seq):
    nc = seq // CMP_STRIDE - CMP_BLOCK // CMP_STRIDE + 1
    ns = seq // SEL_BLOCK
    c_start = np.arange(nc) * CMP_STRIDE
    c_end = c_start + CMP_BLOCK - 1
    s_start = np.arange(ns) * SEL_BLOCK
    cover = (c_start[:, None] <= s_start[None, :] + SEL_BLOCK - 1) & (c_end[:, None] >= s_start[None, :])
    out = np.zeros((ns, seq // CMP_STRIDE), np.float32)
    out[:, :nc] = cover.T
    return out


def nsa_attention(proj, gates, kc, vc, tables, q_norm, k_norm, batch, seq, hpg, blk):
    heads = KV_HEADS * hpg
    nq = seq // LANES
    nch = seq // CMP_STRIDE
    assert nch == LANES, "compressed keys must fill exactly one lane tile"
    n_cmp = nch - CMP_BLOCK // CMP_STRIDE + 1
    ns = seq // SEL_BLOCK
    assert seq % SEL_CHUNK == 0 and ns <= LANES
    cov = jnp.asarray(_cover_t(seq), BF16)
    onehot = jnp.asarray(np.arange(seq)[:, None] // SEL_BLOCK == np.arange(LANES)[None, :], BF16)
    qw = hpg * HEAD_DIM
    q_blk = blk["q"] * HEAD_DIM // qw
    kv = lambda name: pl.BlockSpec((seq, HEAD_DIM), lambda b, g, i: (b, blk[name] + g))
    cspec = pl.BlockSpec((1, 1, nch, HEAD_DIM), lambda b, g, i: (b, g, 0, 0))
    vec = pl.BlockSpec((1, HEAD_DIM), lambda b, g, i: (0, 0))
    rows_all = hpg * LANES
    est = (2 * 4 * 4 * seq * HEAD_DIM + 5 * 2 * seq * HEAD_DIM + 2 * 2 * seq * LANES
           + 2 * 4 * hpg * (N_BIAS_TILES + 1) * LANES * LANES + 6 * 4 * rows_all * (WINDOW + LANES) + (8 << 20))
    return pl.pallas_call(
        functools.partial(_nsa_kernel, hpg=hpg, n_cmp=n_cmp, n_top=min(N_SELECT, ns)),
        grid=(batch, KV_HEADS, nq),
        in_specs=[pl.BlockSpec((LANES, qw), lambda b, g, i: (b * nq + i, q_blk + g)),
                  kv("ks"), kv("vs"), kv("kw"), kv("vw"), cspec, cspec,
                  pl.BlockSpec((1, N_BIAS_TILES * LANES, qw), lambda b, g, i: (g, 0, 0)),
                  pl.BlockSpec((1, LANES, qw), lambda b, g, i: (g, N_BIAS_TILES + i, 0)),
                  pl.BlockSpec((LANES, LANES), lambda b, g, i: (b * nq + i, g)),
                  vec, vec,
                  pl.BlockSpec((ns, nch), lambda b, g, i: (0, 0)),
                  pl.BlockSpec((seq, LANES), lambda b, g, i: (0, 0))],
        out_specs=pl.BlockSpec((LANES, qw), lambda b, g, i: (b * nq + i, g)),
        out_shape=jax.ShapeDtypeStruct((batch * seq, heads * HEAD_DIM), BF16),
        scratch_shapes=[pltpu.VMEM((seq, 2 * HEAD_DIM), BF16), pltpu.VMEM((HEAD_DIM, seq), BF16),
                        pltpu.VMEM((seq, HEAD_DIM), BF16), pltpu.VMEM((HEAD_DIM, seq), BF16)],
        compiler_params=_params(("parallel", "parallel", "arbitrary"), est),
        name="nsa_attention",
    )(proj, proj, proj, proj, proj, kc, vc, tables, tables, gates, q_norm, k_norm, cov, onehot)


def _pool_kernel(x_ref, halo_ref, gm_ref, wp_ref, ps_ref, gf_ref, wrh_ref, wrl_ref, x_out, route_out,
                 *, tq, seq, cg):
    i = pl.program_id(0)
    start = (i * tq) % seq
    x = x_ref[...]
    h = _rms(x, gm_ref[...])
    hh = jnp.where(start == 0, 0.0, _rms(halo_ref[...], gm_ref[...]))
    t1 = (start + lax.broadcasted_iota(jnp.int32, (tq, 1), 0) + 1).astype(F32)
    ys = []
    for gi, w in enumerate(POOL_WINDOWS):
        sl = slice(gi * cg, (gi + 1) * cg)
        s = jnp.concatenate([hh[:, sl], h[:, sl]], axis=0)
        span = 1
        while span < w:
            s = s + pltpu.roll(s, span, 0)
            span *= 2
        dm = s[POOL_HALO:] * (1.0 / jnp.minimum(t1, float(w))) - h[:, sl]
        ys.append(jnp.dot(dm.astype(BF16), wp_ref[gi], preferred_element_type=F32))
    x3 = x + jnp.concatenate(ys, axis=1) * ps_ref[...]
    x_out[...] = x3

    h4 = _rms(x3, gf_ref[...])
    h_hi = h4.astype(BF16)
    h_lo = (h4 - h_hi.astype(F32)).astype(BF16)
    logits = (jnp.dot(h_hi, wrh_ref[...], preferred_element_type=F32)
              + jnp.dot(h_hi, wrl_ref[...], preferred_element_type=F32)
              + jnp.dot(h_lo, wrh_ref[...], preferred_element_type=F32))
    lane = lax.broadcasted_iota(jnp.int32, (tq, LANES), 1)
    logits = jnp.where(lane < N_EXPERTS, logits, -jnp.inf)
    m1 = jnp.max(logits, axis=-1, keepdims=True)
    i1 = jnp.min(jnp.where(logits == m1, lane, LANES), axis=-1, keepdims=True)
    rest = jnp.where(lane == i1, -jnp.inf, logits)
    m2 = jnp.max(rest, axis=-1, keepdims=True)
    i2 = jnp.min(jnp.where(rest == m2, lane, LANES), axis=-1, keepdims=True)
    e2 = jnp.exp(m2 - m1)
    den = 1.0 + e2
    route = jnp.where(lane == 0, i1.astype(F32),
                      jnp.where(lane == 1, i2.astype(F32),
                                jnp.where(lane == 2, 1.0 / den, jnp.where(lane == 3, e2 / den, 0.0))))
    route_out[...] = route


def pool_and_route(x, g_mix, w_pool, pool_scale, g_ffn, w_router, seq, tq=256):
    m, d = x.shape
    cg = d // len(POOL_WINDOWS)
    tq = _pick(seq, tq, POOL_HALO)
    hb = tq // POOL_HALO
    vec = pl.BlockSpec((1, d), lambda i: (0, 0))
    wr = _pad_to(w_router, 1, LANES)
    wr_hi = wr.astype(BF16)
    wr_lo = (wr - wr_hi.astype(F32)).astype(BF16)
    wspec = pl.BlockSpec((d, LANES), lambda i: (0, 0))
    est = 2 * 2 * 4 * tq * d + 2 * 2 * len(POOL_WINDOWS) * cg * cg + 8 * 4 * tq * d + 4 * 2 * d * LANES
    return pl.pallas_call(
        functools.partial(_pool_kernel, tq=tq, seq=seq, cg=cg),
        grid=(m // tq,),
        in_specs=[pl.BlockSpec((tq, d), lambda i: (i, 0)),
                  pl.BlockSpec((POOL_HALO, d), lambda i: (jnp.maximum(i * hb - 1, 0), 0)),
                  vec,
                  pl.BlockSpec((len(POOL_WINDOWS), cg, cg), lambda i: (0, 0, 0)),
                  vec, vec, wspec, wspec],
        out_specs=[pl.BlockSpec((tq, d), lambda i: (i, 0)), pl.BlockSpec((tq, LANES), lambda i: (i, 0))],
        out_shape=[jax.ShapeDtypeStruct((m, d), F32), jax.ShapeDtypeStruct((m, LANES), F32)],
        compiler_params=_params(("parallel",), est),
        name="pool_mixer_router",
    )(x, x, g_mix, w_pool, pool_scale, g_ffn, wr_hi, wr_lo)


def _route_tables(route, tm, n_tiles):
    n = route.shape[0]
    i1 = route[:, 0].astype(jnp.int32)
    i2 = route[:, 1].astype(jnp.int32)
    onehot = jax.nn.one_hot(i1, N_EXPERTS, dtype=jnp.int32) + jax.nn.one_hot(i2, N_EXPERTS, dtype=jnp.int32)
    count = jnp.sum(onehot, axis=0)
    padded = (count + tm - 1) // tm * tm
    end = jnp.cumsum(padded)
    pos = (end - padded)[None, :] + jnp.cumsum(onehot, axis=0) - onehot
    p1 = jnp.take_along_axis(pos, i1[:, None], axis=1)[:, 0]
    p2 = jnp.take_along_axis(pos, i2[:, None], axis=1)[:, 0]
    tok = jnp.arange(n, dtype=jnp.int32)
    rows = n_tiles * tm
    row_token = jnp.zeros((rows,), jnp.int32).at[jnp.concatenate([p1, p2])].set(jnp.concatenate([tok, tok]))
    n_used = (end[-1] // tm).astype(jnp.int32)
    tile = jnp.arange(n_tiles, dtype=jnp.int32)
    tile_expert = jnp.searchsorted(end, tile * tm, side="right").astype(jnp.int32)
    tile_expert = jnp.where(tile < n_used, tile_expert, tile_expert[n_used - 1])
    tile_rows = jnp.clip((count - padded + end)[tile_expert] - tile * tm, 0, tm)
    tile_rows = jnp.where(tile < n_used, tile_rows, 0).astype(jnp.int32)
    ids = jnp.arange(N_EXPERTS, dtype=jnp.int32)
    later = jnp.where((ids[None, :] > ids[:, None]) & (count[None, :] > 0), ids[None, :], N_EXPERTS)
    following = jnp.min(later, axis=1)
    next_expert = jnp.where(following < N_EXPERTS, following, -1)[tile_expert].astype(jnp.int32)
    return (tile_expert, tile_rows, next_expert, row_token, n_used.reshape(1),
            p1.astype(jnp.int32), p2.astype(jnp.int32))


def _gather_start(idx_ref, base, src_hbm, dst_ref, sem, count):
    def issue(r, carry):
        pltpu.make_async_copy(src_hbm.at[pl.ds(idx_ref[base + r], 1), :], dst_ref.at[pl.ds(r, 1), :], sem).start()
        return carry

    lax.fori_loop(0, count, issue, 0, unroll=8)


def _gather_wait(src_hbm, dst_ref, sem, count):
    pltpu.make_async_copy(src_hbm.at[pl.ds(0, count), :], dst_ref, sem).wait()


MOE_SUB = 256


def _moe_gather_kernel(rt_ref, nu_ref, x_hbm, g_ref, o_ref, xg_ref, sem, *, tm):
    i = pl.program_id(0)
    valid = i < nu_ref[0]
    slot = i % 2

    @pl.when(i == 0)
    def _():
        _gather_start(rt_ref, 0, x_hbm, xg_ref.at[0], sem.at[0], tm)

    @pl.when(i + 1 < nu_ref[0])
    def _():
        _gather_start(rt_ref, (i + 1) * tm, x_hbm, xg_ref.at[1 - slot], sem.at[1 - slot], tm)

    @pl.when(valid)
    def _():
        _gather_wait(x_hbm, xg_ref.at[slot], sem.at[slot], tm)
        o_ref[...] = _rms(xg_ref[slot], g_ref[...]).astype(o_ref.dtype)

    @pl.when(jnp.logical_not(valid))
    def _():
        o_ref[...] = jnp.zeros_like(o_ref)


def moe_gather(x, g_ffn, row_token, n_used, tm):
    d = x.shape[1]
    n_tiles = row_token.shape[0] // tm
    return pl.pallas_call(
        functools.partial(_moe_gather_kernel, tm=tm),
        grid_spec=pltpu.PrefetchScalarGridSpec(
            num_scalar_prefetch=2,
            grid=(n_tiles,),
            in_specs=[pl.BlockSpec(memory_space=pl.ANY), pl.BlockSpec((1, d), lambda i, rt, nu: (0, 0))],
            out_specs=pl.BlockSpec((tm, d), lambda i, rt, nu: (i, 0)),
            scratch_shapes=[pltpu.VMEM((2, tm, d), F32), pltpu.SemaphoreType.DMA((2,))]),
        out_shape=jax.ShapeDtypeStruct((n_tiles * tm, d), BF16),
        compiler_params=_params(("arbitrary",), 2 * 4 * tm * d + 2 * 2 * tm * d + 2 * 4 * tm * d),
        name="moe_gather",
    )(row_token, n_used, x, g_ffn)


def _stage_expert_weights(te_ref, nu_ref, nx_ref, w_hbm, stage, work, sem, width):
    sweep = pl.program_id(0)
    i = pl.program_id(1)

    def copies(e, col_tile):
        cols = pl.ds(pl.multiple_of(col_tile * width, width), width)
        return [pltpu.make_async_copy(w.at[e, :, cols], s, sem.at[k]) for k, (w, s) in enumerate(zip(w_hbm, stage))]

    def start(e, col_tile):
        for c in copies(e, col_tile):
            c.start()

    @pl.when((sweep == 0) & (i == 0))
    def _():
        start(te_ref[0], 0)

    @pl.when((i < nu_ref[0]) & ((i == 0) | (te_ref[i] != te_ref[jnp.maximum(i - 1, 0)])))
    def _():
        for c in copies(0, 0):
            c.wait()

        def cast_rows(r, carry):
            rows = pl.ds(pl.multiple_of(r * MOE_SUB, MOE_SUB), MOE_SUB)
            for s, b in zip(stage, work):
                b[rows, :] = s[rows, :].astype(BF16)
            return carry

        lax.fori_loop(0, stage[0].shape[0] // MOE_SUB, cast_rows, 0)
        nxt = nx_ref[i]

        @pl.when(nxt >= 0)
        def _():
            start(nxt, sweep)

        @pl.when((nxt < 0) & (sweep + 1 < pl.num_programs(0)))
        def _():
            start(te_ref[0], sweep + 1)


def _moe_glu_kernel(te_ref, tr_ref, nu_ref, nx_ref, h_ref, w1_hbm, w3_hbm, o_ref, s1, s3, w1b, w3b, sem, *, tm, tf):
    i = pl.program_id(1)
    valid = i < nu_ref[0]
    _stage_expert_weights(te_ref, nu_ref, nx_ref, [w1_hbm, w3_hbm], [s1, s3], [w1b, w3b], sem, tf)

    for sb in range(tm // MOE_SUB):
        rows = pl.ds(sb * MOE_SUB, MOE_SUB)
        live = valid & (sb * MOE_SUB < tr_ref[i])

        @pl.when(live)
        def _():
            h = h_ref[rows, :]
            a = jnp.dot(h, w1b[...], preferred_element_type=F32)
            b = jnp.dot(h, w3b[...], preferred_element_type=F32)
            o_ref[rows, :] = (a * jax.nn.sigmoid(a) * b).astype(o_ref.dtype)

        @pl.when(jnp.logical_not(live))
        def _():
            o_ref[rows, :] = jnp.zeros((MOE_SUB, o_ref.shape[1]), o_ref.dtype)


def moe_glu(h, w1, w3, tile_expert, tile_rows, n_used, next_expert, tm, tf):
    d = h.shape[1]
    ff = w1.shape[2]
    tf = _pick(ff, tf)
    n_tiles = tile_expert.shape[0]
    est = 2 * 4 * d * tf + 2 * 2 * d * tf + 2 * 2 * tm * d + 2 * 2 * tm * tf + 4 * 4 * MOE_SUB * tf + 4 * d * tf
    return pl.pallas_call(
        functools.partial(_moe_glu_kernel, tm=tm, tf=tf),
        grid_spec=pltpu.PrefetchScalarGridSpec(
            num_scalar_prefetch=4,
            grid=(ff // tf, n_tiles),
            in_specs=[pl.BlockSpec((tm, d), lambda f, i, te, tr, nu, nx: (jnp.minimum(i, nu[0] - 1), 0)),
                      pl.BlockSpec(memory_space=pl.ANY), pl.BlockSpec(memory_space=pl.ANY)],
            out_specs=pl.BlockSpec((tm, tf), lambda f, i, te, tr, nu, nx: (i, f)),
            scratch_shapes=[pltpu.VMEM((d, tf), F32), pltpu.VMEM((d, tf), F32),
                            pltpu.VMEM((d, tf), BF16), pltpu.VMEM((d, tf), BF16),
                            pltpu.SemaphoreType.DMA((2,))]),
        out_shape=jax.ShapeDtypeStruct((n_tiles * tm, ff), BF16),
        compiler_params=_params(("arbitrary", "arbitrary"), est),
        name="moe_glu",
    )(tile_expert, tile_rows, n_used, next_expert, h, w1, w3)


def _moe_down_kernel(te_ref, tr_ref, nu_ref, nx_ref, a_ref, w_hbm, o_ref, s2, w2b, sem, *, tm, tn):
    i = pl.program_id(1)
    valid = i < nu_ref[0]
    _stage_expert_weights(te_ref, nu_ref, nx_ref, [w_hbm], [s2], [w2b], sem, tn)
    for sb in range(tm // MOE_SUB):
        rows = pl.ds(sb * MOE_SUB, MOE_SUB)
        live = valid & (sb * MOE_SUB < tr_ref[i])

        @pl.when(live)
        def _():
            o_ref[rows, :] = jnp.dot(a_ref[rows, :], w2b[...], preferred_element_type=F32)

        @pl.when(jnp.logical_not(live))
        def _():
            o_ref[rows, :] = jnp.zeros((MOE_SUB, o_ref.shape[1]), o_ref.dtype)


def moe_down(act, w2, tile_expert, tile_rows, n_used, next_expert, tm, tn):
    ff = act.shape[1]
    d = w2.shape[2]
    tn = _pick(d, tn)
    n_tiles = tile_expert.shape[0]
    est = 4 * ff * tn + 2 * ff * tn + 2 * 2 * tm * ff + 2 * 4 * tm * tn + 4 * MOE_SUB * tn + 4 * MOE_SUB * tn
    return pl.pallas_call(
        functools.partial(_moe_down_kernel, tm=tm, tn=tn),
        grid_spec=pltpu.PrefetchScalarGridSpec(
            num_scalar_prefetch=4,
            grid=(d // tn, n_tiles),
            in_specs=[pl.BlockSpec((tm, ff), lambda j, i, te, tr, nu, nx: (jnp.minimum(i, nu[0] - 1), 0)),
                      pl.BlockSpec(memory_space=pl.ANY)],
            out_specs=pl.BlockSpec((tm, tn), lambda j, i, te, tr, nu, nx: (i, j)),
            scratch_shapes=[pltpu.VMEM((ff, tn), F32), pltpu.VMEM((ff, tn), BF16),
                            pltpu.SemaphoreType.DMA((1,))]),
        out_shape=jax.ShapeDtypeStruct((n_tiles * tm, d), F32),
        compiler_params=_params(("arbitrary", "arbitrary"), est),
        name="moe_down",
    )(tile_expert, tile_rows, n_used, next_expert, act, w2)


def _moe_combine_kernel(p1_ref, p2_ref, x_ref, r_ref, y_hbm, o_ref, ya_ref, yb_ref, sem_a, sem_b, *, tq):
    i = pl.program_id(0)
    slot = i % 2

    def start(step, s):
        _gather_start(p1_ref, step * tq, y_hbm, ya_ref.at[s], sem_a.at[s], tq)
        _gather_start(p2_ref, step * tq, y_hbm, yb_ref.at[s], sem_b.at[s], tq)

    @pl.when(i == 0)
    def _():
        start(0, 0)

    @pl.when(i + 1 < pl.num_programs(0))
    def _():
        start(i + 1, 1 - slot)

    _gather_wait(y_hbm, ya_ref.at[slot], sem_a.at[slot], tq)
    _gather_wait(y_hbm, yb_ref.at[slot], sem_b.at[slot], tq)
    r = r_ref[...]
    o_ref[...] = x_ref[...] + (r[:, 2:3] * ya_ref[slot] + r[:, 3:4] * yb_ref[slot])


def moe_combine(x, route, y, p1, p2, tq=256):
    m, d = x.shape
    tq = _pick(m, tq, 8)
    return pl.pallas_call(
        functools.partial(_moe_combine_kernel, tq=tq),
        grid_spec=pltpu.PrefetchScalarGridSpec(
            num_scalar_prefetch=2,
            grid=(m // tq,),
            in_specs=[pl.BlockSpec((tq, d), lambda i, a, b: (i, 0)),
                      pl.BlockSpec((tq, LANES), lambda i, a, b: (i, 0)),
                      pl.BlockSpec(memory_space=pl.ANY)],
            out_specs=pl.BlockSpec((tq, d), lambda i, a, b: (i, 0)),
            scratch_shapes=[pltpu.VMEM((2, tq, d), F32), pltpu.VMEM((2, tq, d), F32),
                            pltpu.SemaphoreType.DMA((2,)), pltpu.SemaphoreType.DMA((2,))]),
        out_shape=jax.ShapeDtypeStruct((m, d), F32),
        compiler_params=_params(("arbitrary",), 9 * 4 * tq * d),
        name="moe_combine",
    )(p1, p2, x, route, y)


def _pad_to(a, axis, size):
    if a.shape[axis] == size:
        return a
    pad = [(0, 0)] * a.ndim
    pad[axis] = (0, size - a.shape[axis])
    return jnp.pad(a, pad)


def _even_layer(x2d, batch, seq, tables, norm_mix, w_in, conv_w, q_norm, k_norm, pe_k, pe_v, w_cmp_k, w_cmp_v,
                w_out, norm_ffn, w1, w3, w2):
    d = x2d.shape[1]
    ch = d // 2
    heads = ch // HEAD_DIM
    hpg = heads // KV_HEADS
    kv_w = KV_HEADS * HEAD_DIM
    main = 3 * ch + heads * HEAD_DIM + 6 * kv_w
    assert main + heads * N_BRANCH == w_in.shape[1]

    h = rmsnorm(x2d, norm_mix, BF16)
    proj = matmul([(h, w_in.astype(BF16), 0)], None, F32, 1024, 1024, "in_proj", n=main)
    wg = w_in[:, main:].reshape(d, KV_HEADS, hpg * N_BRANCH)
    wg = _pad_to(wg, 2, LANES).reshape(d, KV_HEADS * LANES).astype(BF16)
    gates = matmul([(h, wg, 0)], None, F32, 1024, KV_HEADS * LANES, "gate_proj")

    y_a = short_conv(proj, conv_w.T, ch, seq)
    base = (3 * ch + heads * HEAD_DIM) // HEAD_DIM
    step = kv_w // HEAD_DIM
    blk = {"q": 3 * ch // HEAD_DIM, "kc": base, "vc": base + step, "ks": base + 2 * step,
           "vs": base + 3 * step, "kw": base + 4 * step, "vw": base + 5 * step}
    kn = k_norm.reshape(1, HEAD_DIM)
    kc, vc = compress(proj, blk["kc"], blk["vc"], pe_k, pe_v, w_cmp_k.astype(BF16), w_cmp_v.astype(BF16),
                      kn, batch, seq)
    y_b = nsa_attention(proj, gates, kc, vc, tables, q_norm.reshape(1, HEAD_DIM), kn, batch, seq, hpg, blk)

    wo = w_out.astype(BF16)
    x2d = matmul([(y_a, wo, 0), (y_b, wo, 1)], x2d, F32, 1024, 512, "out_proj")

    h2 = rmsnorm(x2d, norm_ffn, BF16)
    act = glu_matmul(h2, w1, w3, 1024, 256)
    return matmul([(act, w2.astype(BF16), 0)], x2d, F32, 512, 512, "dense_down")


def _odd_layer(x2d, seq, norm_mix, w_pool, pool_scale, norm_ffn, w_router, w1, w3, w2):
    m, d = x2d.shape
    x3, route = pool_and_route(x2d, norm_mix.reshape(1, d), w_pool.astype(BF16), pool_scale.reshape(1, d),
                               norm_ffn.reshape(1, d), w_router, seq)
    tm = min(512, m // 8)
    n_tiles = 2 * m // tm + N_EXPERTS
    tile_expert, tile_rows, next_expert, row_token, n_used, p1, p2 = _route_tables(route, tm, n_tiles)
    hs = moe_gather(x3, norm_ffn.reshape(1, d), row_token, n_used, tm)
    act = moe_glu(hs, w1, w3, tile_expert, tile_rows, n_used, next_expert, tm, 512)
    y = moe_down(act, w2, tile_expert, tile_rows, n_used, next_expert, tm, 512)
    return moe_combine(x3, route, y, p1, p2)


def kernel(x, norm_mix_even, w_in_even, conv_w_even, q_norm_even, k_norm_even, cmp_pe_k_even, cmp_pe_v_even,
           w_cmp_k_even, w_cmp_v_even, w_out_even, norm_ffn_even, w1_dense, w3_dense, w2_dense,
           norm_mix_odd, w_pool_odd, pool_scale_odd, norm_ffn_odd, w_router_odd,
           w1_moe, w3_moe, w2_moe, rel_bias):
    batch, seq, d = x.shape
    depth = norm_mix_even.shape[0] + norm_mix_odd.shape[0]
    tables = bias_tables(rel_bias, seq)
    x2d = x.reshape(batch * seq, d)
    for layer in range(depth):
        i = layer // 2
        if layer % 2 == 0:
            x2d = _even_layer(x2d, batch, seq, tables, norm_mix_even[i], w_in_even[i], conv_w_even[i],
                              q_norm_even[i], k_norm_even[i], cmp_pe_k_even[i], cmp_pe_v_even[i],
                              w_cmp_k_even[i], w_cmp_v_even[i], w_out_even[i], norm_ffn_even[i],
                              w1_dense[i], w3_dense[i], w2_dense[i])
        else:
            x2d = _odd_layer(x2d, seq, norm_mix_odd[i], w_pool_odd[i], pool_scale_odd[i], norm_ffn_odd[i],
                             w_router_odd[i], w1_moe[i], w3_moe[i], w2_moe[i])
    return x2d.reshape(batch, seq, d)
```

```python
import functools
import math

import numpy as np
import jax
import jax.numpy as jnp
from jax import lax
from jax.experimental import pallas as pl
from jax.experimental.pallas import tpu as pltpu

F32 = jnp.float32
BF16 = jnp.bfloat16

HEAD_DIM = 128
KV_HEADS = 4
CONV_TAPS = 3
N_BRANCH = 3
CMP_BLOCK = 32
CMP_STRIDE = 16
SEL_BLOCK = 64
N_SELECT = 16
WINDOW = 512
FORCE_SCORE = 1e6
NEG_INF = -1e30
REL_BUCKETS = 32
REL_MAX_DIST = 128
POOL_WINDOWS = (2, 4, 8, 16)
N_EXPERTS = 8
EPS = 1e-6

LANES = 128
POOL_HALO = 16
CONV_HALO = 8
VMEM_CAP = 60 * 1024 * 1024
NT_DIMS = (((1,), (1,)), ((), ()))


def _pick(n, pref, mult=LANES):
    t = min(pref, n)
    while n % t or t % mult:
        t -= mult
    return t


def _params(sem, est_bytes):
    limit = int(min(max(est_bytes * 5 // 4 + (4 << 20), 16 << 20), VMEM_CAP))
    return pltpu.CompilerParams(dimension_semantics=sem, vmem_limit_bytes=limit)


def _rms(x, g):
    return x * lax.rsqrt(jnp.mean(x * x, axis=-1, keepdims=True) + EPS) * g


def _rmsnorm_kernel(x_ref, g_ref, o_ref):
    o_ref[...] = _rms(x_ref[...], g_ref[...]).astype(o_ref.dtype)


def rmsnorm(x, g, out_dtype, tm=256):
    m, d = x.shape
    tm = _pick(m, tm, 8)
    return pl.pallas_call(
        _rmsnorm_kernel,
        grid=(m // tm,),
        in_specs=[pl.BlockSpec((tm, d), lambda i: (i, 0)), pl.BlockSpec((1, d), lambda i: (0, 0))],
        out_specs=pl.BlockSpec((tm, d), lambda i: (i, 0)),
        out_shape=jax.ShapeDtypeStruct((m, d), out_dtype),
        compiler_params=_params(("parallel",), 2 * tm * d * 8),
        name="rmsnorm",
    )(x, g.reshape(1, d))


def _mm_kernel(*refs, n_pairs, has_res):
    o_ref = refs[-1]
    acc = None
    for p in range(n_pairs):
        d = jnp.dot(refs[2 * p][...], refs[2 * p + 1][...], preferred_element_type=F32)
        acc = d if acc is None else acc + d
    if has_res:
        acc = refs[2 * n_pairs][...] + acc
    o_ref[...] = acc.astype(o_ref.dtype)


def matmul(pairs, res, out_dtype, tm, tn, name, n=None):
    m = pairs[0][0].shape[0]
    n = pairs[0][1].shape[1] if n is None else n
    tm = _pick(m, tm)
    tn = _pick(n, tn)
    in_specs, args, est = [], [], 0
    for x, w, r in pairs:
        k = x.shape[1]
        in_specs += [pl.BlockSpec((tm, k), lambda i, j: (i, 0)), pl.BlockSpec((k, tn), lambda i, j, r=r: (r, j))]
        args += [x, w]
        est += 2 * 2 * (tm * k + k * tn)
    if res is not None:
        in_specs.append(pl.BlockSpec((tm, tn), lambda i, j: (i, j)))
        args.append(res)
        est += 2 * 4 * tm * tn
    est += 2 * 4 * tm * tn + 4 * tm * tn
    return pl.pallas_call(
        functools.partial(_mm_kernel, n_pairs=len(pairs), has_res=res is not None),
        grid=(m // tm, n // tn),
        in_specs=in_specs,
        out_specs=pl.BlockSpec((tm, tn), lambda i, j: (i, j)),
        out_shape=jax.ShapeDtypeStruct((m, n), out_dtype),
        compiler_params=_params(("parallel", "parallel"), est),
        name=name,
    )(*args)


def _glu_kernel(h_ref, w1_ref, w3_ref, o_ref):
    h = h_ref[...]
    a = jnp.dot(h, w1_ref[...].astype(BF16), preferred_element_type=F32)
    b = jnp.dot(h, w3_ref[...].astype(BF16), preferred_element_type=F32)
    o_ref[...] = (a * jax.nn.sigmoid(a) * b).astype(o_ref.dtype)


def glu_matmul(h, w1, w3, tm, tn):
    m, k = h.shape
    n = w1.shape[1]
    tm = _pick(m, tm)
    tn = _pick(n, tn)
    est = 2 * 2 * (tm * k + tm * tn) + 2 * 2 * 4 * k * tn + 2 * 2 * k * tn + 3 * 4 * tm * tn
    return pl.pallas_call(
        _glu_kernel,
        grid=(m // tm, n // tn),
        in_specs=[pl.BlockSpec((tm, k), lambda i, j: (i, 0)),
                  pl.BlockSpec((k, tn), lambda i, j: (0, j)),
                  pl.BlockSpec((k, tn), lambda i, j: (0, j))],
        out_specs=pl.BlockSpec((tm, tn), lambda i, j: (i, j)),
        out_shape=jax.ShapeDtypeStruct((m, n), BF16),
        compiler_params=_params(("parallel", "parallel"), est),
        name="dense_glu",
    )(h, w1, w3)


def _conv_kernel(ab_ref, ac_ref, au_ref, hc_ref, hu_ref, w_ref, o_ref, *, tq, seq):
    i = pl.program_id(0)
    first = (i * tq) % seq == 0
    v = ac_ref[...] * au_ref[...]
    hv = jnp.where(first, 0.0, hc_ref[...] * hu_ref[...])
    rows = lax.broadcasted_iota(jnp.int32, v.shape, 0)
    v1 = jnp.where(rows == 0, hv[CONV_HALO - 1:CONV_HALO], pltpu.roll(v, 1, 0))
    v2 = jnp.where(rows == 0, hv[CONV_HALO - 2:CONV_HALO - 1],
                   jnp.where(rows == 1, hv[CONV_HALO - 1:CONV_HALO], pltpu.roll(v, 2, 0)))
    y = w_ref[0:1, :] * v2 + w_ref[1:2, :] * v1 + w_ref[2:3, :] * v
    o_ref[...] = (ab_ref[...] * y).astype(o_ref.dtype)


def short_conv(proj, conv_w_t, ch, seq, tq=512, tc=1024):
    m = proj.shape[0]
    tq = _pick(seq, tq, CONV_HALO)
    tc = _pick(ch, tc)
    nj = ch // tc
    hb = tq // CONV_HALO
    halo = lambda off: pl.BlockSpec((CONV_HALO, tc), lambda i, j: (jnp.maximum(i * hb - 1, 0), off + j))
    return pl.pallas_call(
        functools.partial(_conv_kernel, tq=tq, seq=seq),
        grid=(m // tq, nj),
        in_specs=[pl.BlockSpec((tq, tc), lambda i, j: (i, j)),
                  pl.BlockSpec((tq, tc), lambda i, j: (i, nj + j)),
                  pl.BlockSpec((tq, tc), lambda i, j: (i, 2 * nj + j)),
                  halo(nj), halo(2 * nj),
                  pl.BlockSpec((CONV_TAPS, tc), lambda i, j: (0, j))],
        out_specs=pl.BlockSpec((tq, tc), lambda i, j: (i, j)),
        out_shape=jax.ShapeDtypeStruct((m, ch), BF16),
        compiler_params=_params(("parallel", "parallel"), 2 * 4 * 4 * tq * tc + 6 * 4 * tq * tc),
        name="short_conv",
    )(proj, proj, proj, proj, proj, conv_w_t)


def _rel_bucket_np(dist):
    dist = np.maximum(dist, 0)
    exact = REL_BUCKETS // 2
    d = np.maximum(dist, exact).astype(np.float32)
    large = exact + (np.log(d / np.float32(exact)) / np.float32(math.log(REL_MAX_DIST / exact))
                     * np.float32(REL_BUCKETS - exact)).astype(np.int32)
    return np.where(dist < exact, dist, np.minimum(large, REL_BUCKETS - 1)).astype(np.int32)


WIN_TILES = WINDOW // LANES
N_BIAS_TILES = 2 * WIN_TILES + 1


def _bias_index_tables(seq):
    r = np.arange(LANES)[None, :]
    c = np.arange(LANES)[:, None]
    far = np.full((LANES, LANES), REL_BUCKETS - 1, np.int32)
    zero = np.zeros((LANES, LANES), np.float32)
    neg = np.full((LANES, LANES), NEG_INF, np.float32)
    assert _rel_bucket_np(np.arange(LANES + 1, 4 * seq)).min() == REL_BUCKETS - 1
    idx, add = [], []
    for d in range(-WIN_TILES, WIN_TILES + 1):
        if d < 0:
            idx.append(far), add.append(neg)
        elif d == 0:
            idx.append(_rel_bucket_np(r - c)), add.append(np.where(r >= c, zero, neg))
        elif d == 1:
            idx.append(_rel_bucket_np(LANES + r - c)), add.append(zero)
        elif d < WIN_TILES:
            idx.append(far), add.append(zero)
        else:
            idx.append(far), add.append(np.where(c > r, zero, neg))
    n_tile_rows = len(idx) * LANES
    for qt in range(seq // LANES):
        idx.append(_rel_bucket_np(qt * LANES + r - (c * CMP_STRIDE + CMP_BLOCK - 1)))
        add.append(zero)
    idx = np.concatenate(idx, axis=0).astype(np.int32)
    add = np.concatenate(add, axis=0).astype(np.float32)
    shift = (np.arange(idx.shape[0])[:, None] < n_tile_rows).astype(np.float32) * np.ones((1, LANES), np.float32)
    return idx, shift, add


def _bias_kernel(tbl_ref, idx_ref, shift_ref, add_ref, o_ref, *, hpg):
    g = pl.program_id(0)
    idx = idx_ref[...]
    for h in range(hpg):
        head = g * hpg + h
        acc = jnp.zeros(idx.shape, F32)
        for b in range(REL_BUCKETS):
            acc = jnp.where(idx == b, tbl_ref[b, head], acc)
        o_ref[0, :, h * LANES:(h + 1) * LANES] = (acc - shift_ref[...] * tbl_ref[REL_BUCKETS - 1, head]
                                                   + add_ref[...])


def bias_tables(rel_bias, seq):
    heads = rel_bias.shape[1]
    hpg = heads // KV_HEADS
    idx, shift, add = (jnp.asarray(a) for a in _bias_index_tables(seq))
    rows = idx.shape[0]
    full = pl.BlockSpec((rows, LANES), lambda g: (0, 0))
    return pl.pallas_call(
        functools.partial(_bias_kernel, hpg=hpg),
        grid=(KV_HEADS,),
        in_specs=[pl.BlockSpec(memory_space=pltpu.SMEM), full, full, full],
        out_specs=pl.BlockSpec((1, rows, hpg * LANES), lambda g: (g, 0, 0)),
        out_shape=jax.ShapeDtypeStruct((KV_HEADS, rows, hpg * LANES), F32),
        compiler_params=_params(("parallel",), (6 + 2 * hpg + 4) * 4 * rows * LANES),
        name="rel_bias_tables",
    )(rel_bias, idx, shift, add)


def _compress_kernel(k_ref, v_ref, pek_ref, pev_ref, wk_ref, wv_ref, kn_ref, kc_ref, vc_ref, *, nch):
    half = CMP_BLOCK // CMP_STRIDE
    assert half == 2

    def comp(x_ref, pe_ref, w_ref):
        lo = jnp.zeros((nch, HEAD_DIM), F32)
        hi = jnp.zeros((nch, HEAD_DIM), F32)
        for l in range(CMP_STRIDE):
            rows = x_ref[pl.ds(l, nch, stride=CMP_STRIDE), :]
            lo += jnp.dot((rows + pe_ref[l:l + 1, :]).astype(BF16), w_ref[l], preferred_element_type=F32)
            hi += jnp.dot((rows + pe_ref[CMP_STRIDE + l:CMP_STRIDE + l + 1, :]).astype(BF16),
                          w_ref[CMP_STRIDE + l], preferred_element_type=F32)
        return lo + pltpu.roll(hi, nch - 1, 0)

    kc_ref[0, 0] = _rms(comp(k_ref, pek_ref, wk_ref), kn_ref[...])
    vc_ref[0, 0] = comp(v_ref, pev_ref, wv_ref).T


def compress(proj, kc_blk, vc_blk, pe_k, pe_v, w_k, w_v, k_norm, batch, seq):
    nch = seq // CMP_STRIDE
    wspec = pl.BlockSpec((CMP_BLOCK, HEAD_DIM, HEAD_DIM), lambda b, g: (0, 0, 0))
    pespec = pl.BlockSpec((CMP_BLOCK, HEAD_DIM), lambda b, g: (0, 0))
    ospec = pl.BlockSpec((1, 1, nch, HEAD_DIM), lambda b, g: (b, g, 0, 0))
    oshape = jax.ShapeDtypeStruct((batch, KV_HEADS, nch, HEAD_DIM), F32)
    return pl.pallas_call(
        functools.partial(_compress_kernel, nch=nch),
        grid=(batch, KV_HEADS),
        in_specs=[pl.BlockSpec((seq, HEAD_DIM), lambda b, g: (b, kc_blk + g)),
                  pl.BlockSpec((seq, HEAD_DIM), lambda b, g: (b, vc_blk + g)),
                  pespec, pespec, wspec, wspec,
                  pl.BlockSpec((1, HEAD_DIM), lambda b, g: (0, 0))],
        out_specs=[ospec, ospec],
        out_shape=[oshape, oshape],
        compiler_params=_params(("parallel", "parallel"), 4 * 4 * seq * HEAD_DIM + (4 << 20)),
        name="nsa_compress",
    )(proj, proj, pe_k, pe_v, w_k, w_v, k_norm)


SEL_CHUNK = 1024


def _nsa_kernel(q_ref, ks_ref, vs_ref, kw_ref, vw_ref, kc_ref, vct_ref, wt_ref, bc_ref, gt_ref,
                qn_ref, kn_ref, cov_ref, oh_ref, o_ref, ksa, vst, kwn, vwt, *, hpg, n_cmp, n_top):
    qi = pl.program_id(2)
    cols_all = hpg * LANES
    scale = HEAD_DIM ** -0.5
    n_sel_blocks = cov_ref.shape[0]
    chunk_tiles = SEL_CHUNK // LANES
    seq = ks_ref.shape[0]

    @pl.when(qi == 0)
    def _():
        ksa[:, :HEAD_DIM] = _rms(ks_ref[...], kn_ref[...]).astype(BF16)
        ksa[:, HEAD_DIM:] = oh_ref[...]
        kwn[...] = _rms(kw_ref[...], kn_ref[...]).astype(BF16)
        for c in range(seq // LANES):
            blk = slice(c * LANES, (c + 1) * LANES)
            vst[:, blk] = vs_ref[blk, :].T.astype(BF16)
            vwt[:, blk] = vw_ref[blk, :].T.astype(BF16)

    qs = [_rms(q_ref[:, h * HEAD_DIM:(h + 1) * HEAD_DIM], qn_ref[...]).astype(BF16) for h in range(hpg)]
    q = jnp.concatenate(qs, axis=0) if hpg > 1 else qs[0]

    def bias_tile(d):
        off = pl.multiple_of((d + WIN_TILES) * LANES, LANES)
        return wt_ref[0, pl.ds(off, LANES), :]

    blk_row = lax.broadcasted_iota(jnp.int32, (LANES, cols_all), 0)
    qry = lax.broadcasted_iota(jnp.int32, (LANES, cols_all), 1) & (LANES - 1)

    lc = lax.dot_general(kc_ref[0, 0].astype(BF16), q, NT_DIMS, preferred_element_type=F32) * scale + bc_ref[0]
    mc = ((qi * LANES + qry - (blk_row * CMP_STRIDE + CMP_BLOCK - 1)) >= 0) & (blk_row < n_cmp)
    zc = jnp.where(mc, lc, NEG_INF)
    ec = jnp.exp(zc - jnp.max(zc, axis=0, keepdims=True))
    pc = ec * (1.0 / jnp.sum(ec, axis=0, keepdims=True)) * mc.astype(F32)
    o_cmp = jnp.dot(vct_ref[0, 0].astype(BF16), pc.astype(BF16), preferred_element_type=F32)

    ps = pc[:, 0:LANES]
    for h in range(1, hpg):
        ps = ps + pc[:, h * LANES:(h + 1) * LANES]
    p1 = ps.astype(BF16)
    r1 = ps - p1.astype(F32)
    p2 = r1.astype(BF16)
    p3 = (r1 - p2.astype(F32)).astype(BF16)
    cov = cov_ref[...]
    score = (jnp.dot(cov, p1, preferred_element_type=F32) + jnp.dot(cov, p2, preferred_element_type=F32)
             + jnp.dot(cov, p3, preferred_element_type=F32))
    jj = lax.broadcasted_iota(jnp.int32, (n_sel_blocks, LANES), 0)
    ql = lax.broadcasted_iota(jnp.int32, (n_sel_blocks, LANES), 1)
    cur = (LANES // SEL_BLOCK) * qi + ql // SEL_BLOCK
    forced = (jj == 0) | (jj == cur) | (jj == cur - 1)
    score = jnp.where(forced, FORCE_SCORE, jnp.where(jj > cur, -FORCE_SCORE, score))
    rank = jnp.zeros((n_sel_blocks, LANES), F32)
    for j2 in range(n_sel_blocks):
        other = score[j2:j2 + 1, :]
        rank += ((other > score) | ((other == score) & (j2 < jj))).astype(F32)
    sel_t = ((rank < n_top) & (jj <= cur)).astype(BF16)
    sel_t = jnp.concatenate([sel_t, jnp.ones((LANES - n_sel_blocks, LANES), BF16)], axis=0)
    eye = (lax.broadcasted_iota(jnp.int32, (LANES, LANES), 0)
           == lax.broadcasted_iota(jnp.int32, (LANES, LANES), 1)).astype(BF16)
    selq = lax.dot_general(eye, sel_t, NT_DIMS, preferred_element_type=F32)
    sel_neg = ((selq - 1.0) * (-NEG_INF)).astype(BF16)
    sel_neg = jnp.concatenate([sel_neg] * hpg, axis=0) if hpg > 1 else sel_neg
    qa = jnp.concatenate([q, sel_neg], axis=1)

    def sel_chunk(c, carry, near):
        m, l, acc = carry
        off = pl.multiple_of(c * SEL_CHUNK, SEL_CHUNK)
        s = lax.dot_general(ksa[pl.ds(off, SEL_CHUNK), :], qa, NT_DIMS, preferred_element_type=F32) * scale
        if near:
            s = jnp.concatenate([s[u * LANES:(u + 1) * LANES]
                                 + bias_tile(jnp.clip(qi - (c * chunk_tiles + u), 0, 2))
                                 for u in range(chunk_tiles)], axis=0)
        m_new = jnp.maximum(m, jnp.max(s, axis=0, keepdims=True))
        alpha = jnp.exp(m - m_new)
        p = jnp.exp(s - m_new)
        l = alpha * l + jnp.sum(p, axis=0, keepdims=True)
        acc = alpha * acc + jnp.dot(vst[:, pl.ds(off, SEL_CHUNK)], p.astype(BF16), preferred_element_type=F32)
        return m_new, l, acc

    init = (jnp.full((1, cols_all), NEG_INF, F32), jnp.zeros((1, cols_all), F32),
            jnp.zeros((HEAD_DIM, cols_all), F32))
    n_far = (jnp.maximum(qi, 1) - 1) // chunk_tiles
    carry = lax.fori_loop(0, n_far, functools.partial(sel_chunk, near=False), init)
    _, l_s, a_s = lax.fori_loop(n_far, qi // chunk_tiles + 1, functools.partial(sel_chunk, near=True), carry)

    win_keys = WINDOW + LANES
    kt0 = jnp.maximum(qi - WIN_TILES, 0)
    woff = pl.multiple_of(kt0 * LANES, LANES)
    sw = lax.dot_general(kwn[pl.ds(woff, win_keys), :], q, NT_DIMS, preferred_element_type=F32) * scale
    zw = jnp.concatenate([sw[u * LANES:(u + 1) * LANES] + bias_tile(qi - (kt0 + u))
                          for u in range(win_keys // LANES)], axis=0)
    pw = jnp.exp(zw - jnp.max(zw, axis=0, keepdims=True))
    l_w = jnp.sum(pw, axis=0, keepdims=True)
    a_w = jnp.dot(vwt[:, pl.ds(woff, win_keys)], pw.astype(BF16), preferred_element_type=F32)

    gt = jax.nn.sigmoid(gt_ref[...]).T
    inv_s = 1.0 / l_s
    inv_w = 1.0 / l_w
    for h in range(hpg):
        sl = slice(h * LANES, (h + 1) * LANES)
        c0 = h * N_BRANCH
        o = (gt[c0:c0 + 1] * o_cmp[:, sl] + (gt[c0 + 1:c0 + 2] * inv_s[:, sl]) * a_s[:, sl]
             + (gt[c0 + 2:c0 + 3] * inv_w[:, sl]) * a_w[:, sl])
        o_ref[:, h * HEAD_DIM:(h + 1) * HEAD_DIM] = o.T.astype(o_ref.dtype)


def _cover_t(seq):
    nc = seq // CMP_STRIDE - CMP_BLOCK // CMP_STRIDE + 1
    ns = seq // SEL_BLOCK
    c_start = np.arange(nc) * CMP_STRIDE
    c_end = c_start + CMP_BLOCK - 1
    s_start = np.arange(ns) * SEL_BLOCK
    cover = (c_start[:, None] <= s_start[None, :] + SEL_BLOCK - 1) & (c_end[:, None] >= s_start[None, :])
    out = np.zeros((ns, seq // CMP_STRIDE), np.float32)
    out[:, :nc] = cover.T
    return out


def nsa_attention(proj, gates, kc, vc, tables, q_norm, k_norm, batch, seq, hpg, blk):
    heads = KV_HEADS * hpg
    nq = seq // LANES
    nch = seq // CMP_STRIDE
    assert nch == LANES, "compressed keys must fill exactly one lane tile"
    n_cmp = nch - CMP_BLOCK // CMP_STRIDE + 1
    ns = seq // SEL_BLOCK
    assert seq % SEL_CHUNK == 0 and ns <= LANES
    cov = jnp.asarray(_cover_t(seq), BF16)
    onehot = jnp.asarray(np.arange(seq)[:, None] // SEL_BLOCK == np.arange(LANES)[None, :], BF16)
    qw = hpg * HEAD_DIM
    q_blk = blk["q"] * HEAD_DIM // qw
    kv = lambda name: pl.BlockSpec((seq, HEAD_DIM), lambda b, g, i: (b, blk[name] + g))
    cspec = pl.BlockSpec((1, 1, nch, HEAD_DIM), lambda b, g, i: (b, g, 0, 0))
    vec = pl.BlockSpec((1, HEAD_DIM), lambda b, g, i: (0, 0))
    rows_all = hpg * LANES
    est = (2 * 4 * 4 * seq * HEAD_DIM + 5 * 2 * seq * HEAD_DIM + 2 * 2 * seq * LANES
           + 2 * 4 * hpg * (N_BIAS_TILES + 1) * LANES * LANES + 6 * 4 * rows_all * (WINDOW + LANES) + (8 << 20))
    return pl.pallas_call(
        functools.partial(_nsa_kernel, hpg=hpg, n_cmp=n_cmp, n_top=min(N_SELECT, ns)),
        grid=(batch, KV_HEADS, nq),
        in_specs=[pl.BlockSpec((LANES, qw), lambda b, g, i: (b * nq + i, q_blk + g)),
                  kv("ks"), kv("vs"), kv("kw"), kv("vw"), cspec, cspec,
                  pl.BlockSpec((1, N_BIAS_TILES * LANES, qw), lambda b, g, i: (g, 0, 0)),
                  pl.BlockSpec((1, LANES, qw), lambda b, g, i: (g, N_BIAS_TILES + i, 0)),
                  pl.BlockSpec((LANES, LANES), lambda b, g, i: (b * nq + i, g)),
                  vec, vec,
                  pl.BlockSpec((ns, nch), lambda b, g, i: (0, 0)),
                  pl.BlockSpec((seq, LANES), lambda b, g, i: (0, 0))],
        out_specs=pl.BlockSpec((LANES, qw), lambda b, g, i: (b * nq + i, g)),
        out_shape=jax.ShapeDtypeStruct((batch * seq, heads * HEAD_DIM), BF16),
        scratch_shapes=[pltpu.VMEM((seq, 2 * HEAD_DIM), BF16), pltpu.VMEM((HEAD_DIM, seq), BF16),
                        pltpu.VMEM((seq, HEAD_DIM), BF16), pltpu.VMEM((HEAD_DIM, seq), BF16)],
        compiler_params=_params(("parallel", "parallel", "arbitrary"), est),
        name="nsa_attention",
    )(proj, proj, proj, proj, proj, kc, vc, tables, tables, gates, q_norm, k_norm, cov, onehot)


def _pool_kernel(x_ref, halo_ref, gm_ref, wp_ref, ps_ref, gf_ref, wrh_ref, wrl_ref, x_out, route_out,
                 *, tq, seq, cg):
    i = pl.program_id(0)
    start = (i * tq) % seq
    x = x_ref[...]
    h = _rms(x, gm_ref[...])
    hh = jnp.where(start == 0, 0.0, _rms(halo_ref[...], gm_ref[...]))
    t1 = (start + lax.broadcasted_iota(jnp.int32, (tq, 1), 0) + 1).astype(F32)
    ys = []
    for gi, w in enumerate(POOL_WINDOWS):
        sl = slice(gi * cg, (gi + 1) * cg)
        s = jnp.concatenate([hh[:, sl], h[:, sl]], axis=0)
        span = 1
        while span < w:
            s = s + pltpu.roll(s, span, 0)
            span *= 2
        dm = s[POOL_HALO:] * (1.0 / jnp.minimum(t1, float(w))) - h[:, sl]
        ys.append(jnp.dot(dm.astype(BF16), wp_ref[gi], preferred_element_type=F32))
    x3 = x + jnp.concatenate(ys, axis=1) * ps_ref[...]
    x_out[...] = x3

    h4 = _rms(x3, gf_ref[...])
    h_hi = h4.astype(BF16)
    h_lo = (h4 - h_hi.astype(F32)).astype(BF16)
    logits = (jnp.dot(h_hi, wrh_ref[...], preferred_element_type=F32)
              + jnp.dot(h_hi, wrl_ref[...], preferred_element_type=F32)
              + jnp.dot(h_lo, wrh_ref[...], preferred_element_type=F32))
    lane = lax.broadcasted_iota(jnp.int32, (tq, LANES), 1)
    logits = jnp.where(lane < N_EXPERTS, logits, -jnp.inf)
    m1 = jnp.max(logits, axis=-1, keepdims=True)
    i1 = jnp.min(jnp.where(logits == m1, lane, LANES), axis=-1, keepdims=True)
    rest = jnp.where(lane == i1, -jnp.inf, logits)
    m2 = jnp.max(rest, axis=-1, keepdims=True)
    i2 = jnp.min(jnp.where(rest == m2, lane, LANES), axis=-1, keepdims=True)
    e2 = jnp.exp(m2 - m1)
    den = 1.0 + e2
    route = jnp.where(lane == 0, i1.astype(F32),
                      jnp.where(lane == 1, i2.astype(F32),
                                jnp.where(lane == 2, 1.0 / den, jnp.where(lane == 3, e2 / den, 0.0))))
    route_out[...] = route


def pool_and_route(x, g_mix, w_pool, pool_scale, g_ffn, w_router, seq, tq=256):
    m, d = x.shape
    cg = d // len(POOL_WINDOWS)
    tq = _pick(seq, tq, POOL_HALO)
    hb = tq // POOL_HALO
    vec = pl.BlockSpec((1, d), lambda i: (0, 0))
    wr = _pad_to(w_router, 1, LANES)
    wr_hi = wr.astype(BF16)
    wr_lo = (wr - wr_hi.astype(F32)).astype(BF16)
    wspec = pl.BlockSpec((d, LANES), lambda i: (0, 0))
    est = 2 * 2 * 4 * tq * d + 2 * 2 * len(POOL_WINDOWS) * cg * cg + 8 * 4 * tq * d + 4 * 2 * d * LANES
    return pl.pallas_call(
        functools.partial(_pool_kernel, tq=tq, seq=seq, cg=cg),
        grid=(m // tq,),
        in_specs=[pl.BlockSpec((tq, d), lambda i: (i, 0)),
                  pl.BlockSpec((POOL_HALO, d), lambda i: (jnp.maximum(i * hb - 1, 0), 0)),
                  vec,
                  pl.BlockSpec((len(POOL_WINDOWS), cg, cg), lambda i: (0, 0, 0)),
                  vec, vec, wspec, wspec],
        out_specs=[pl.BlockSpec((tq, d), lambda i: (i, 0)), pl.BlockSpec((tq, LANES), lambda i: (i, 0))],
        out_shape=[jax.ShapeDtypeStruct((m, d), F32), jax.ShapeDtypeStruct((m, LANES), F32)],
        compiler_params=_params(("parallel",), est),
        name="pool_mixer_router",
    )(x, x, g_mix, w_pool, pool_scale, g_ffn, wr_hi, wr_lo)


def _route_tables(route, tm, n_tiles):
    n = route.shape[0]
    i1 = route[:, 0].astype(jnp.int32)
    i2 = route[:, 1].astype(jnp.int32)
    onehot = jax.nn.one_hot(i1, N_EXPERTS, dtype=jnp.int32) + jax.nn.one_hot(i2, N_EXPERTS, dtype=jnp.int32)
    count = jnp.sum(onehot, axis=0)
    padded = (count + tm - 1) // tm * tm
    end = jnp.cumsum(padded)
    pos = (end - padded)[None, :] + jnp.cumsum(onehot, axis=0) - onehot
    p1 = jnp.take_along_axis(pos, i1[:, None], axis=1)[:, 0]
    p2 = jnp.take_along_axis(pos, i2[:, None], axis=1)[:, 0]
    tok = jnp.arange(n, dtype=jnp.int32)
    rows = n_tiles * tm
    row_token = jnp.zeros((rows,), jnp.int32).at[jnp.concatenate([p1, p2])].set(jnp.concatenate([tok, tok]))
    n_used = (end[-1] // tm).astype(jnp.int32)
    tile = jnp.arange(n_tiles, dtype=jnp.int32)
    tile_expert = jnp.searchsorted(end, tile * tm, side="right").astype(jnp.int32)
    tile_expert = jnp.where(tile < n_used, tile_expert, tile_expert[n_used - 1])
    tile_rows = jnp.clip((count - padded + end)[tile_expert] - tile * tm, 0, tm)
    tile_rows = jnp.where(tile < n_used, tile_rows, 0).astype(jnp.int32)
    ids = jnp.arange(N_EXPERTS, dtype=jnp.int32)
    later = jnp.where((ids[None, :] > ids[:, None]) & (count[None, :] > 0), ids[None, :], N_EXPERTS)
    following = jnp.min(later, axis=1)
    next_expert = jnp.where(following < N_EXPERTS, following, -1)[tile_expert].astype(jnp.int32)
    return (tile_expert, tile_rows, next_expert, row_token, n_used.reshape(1),
            p1.astype(jnp.int32), p2.astype(jnp.int32))


GATHER_UNROLL = 8


def _gather_start(idx_ref, base, src_hbm, dst_ref, sem, count):
    assert count % GATHER_UNROLL == 0

    def issue(g, carry):
        for k in range(GATHER_UNROLL):
            r = g * GATHER_UNROLL + k
            pltpu.make_async_copy(src_hbm.at[pl.ds(idx_ref[base + r], 1), :], dst_ref.at[pl.ds(r, 1), :],
                                  sem).start(priority=k % 2)
        return carry

    lax.fori_loop(0, count // GATHER_UNROLL, issue, 0)


def _gather_wait(src_hbm, dst_ref, sem, count):
    pltpu.make_async_copy(src_hbm.at[pl.ds(0, count), :], dst_ref, sem).wait()


MOE_SUB = 256


def _moe_gather_kernel(rt_ref, nu_ref, x_hbm, g_ref, o_ref, xg_ref, sem, *, tm):
    i = pl.program_id(0)
    valid = i < nu_ref[0]
    slot = i % 2

    @pl.when(i == 0)
    def _():
        _gather_start(rt_ref, 0, x_hbm, xg_ref.at[0], sem.at[0], tm)

    @pl.when(i + 1 < nu_ref[0])
    def _():
        _gather_start(rt_ref, (i + 1) * tm, x_hbm, xg_ref.at[1 - slot], sem.at[1 - slot], tm)

    @pl.when(valid)
    def _():
        _gather_wait(x_hbm, xg_ref.at[slot], sem.at[slot], tm)
        o_ref[...] = _rms(xg_ref[slot], g_ref[...]).astype(o_ref.dtype)

    @pl.when(jnp.logical_not(valid))
    def _():
        o_ref[...] = jnp.zeros_like(o_ref)


def moe_gather(x, g_ffn, row_token, n_used, tm):
    d = x.shape[1]
    n_tiles = row_token.shape[0] // tm
    return pl.pallas_call(
        functools.partial(_moe_gather_kernel, tm=tm),
        grid_spec=pltpu.PrefetchScalarGridSpec(
            num_scalar_prefetch=2,
            grid=(n_tiles,),
            in_specs=[pl.BlockSpec(memory_space=pl.ANY), pl.BlockSpec((1, d), lambda i, rt, nu: (0, 0))],
            out_specs=pl.BlockSpec((tm, d), lambda i, rt, nu: (i, 0)),
            scratch_shapes=[pltpu.VMEM((2, tm, d), F32), pltpu.SemaphoreType.DMA((2,))]),
        out_shape=jax.ShapeDtypeStruct((n_tiles * tm, d), BF16),
        compiler_params=_params(("arbitrary",), 2 * 4 * tm * d + 2 * 2 * tm * d + 2 * 4 * tm * d),
        name="moe_gather",
    )(row_token, n_used, x, g_ffn)


def _stage_expert_weights(te_ref, nu_ref, nx_ref, w_hbm, stage, work, sem, width):
    sweep = pl.program_id(0)
    i = pl.program_id(1)

    def copies(e, col_tile):
        cols = pl.ds(pl.multiple_of(col_tile * width, width), width)
        return [pltpu.make_async_copy(w.at[e, :, cols], s, sem.at[k]) for k, (w, s) in enumerate(zip(w_hbm, stage))]

    def start(e, col_tile):
        for c in copies(e, col_tile):
            c.start()

    @pl.when((sweep == 0) & (i == 0))
    def _():
        start(te_ref[0], 0)

    @pl.when((i < nu_ref[0]) & ((i == 0) | (te_ref[i] != te_ref[jnp.maximum(i - 1, 0)])))
    def _():
        for c in copies(0, 0):
            c.wait()

        def cast_rows(r, carry):
            rows = pl.ds(pl.multiple_of(r * MOE_SUB, MOE_SUB), MOE_SUB)
            for s, b in zip(stage, work):
                b[rows, :] = s[rows, :].astype(BF16)
            return carry

        lax.fori_loop(0, stage[0].shape[0] // MOE_SUB, cast_rows, 0)
        nxt = nx_ref[i]

        @pl.when(nxt >= 0)
        def _():
            start(nxt, sweep)

        @pl.when((nxt < 0) & (sweep + 1 < pl.num_programs(0)))
        def _():
            start(te_ref[0], sweep + 1)


def _moe_glu_kernel(te_ref, tr_ref, nu_ref, nx_ref, h_ref, w1_hbm, w3_hbm, o_ref, s1, s3, w1b, w3b, sem, *, tm, tf):
    i = pl.program_id(1)
    valid = i < nu_ref[0]
    _stage_expert_weights(te_ref, nu_ref, nx_ref, [w1_hbm, w3_hbm], [s1, s3], [w1b, w3b], sem, tf)

    for sb in range(tm // MOE_SUB):
        rows = pl.ds(sb * MOE_SUB, MOE_SUB)
        live = valid & (sb * MOE_SUB < tr_ref[i])

        @pl.when(live)
        def _():
            h = h_ref[rows, :]
            a = jnp.dot(h, w1b[...], preferred_element_type=F32)
            b = jnp.dot(h, w3b[...], preferred_element_type=F32)
            o_ref[rows, :] = (a * jax.nn.sigmoid(a) * b).astype(o_ref.dtype)

        @pl.when(jnp.logical_not(live))
        def _():
            o_ref[rows, :] = jnp.zeros((MOE_SUB, o_ref.shape[1]), o_ref.dtype)


def moe_glu(h, w1, w3, tile_expert, tile_rows, n_used, next_expert, tm, tf):
    d = h.shape[1]
    ff = w1.shape[2]
    tf = _pick(ff, tf)
    n_tiles = tile_expert.shape[0]
    est = 2 * 4 * d * tf + 2 * 2 * d * tf + 2 * 2 * tm * d + 2 * 2 * tm * tf + 4 * 4 * MOE_SUB * tf + 4 * d * tf
    return pl.pallas_call(
        functools.partial(_moe_glu_kernel, tm=tm, tf=tf),
        grid_spec=pltpu.PrefetchScalarGridSpec(
            num_scalar_prefetch=4,
            grid=(ff // tf, n_tiles),
            in_specs=[pl.BlockSpec((tm, d), lambda f, i, te, tr, nu, nx: (jnp.minimum(i, nu[0] - 1), 0)),
                      pl.BlockSpec(memory_space=pl.ANY), pl.BlockSpec(memory_space=pl.ANY)],
            out_specs=pl.BlockSpec((tm, tf), lambda f, i, te, tr, nu, nx: (i, f)),
            scratch_shapes=[pltpu.VMEM((d, tf), F32), pltpu.VMEM((d, tf), F32),
                            pltpu.VMEM((d, tf), BF16), pltpu.VMEM((d, tf), BF16),
                            pltpu.SemaphoreType.DMA((2,))]),
        out_shape=jax.ShapeDtypeStruct((n_tiles * tm, ff), BF16),
        compiler_params=_params(("arbitrary", "arbitrary"), est),
        name="moe_glu",
    )(tile_expert, tile_rows, n_used, next_expert, h, w1, w3)


def _moe_down_kernel(te_ref, tr_ref, nu_ref, nx_ref, a_ref, w_hbm, o_ref, s2, w2b, sem, *, tm, tn):
    i = pl.program_id(1)
    valid = i < nu_ref[0]
    _stage_expert_weights(te_ref, nu_ref, nx_ref, [w_hbm], [s2], [w2b], sem, tn)
    for sb in range(tm // MOE_SUB):
        rows = pl.ds(sb * MOE_SUB, MOE_SUB)
        live = valid & (sb * MOE_SUB < tr_ref[i])

        @pl.when(live)
        def _():
            o_ref[rows, :] = jnp.dot(a_ref[rows, :], w2b[...], preferred_element_type=F32)

        @pl.when(jnp.logical_not(live))
        def _():
            o_ref[rows, :] = jnp.zeros((MOE_SUB, o_ref.shape[1]), o_ref.dtype)


def moe_down(act, w2, tile_expert, tile_rows, n_used, next_expert, tm, tn):
    ff = act.shape[1]
    d = w2.shape[2]
    tn = _pick(d, tn)
    n_tiles = tile_expert.shape[0]
    est = 4 * ff * tn + 2 * ff * tn + 2 * 2 * tm * ff + 2 * 4 * tm * tn + 4 * MOE_SUB * tn + 4 * MOE_SUB * tn
    return pl.pallas_call(
        functools.partial(_moe_down_kernel, tm=tm, tn=tn),
        grid_spec=pltpu.PrefetchScalarGridSpec(
            num_scalar_prefetch=4,
            grid=(d // tn, n_tiles),
            in_specs=[pl.BlockSpec((tm, ff), lambda j, i, te, tr, nu, nx: (jnp.minimum(i, nu[0] - 1), 0)),
                      pl.BlockSpec(memory_space=pl.ANY)],
            out_specs=pl.BlockSpec((tm, tn), lambda j, i, te, tr, nu, nx: (i, j)),
            scratch_shapes=[pltpu.VMEM((ff, tn), F32), pltpu.VMEM((ff, tn), BF16),
                            pltpu.SemaphoreType.DMA((1,))]),
        out_shape=jax.ShapeDtypeStruct((n_tiles * tm, d), F32),
        compiler_params=_params(("arbitrary", "arbitrary"), est),
        name="moe_down",
    )(tile_expert, tile_rows, n_used, next_expert, act, w2)


def _moe_combine_kernel(p1_ref, p2_ref, x_ref, r_ref, y_hbm, o_ref, ya_ref, yb_ref, sem_a, sem_b, *, tq):
    i = pl.program_id(0)
    slot = i % 2

    def start(step, s):
        _gather_start(p1_ref, step * tq, y_hbm, ya_ref.at[s], sem_a.at[s], tq)
        _gather_start(p2_ref, step * tq, y_hbm, yb_ref.at[s], sem_b.at[s], tq)

    @pl.when(i == 0)
    def _():
        start(0, 0)

    @pl.when(i + 1 < pl.num_programs(0))
    def _():
        start(i + 1, 1 - slot)

    _gather_wait(y_hbm, ya_ref.at[slot], sem_a.at[slot], tq)
    _gather_wait(y_hbm, yb_ref.at[slot], sem_b.at[slot], tq)
    r = r_ref[...]
    o_ref[...] = x_ref[...] + (r[:, 2:3] * ya_ref[slot] + r[:, 3:4] * yb_ref[slot])


def moe_combine(x, route, y, p1, p2, tq=256):
    m, d = x.shape
    tq = _pick(m, tq, 8)
    return pl.pallas_call(
        functools.partial(_moe_combine_kernel, tq=tq),
        grid_spec=pltpu.PrefetchScalarGridSpec(
            num_scalar_prefetch=2,
            grid=(m // tq,),
            in_specs=[pl.BlockSpec((tq, d), lambda i, a, b: (i, 0)),
                      pl.BlockSpec((tq, LANES), lambda i, a, b: (i, 0)),
                      pl.BlockSpec(memory_space=pl.ANY)],
            out_specs=pl.BlockSpec((tq, d), lambda i, a, b: (i, 0)),
            scratch_shapes=[pltpu.VMEM((2, tq, d), F32), pltpu.VMEM((2, tq, d), F32),
                            pltpu.SemaphoreType.DMA((2,)), pltpu.SemaphoreType.DMA((2,))]),
        out_shape=jax.ShapeDtypeStruct((m, d), F32),
        compiler_params=_params(("arbitrary",), 9 * 4 * tq * d),
        name="moe_combine",
    )(p1, p2, x, route, y)


def _pad_to(a, axis, size):
    if a.shape[axis] == size:
        return a
    pad = [(0, 0)] * a.ndim
    pad[axis] = (0, size - a.shape[axis])
    return jnp.pad(a, pad)


def _even_layer(x2d, batch, seq, tables, norm_mix, w_in, conv_w, q_norm, k_norm, pe_k, pe_v, w_cmp_k, w_cmp_v,
                w_out, norm_ffn, w1, w3, w2):
    d = x2d.shape[1]
    ch = d // 2
    heads = ch // HEAD_DIM
    hpg = heads // KV_HEADS
    kv_w = KV_HEADS * HEAD_DIM
    main = 3 * ch + heads * HEAD_DIM + 6 * kv_w
    assert main + heads * N_BRANCH == w_in.shape[1]

    h = rmsnorm(x2d, norm_mix, BF16)
    proj = matmul([(h, w_in.astype(BF16), 0)], None, F32, 1024, 1024, "in_proj", n=main)
    wg = w_in[:, main:].reshape(d, KV_HEADS, hpg * N_BRANCH)
    wg = _pad_to(wg, 2, LANES).reshape(d, KV_HEADS * LANES).astype(BF16)
    gates = matmul([(h, wg, 0)], None, F32, 1024, KV_HEADS * LANES, "gate_proj")

    y_a = short_conv(proj, conv_w.T, ch, seq)
    base = (3 * ch + heads * HEAD_DIM) // HEAD_DIM
    step = kv_w // HEAD_DIM
    blk = {"q": 3 * ch // HEAD_DIM, "kc": base, "vc": base + step, "ks": base + 2 * step,
           "vs": base + 3 * step, "kw": base + 4 * step, "vw": base + 5 * step}
    kn = k_norm.reshape(1, HEAD_DIM)
    kc, vc = compress(proj, blk["kc"], blk["vc"], pe_k, pe_v, w_cmp_k.astype(BF16), w_cmp_v.astype(BF16),
                      kn, batch, seq)
    y_b = nsa_attention(proj, gates, kc, vc, tables, q_norm.reshape(1, HEAD_DIM), kn, batch, seq, hpg, blk)

    wo = w_out.astype(BF16)
    x2d = matmul([(y_a, wo, 0), (y_b, wo, 1)], x2d, F32, 1024, 512, "out_proj")

    h2 = rmsnorm(x2d, norm_ffn, BF16)
    act = glu_matmul(h2, w1, w3, 1024, 256)
    return matmul([(act, w2.astype(BF16), 0)], x2d, F32, 512, 512, "dense_down")


def _odd_layer(x2d, seq, norm_mix, w_pool, pool_scale, norm_ffn, w_router, w1, w3, w2):
    m, d = x2d.shape
    x3, route = pool_and_route(x2d, norm_mix.reshape(1, d), w_pool.astype(BF16), pool_scale.reshape(1, d),
                               norm_ffn.reshape(1, d), w_router, seq)
    tm = min(512, m // 8)
    n_tiles = 2 * m // tm + N_EXPERTS
    tile_expert, tile_rows, next_expert, row_token, n_used, p1, p2 = _route_tables(route, tm, n_tiles)
    hs = moe_gather(x3, norm_ffn.reshape(1, d), row_token, n_used, tm)
    act = moe_glu(hs, w1, w3, tile_expert, tile_rows, n_used, next_expert, tm, 512)
    y = moe_down(act, w2, tile_expert, tile_rows, n_used, next_expert, tm, 512)
    return moe_combine(x3, route, y, p1, p2)


def kernel(x, norm_mix_even, w_in_even, conv_w_even, q_norm_even, k_norm_even, cmp_pe_k_even, cmp_pe_v_even,
           w_cmp_k_even, w_cmp_v_even, w_out_even, norm_ffn_even, w1_dense, w3_dense, w2_dense,
           norm_mix_odd, w_pool_odd, pool_scale_odd, norm_ffn_odd, w_router_odd,
           w1_moe, w3_moe, w2_moe, rel_bias):
    batch, seq, d = x.shape
    depth = norm_mix_even.shape[0] + norm_mix_odd.shape[0]
    tables = bias_tables(rel_bias, seq)
    x2d = x.reshape(batch * seq, d)
    for layer in range(depth):
        i = layer // 2
        if layer % 2 == 0:
            x2d = _even_layer(x2d, batch, seq, tables, norm_mix_even[i], w_in_even[i], conv_w_even[i],
                              q_norm_even[i], k_norm_even[i], cmp_pe_k_even[i], cmp_pe_v_even[i],
                              w_cmp_k_even[i], w_cmp_v_even[i], w_out_even[i], norm_ffn_even[i],
                              w1_dense[i], w3_dense[i], w2_dense[i])
        else:
            x2d = _odd_layer(x2d, seq, norm_mix_odd[i], w_pool_odd[i], pool_scale_odd[i], norm_ffn_odd[i],
                             w_router_odd[i], w1_moe[i], w3_moe[i], w2_moe[i])
    return x2d.reshape(batch, seq, d)
```
